```python
import math
import jax, jax.numpy as jnp
from jax import lax
import numpy as np

D_MODEL = 2048
BATCH = 4
SEQ = 2048
DEPTH = 4
DEC_BATCH = 8
DEC_SEQ = 32
PAST_LEN = 2048

CHUNK = 64
Q_BLOCK = 128
N_MEM = 256
EPS = 1e-6
NEG_INF = -1e30

SSM_WIDTH = 1024
SSM_GROUP = 16
SSM_GROUPS = SSM_WIDTH // SSM_GROUP
SSM_STATE = 64

MLA_HEADS = 8
QK_NOPE = 128
QK_ROPE = 64
V_HEAD = 128
Q_LORA = 512
KV_LORA = 512
MLA_WIDTH = MLA_HEADS * V_HEAD
ROPE_THETA = 10000.0

MEM_HEADS = 4
MEM_HEAD_DIM = 128
MEM_WIDTH = MEM_HEADS * MEM_HEAD_DIM

N_BRANCH = 3
IN_SPLITS = (SSM_WIDTH, SSM_WIDTH, Q_LORA, KV_LORA, QK_ROPE, MLA_WIDTH, MEM_WIDTH, MEM_WIDTH, N_BRANCH * D_MODEL)
D_IN = SSM_WIDTH * 2 + Q_LORA + KV_LORA + QK_ROPE + MLA_WIDTH + MEM_WIDTH * 2 + N_BRANCH * D_MODEL

kernel_name = "hybrid_s5_mla_memory_stream_step"


def rms_norm(x, g):
    xf = x.astype(jnp.float32)
    y = xf * lax.rsqrt(jnp.mean(xf * xf, axis=-1, keepdims=True) + EPS)
    return (y * g.astype(jnp.float32)).astype(x.dtype)


def split_columns(p):
    offs = [int(o) for o in np.cumsum(np.array(IN_SPLITS))[:-1]]
    return jnp.split(p, offs, axis=-1)


def rope(x, pos):
    half = x.shape[-1] // 2
    inv = ROPE_THETA ** (-jnp.arange(half, dtype=jnp.float32) / half)
    ang = pos.astype(jnp.float32)[:, None] * inv[None, :]
    if x.ndim == 4:
        ang = ang[:, None, :]
    cos, sin = jnp.cos(ang), jnp.sin(ang)
    xf = x.astype(jnp.float32)
    x1, x2 = xf[..., :half], xf[..., half:]
    return jnp.concatenate([x1 * cos - x2 * sin, x1 * sin + x2 * cos], axis=-1).astype(x.dtype)


def s5_discretize(a_re, a_im, log_dt, b_re, b_im):
    dt = jnp.exp(log_dt.astype(jnp.float32))[:, None]
    lr, li = a_re.astype(jnp.float32), a_im.astype(jnp.float32)
    mag = jnp.exp(lr * dt)
    ab_re, ab_im = mag * jnp.cos(li * dt), mag * jnp.sin(li * dt)
    den = lr * lr + li * li
    nr, ni = ab_re - 1.0, ab_im
    f_re = (nr * lr + ni * li) / den
    f_im = (ni * lr - nr * li) / den
    br, bi = b_re.astype(jnp.float32), b_im.astype(jnp.float32)
    bb_re = f_re[..., None] * br - f_im[..., None] * bi
    bb_im = f_re[..., None] * bi + f_im[..., None] * br
    return ab_re, ab_im, bb_re, bb_im


def _complex_affine_combine(e1, e2):
    a1r, a1i, b1r, b1i = e1
    a2r, a2i, b2r, b2i = e2
    return (a2r * a1r - a2i * a1i,
            a2r * a1i + a2i * a1r,
            a2r * b1r - a2i * b1i + b2r,
            a2r * b1i + a2i * b1r + b2i)


def s5_branch(u, z, h0_re, h0_im, a_re, a_im, log_dt, b_re, b_im, c_re, c_im, d_skip, w_glu):
    Bsz, S, _ = u.shape
    uf = u.astype(jnp.float32)
    ab_re, ab_im, bb_re, bb_im = s5_discretize(a_re, a_im, log_dt, b_re, b_im)
    ug = uf.reshape(Bsz, S, SSM_GROUPS, SSM_GROUP)
    bu_re = jnp.einsum('bsgc,gpc->bsgp', ug, bb_re)
    bu_im = jnp.einsum('bsgc,gpc->bsgp', ug, bb_im)
    h0r, h0i = h0_re.astype(jnp.float32), h0_im.astype(jnp.float32)
    bu_re = bu_re.at[:, 0].add(ab_re * h0r - ab_im * h0i)
    bu_im = bu_im.at[:, 0].add(ab_re * h0i + ab_im * h0r)
    a_r = jnp.broadcast_to(ab_re, bu_re.shape)
    a_i = jnp.broadcast_to(ab_im, bu_im.shape)
    _, _, h_re, h_im = lax.associative_scan(_complex_affine_combine, (a_r, a_i, bu_re, bu_im), axis=1)
    y = (jnp.einsum('bsgp,gcp->bsgc', h_re, c_re.astype(jnp.float32))
         - jnp.einsum('bsgp,gcp->bsgc', h_im, c_im.astype(jnp.float32)))
    y = y.reshape(Bsz, S, SSM_WIDTH) + d_skip.astype(jnp.float32) * uf
    y = jax.nn.gelu(y)
    ga, gb = jnp.split(y @ w_glu.astype(jnp.float32), 2, axis=-1)
    y = ga * jax.nn.sigmoid(gb)
    y = y * jax.nn.silu(z.astype(jnp.float32))
    return y, h_re[:, -1], h_im[:, -1]


def _mla_block(q_nope, q_rope, q_pos, k_nope, k_rope, v, k_pos):
    scale = (QK_NOPE + QK_ROPE) ** -0.5
    s = (jnp.einsum('bqhd,bkhd->bhqk', q_nope, k_nope).astype(jnp.float32)
         + jnp.einsum('bqhr,bkr->bhqk', q_rope, k_rope).astype(jnp.float32)) * scale
    mask = (k_pos[None, :] // CHUNK) <= (q_pos[:, None] // CHUNK)
    s = jnp.where(mask[None, None], s, NEG_INF)
    p = jax.nn.softmax(s, axis=-1).astype(v.dtype)
    return jnp.einsum('bhqk,bkhd->bqhd', p, v)


def chunk_causal_mla_attention(q_nope, q_rope, k_nope, k_rope, v, q_pos, k_pos):
    Bsz, S = q_nope.shape[:2]
    if S <= Q_BLOCK:
        return _mla_block(q_nope, q_rope, q_pos, k_nope, k_rope, v, k_pos)
    nb = S // Q_BLOCK
    qn = q_nope.reshape(Bsz, nb, Q_BLOCK, MLA_HEADS, QK_NOPE).transpose(1, 0, 2, 3, 4)
    qr = q_rope.reshape(Bsz, nb, Q_BLOCK, MLA_HEADS, QK_ROPE).transpose(1, 0, 2, 3, 4)
    qp = q_pos.reshape(nb, Q_BLOCK)

    def one_block(args):
        bqn, bqr, bqp = args
        return _mla_block(bqn, bqr, bqp, k_nope, k_rope, v, k_pos)

    o = lax.map(one_block, (qn, qr, qp))
    return o.transpose(1, 0, 2, 3, 4).reshape(Bsz, S, MLA_HEADS, V_HEAD)


def mla_branch(c_q, c_kv, k_rope_raw, z, q_pos, ckv_past, krope_past, q_norm, w_uq, kv_norm, w_uk, w_uv):
    Bsz, S, _ = c_q.shape
    q = (rms_norm(c_q, q_norm) @ w_uq).reshape(Bsz, S, MLA_HEADS, QK_NOPE + QK_ROPE)
    q_nope = q[..., :QK_NOPE]
    q_rope = rope(q[..., QK_NOPE:], q_pos)
    ckv_new = rms_norm(c_kv, kv_norm)
    krope_new = rope(k_rope_raw, q_pos)
    if ckv_past is None:
        ckv, krope, k_pos = ckv_new, krope_new, q_pos
    else:
        ckv = jnp.concatenate([ckv_past.astype(ckv_new.dtype), ckv_new], axis=1)
        krope = jnp.concatenate([krope_past.astype(krope_new.dtype), krope_new], axis=1)
        k_pos = jnp.arange(ckv.shape[1], dtype=jnp.int32)
    Sk = ckv.shape[1]
    k_nope = (ckv @ w_uk).reshape(Bsz, Sk, MLA_HEADS, QK_NOPE)
    v = (ckv @ w_uv).reshape(Bsz, Sk, MLA_HEADS, V_HEAD)
    o = chunk_causal_mla_attention(q_nope, q_rope, k_nope, krope, v, q_pos, k_pos)
    o = o.reshape(Bsz, S, MLA_WIDTH) * jax.nn.silu(z.astype(jnp.float32)).astype(o.dtype)
    return o, ckv_new, krope_new


def memory_branch(q, z, mem_k, mem_v):
    Bsz, S, _ = q.shape
    qh = q.reshape(Bsz, S, MEM_HEADS, MEM_HEAD_DIM)
    s = jnp.einsum('bqhd,bmhd->bhqm', qh, mem_k.astype(qh.dtype)).astype(jnp.float32) * (MEM_HEAD_DIM ** -0.5)
    p = jax.nn.softmax(s, axis=-1).astype(qh.dtype)
    o = jnp.einsum('bhqm,bmhd->bqhd', p, mem_v.astype(qh.dtype)).reshape(Bsz, S, MEM_WIDTH)
    return o * jax.nn.silu(z.astype(jnp.float32)).astype(o.dtype)


def setup_inputs(seed: int = 0) -> dict:
    key = jax.random.key(seed)
    ks = iter(jax.random.split(key, 48))
    f32 = jnp.float32

    def nrm(shape, scale=1.0):
        return jax.random.normal(next(ks), shape, f32) * scale

    def gain(shape):
        return 1.0 + 0.05 * jax.random.normal(next(ks), shape, f32)

    G, P, C = SSM_GROUPS, SSM_STATE, SSM_GROUP
    return {
        "x_prompt": nrm((BATCH, SEQ, D_MODEL)),
        "x_sample": nrm((DEC_BATCH, DEC_SEQ, D_MODEL)),
        "cache_mla_ckv": nrm((DEPTH, DEC_BATCH, PAST_LEN, KV_LORA)),
        "cache_mla_krope": nrm((DEPTH, DEC_BATCH, PAST_LEN, QK_ROPE)),
        "cache_mem_k": nrm((DEPTH, DEC_BATCH, N_MEM, MEM_HEADS, MEM_HEAD_DIM)),
        "cache_mem_v": nrm((DEPTH, DEC_BATCH, N_MEM, MEM_HEADS, MEM_HEAD_DIM)),
        "state_ssm_re": nrm((DEPTH, DEC_BATCH, G, P), 0.2),
        "state_ssm_im": nrm((DEPTH, DEC_BATCH, G, P), 0.2),
        "mem_prompt": nrm((BATCH, N_MEM, D_MODEL)),
        "norm_pre": gain((DEPTH, D_MODEL)),
        "w_in": nrm((DEPTH, D_MODEL, D_IN), D_MODEL ** -0.5),
        "ssm_a_re": -0.5 + 0.01 * nrm((DEPTH, G, P)),
        "ssm_a_im": math.pi * jnp.arange(P, dtype=f32) + 0.01 * nrm((DEPTH, G, P)),
        "ssm_log_dt": jax.random.uniform(next(ks), (DEPTH, G), f32, math.log(1e-3), math.log(1e-1)),
        "ssm_b_re": nrm((DEPTH, G, P, C), (2.0 * C) ** -0.5),
        "ssm_b_im": nrm((DEPTH, G, P, C), (2.0 * C) ** -0.5),
        "ssm_c_re": nrm((DEPTH, G, C, P), 2.0 * P ** -0.5),
        "ssm_c_im": nrm((DEPTH, G, C, P), 2.0 * P ** -0.5),
        "ssm_d": nrm((DEPTH, SSM_WIDTH), 0.5),
        "w_glu": nrm((DEPTH, SSM_WIDTH, 2 * SSM_WIDTH), SSM_WIDTH ** -0.5),
        "mla_q_norm": gain((DEPTH, Q_LORA)),
        "w_uq": nrm((DEPTH, Q_LORA, MLA_HEADS * (QK_NOPE + QK_ROPE)), Q_LORA ** -0.5),
        "mla_kv_norm": gain((DEPTH, KV_LORA)),
        "w_uk": nrm((DEPTH, KV_LORA, MLA_HEADS * QK_NOPE), KV_LORA ** -0.5),
        "w_uv": nrm((DEPTH, KV_LORA, MLA_HEADS * V_HEAD), KV_LORA ** -0.5),
        "w_mem_k": nrm((DEPTH, D_MODEL, MEM_WIDTH), D_MODEL ** -0.5),
        "w_mem_v": nrm((DEPTH, D_MODEL, MEM_WIDTH), D_MODEL ** -0.5),
        "w_ssm_o": nrm((DEPTH, SSM_WIDTH, D_MODEL), SSM_WIDTH ** -0.5),
        "w_mla_o": nrm((DEPTH, MLA_WIDTH, D_MODEL), MLA_WIDTH ** -0.5),
        "w_mem_o": nrm((DEPTH, MEM_WIDTH, D_MODEL), MEM_WIDTH ** -0.5),
        "w_out": nrm((DEPTH, D_MODEL, D_MODEL), D_MODEL ** -0.5),
        "norm_post": gain((DEPTH, D_MODEL)),
    }


def reference(x_prompt, x_sample, cache_mla_ckv, cache_mla_krope, cache_mem_k, cache_mem_v,
              state_ssm_re, state_ssm_im, mem_prompt,
              norm_pre, w_in, ssm_a_re, ssm_a_im, ssm_log_dt, ssm_b_re, ssm_b_im,
              ssm_c_re, ssm_c_im, ssm_d, w_glu, mla_q_norm, w_uq, mla_kv_norm, w_uk, w_uv,
              w_mem_k, w_mem_v, w_ssm_o, w_mla_o, w_mem_o, w_out, norm_post):

    def layer(l, x, mem_k, mem_v, h0_re, h0_im, ckv_past, krope_past, q_pos):
        Bsz, S, _ = x.shape
        h = rms_norm(x, norm_pre[l])
        u, z_ssm, c_q, c_kv, k_rope_raw, z_mla, q_mem, z_mem, gate_logits = split_columns(h @ w_in[l])
        y_ssm, h_re, h_im = s5_branch(u, z_ssm, h0_re, h0_im, ssm_a_re[l], ssm_a_im[l], ssm_log_dt[l],
                                      ssm_b_re[l], ssm_b_im[l], ssm_c_re[l], ssm_c_im[l], ssm_d[l], w_glu[l])
        y_mla, ckv_new, krope_new = mla_branch(c_q, c_kv, k_rope_raw, z_mla, q_pos, ckv_past, krope_past,
                                               mla_q_norm[l], w_uq[l], mla_kv_norm[l], w_uk[l], w_uv[l])
        y_mem = memory_branch(q_mem, z_mem, mem_k, mem_v)
        g = jax.nn.sigmoid(gate_logits.astype(jnp.float32)).reshape(Bsz, S, N_BRANCH, D_MODEL)
        merged = (g[:, :, 0] * (y_ssm.astype(x.dtype) @ w_ssm_o[l])
                  + g[:, :, 1] * (y_mla.astype(x.dtype) @ w_mla_o[l])
                  + g[:, :, 2] * (y_mem.astype(x.dtype) @ w_mem_o[l]))
        out = rms_norm(merged.astype(x.dtype) @ w_out[l], norm_post[l])
        return x + out.astype(x.dtype), h_re, h_im, ckv_new, krope_new

    bp, sp = x_prompt.shape[0], x_prompt.shape[1]
    pos_p = jnp.arange(sp, dtype=jnp.int32)
    h0 = jnp.zeros((bp, SSM_GROUPS, SSM_STATE), jnp.float32)
    xp = x_prompt
    p_re, p_im, p_ckv, p_kr, p_mk, p_mv = [], [], [], [], [], []
    for l in range(DEPTH):
        mk = (mem_prompt @ w_mem_k[l]).reshape(bp, N_MEM, MEM_HEADS, MEM_HEAD_DIM)
        mv = (mem_prompt @ w_mem_v[l]).reshape(bp, N_MEM, MEM_HEADS, MEM_HEAD_DIM)
        xp, hr, hi, ckv, kr = layer(l, xp, mk, mv, h0, h0, None, None, pos_p)
        p_re.append(hr); p_im.append(hi); p_ckv.append(ckv); p_kr.append(kr); p_mk.append(mk); p_mv.append(mv)

    past = cache_mla_ckv.shape[2]
    pos_s = past + jnp.arange(x_sample.shape[1], dtype=jnp.int32)
    xs = x_sample
    s_re, s_im, s_ckv, s_kr = [], [], [], []
    for l in range(DEPTH):
        xs, hr, hi, ckv, kr = layer(l, xs, cache_mem_k[l], cache_mem_v[l], state_ssm_re[l], state_ssm_im[l],
                                    cache_mla_ckv[l], cache_mla_krope[l], pos_s)
        s_re.append(hr); s_im.append(hi); s_ckv.append(ckv); s_kr.append(kr)

    return (xp, xs,
            jnp.stack(p_re), jnp.stack(p_im), jnp.stack(p_ckv), jnp.stack(p_kr), jnp.stack(p_mk), jnp.stack(p_mv),
            jnp.stack(s_re), jnp.stack(s_im), jnp.stack(s_ckv), jnp.stack(s_kr))
```

```python
import functools
import math

import jax
import jax.numpy as jnp
import numpy as np
from jax import lax
from jax.experimental import pallas as pl
from jax.experimental.pallas import tpu as pltpu

D_MODEL = 2048
BATCH = 4
SEQ = 2048
DEPTH = 4
DEC_BATCH = 8
DEC_SEQ = 32
PAST_LEN = 2048
CHUNK = 64
N_MEM = 256
EPS = 1e-6
NEG_INF = -1e30
SSM_WIDTH = 1024
SSM_GROUP = 16
SSM_GROUPS = 64
SSM_STATE = 64
MLA_HEADS = 8
QK_NOPE = 128
QK_ROPE = 64
V_HEAD = 128
Q_LORA = 512
KV_LORA = 512
MLA_WIDTH = 1024
ROPE_THETA = 10000.0
MEM_HEADS = 4
MEM_HEAD_DIM = 128
MEM_WIDTH = 512

N_PROMPT = BATCH * SEQ
N_SAMPLE = DEC_BATCH * DEC_SEQ
N_TOK = N_PROMPT + N_SAMPLE

P_GATE = 0
P_U = 6144
P_ZS = 7168
P_CQ = 8192
P_CKV = 8704
P_ZA = 9216
P_QM = 10240
P_ZM = 10752
P_COLS = 11264

MLA_SCALE = (QK_NOPE + QK_ROPE) ** -0.5
MEM_SCALE = MEM_HEAD_DIM ** -0.5

V7X_VMEM_LIMIT = 52 * 1024 * 1024

BF16 = jnp.bfloat16
F32 = jnp.float32


def _params(sem, vmem=V7X_VMEM_LIMIT):
    return pltpu.CompilerParams(dimension_semantics=sem, vmem_limit_bytes=vmem)


def _dot(a, b):
    return jnp.dot(a, b, preferred_element_type=F32)


def _dot_nt(a, b):
    return lax.dot_general(a, b, (((1,), (1,)), ((), ())), preferred_element_type=F32)


def _rms(x, g):
    return x * lax.rsqrt(jnp.mean(x * x, axis=-1, keepdims=True) + EPS) * g


def _silu(x):
    return x * jax.nn.sigmoid(x)


IN_TM = 768
IN_TN = 1024


def _in_proj_kernel(x_ref, g_ref, w_ref, wkr_ref, p_ref, kr_ref, h_ref):
    @pl.when(pl.program_id(1) == 0)
    def _():
        h = _rms(x_ref[...], g_ref[...]).astype(BF16)
        h_ref[...] = h
        kr_ref[...] = _dot(h, wkr_ref[...])

    p_ref[...] = _dot(h_ref[...], w_ref[...])


def _in_proj(x, g_pre, w, wkr):
    return pl.pallas_call(
        _in_proj_kernel,
        grid=(N_TOK // IN_TM, P_COLS // IN_TN),
        in_specs=[
            pl.BlockSpec((IN_TM, D_MODEL), lambda i, j: (i, 0)),
            pl.BlockSpec((1, D_MODEL), lambda i, j: (0, 0)),
            pl.BlockSpec((D_MODEL, IN_TN), lambda i, j: (0, j)),
            pl.BlockSpec((D_MODEL, 128), lambda i, j: (0, 0)),
        ],
        out_specs=[
            pl.BlockSpec((IN_TM, IN_TN), lambda i, j: (i, j)),
            pl.BlockSpec((IN_TM, 128), lambda i, j: (i, 0)),
        ],
        out_shape=[
            jax.ShapeDtypeStruct((N_TOK, P_COLS), F32),
            jax.ShapeDtypeStruct((N_TOK, 128), F32),
        ],
        scratch_shapes=[pltpu.VMEM((IN_TM, D_MODEL), BF16)],
        compiler_params=_params(("parallel", "arbitrary")),
        name="in_proj",
    )(x, g_pre, w, wkr)


PREP_TM = 768


def _mla_prep_kernel(cq_ref, ckv_ref, kr_ref, cs_ref, qg_ref, kvg_ref, wuq_ref, wukv_ref,
                     qn_ref, qr_ref, ckvo_ref, kn_ref, v_ref, kro_ref, krp_ref):
    cs = cs_ref[...]
    hq = _rms(cq_ref[...], qg_ref[...]).astype(BF16)
    q = _dot(hq, wuq_ref[...])
    qn_ref[...] = (q[:, :MLA_WIDTH] * MLA_SCALE).astype(BF16)
    parts = []
    for h in range(MLA_HEADS):
        t = q[:, MLA_WIDTH + 128 * h:MLA_WIDTH + 128 * (h + 1)] * cs
        parts.append(((t + pltpu.roll(t, 64, axis=1)) * MLA_SCALE).astype(BF16))
    qr_ref[...] = jnp.concatenate(parts, axis=1)

    ckv = _rms(ckv_ref[...], kvg_ref[...])
    ckvo_ref[...] = ckv
    kv = _dot(ckv.astype(BF16), wukv_ref[...])
    kn_ref[...] = kv[:, :MLA_WIDTH].astype(BF16)
    v_ref[...] = kv[:, MLA_WIDTH:].astype(BF16)

    tk = kr_ref[...] * cs
    rk = tk + pltpu.roll(tk, 64, axis=1)
    kro_ref[...] = rk[:, :QK_ROPE]
    lane = lax.broadcasted_iota(jnp.int32, rk.shape, 1)
    krp_ref[...] = jnp.where(lane < QK_ROPE, rk, 0.0).astype(BF16)


def _mla_prep(p, kr, cs, qg, kvg, wuq, wukv):
    tm = PREP_TM
    row = lambda c: (lambda i: (i, c))
    const = lambda i: (0, 0)
    return pl.pallas_call(
        _mla_prep_kernel,
        grid=(N_TOK // tm,),
        in_specs=[
            pl.BlockSpec((tm, Q_LORA), row(P_CQ // Q_LORA)),
            pl.BlockSpec((tm, KV_LORA), row(P_CKV // KV_LORA)),
            pl.BlockSpec((tm, 128), row(0)),
            pl.BlockSpec((tm, 128), row(0)),
            pl.BlockSpec((1, Q_LORA), const),
            pl.BlockSpec((1, KV_LORA), const),
            pl.BlockSpec((Q_LORA, 2048), const),
            pl.BlockSpec((KV_LORA, 2048), const),
        ],
        out_specs=[
            pl.BlockSpec((tm, MLA_WIDTH), row(0)),
            pl.BlockSpec((tm, MLA_WIDTH), row(0)),
            pl.BlockSpec((tm, KV_LORA), row(0)),
            pl.BlockSpec((tm, MLA_WIDTH), row(0)),
            pl.BlockSpec((tm, MLA_WIDTH), row(0)),
            pl.BlockSpec((tm, QK_ROPE), row(0)),
            pl.BlockSpec((tm, 128), row(0)),
        ],
        out_shape=[
            jax.ShapeDtypeStruct((N_TOK, MLA_WIDTH), BF16),
            jax.ShapeDtypeStruct((N_TOK, MLA_WIDTH), BF16),
            jax.ShapeDtypeStruct((N_TOK, KV_LORA), F32),
            jax.ShapeDtypeStruct((N_TOK, MLA_WIDTH), BF16),
            jax.ShapeDtypeStruct((N_TOK, MLA_WIDTH), BF16),
            jax.ShapeDtypeStruct((N_TOK, QK_ROPE), F32),
            jax.ShapeDtypeStruct((N_TOK, 128), BF16),
        ],
        compiler_params=_params(("parallel",)),
        name="mla_prep",
    )(p, p, kr, cs, qg, kvg, wuq, wukv)


def _mm_kernel(x_ref, w_ref, *o_refs):
    y = _dot(x_ref[...].astype(BF16), w_ref[...])
    off = 0
    for o_ref in o_refs:
        n = o_ref.shape[1]
        o_ref[...] = y[:, off:off + n].astype(o_ref.dtype)
        off += n


def _matmul(x, w, out_widths, out_dtype, tm, name):
    m, k = x.shape
    n = w.shape[1]
    assert sum(out_widths) == n and m % tm == 0
    return pl.pallas_call(
        _mm_kernel,
        grid=(m // tm,),
        in_specs=[pl.BlockSpec((tm, k), lambda i: (i, 0)),
                  pl.BlockSpec((k, n), lambda i: (0, 0))],
        out_specs=[pl.BlockSpec((tm, wd), lambda i: (i, 0)) for wd in out_widths],
        out_shape=[jax.ShapeDtypeStruct((m, wd), out_dtype) for wd in out_widths],
        compiler_params=_params(("parallel",)),
        name=name,
    )(x, w)


ATT_TQ = 512


def _softmax_pv(parts, z):
    m = functools.reduce(jnp.maximum, [jnp.max(s, axis=1, keepdims=True) for s, _ in parts])
    l = 0.0
    o = 0.0
    for s, v in parts:
        e = jnp.exp(s - m)
        l = l + jnp.sum(e, axis=1, keepdims=True)
        o = o + _dot(e.astype(BF16), v)
    return o * (1.0 / l) * _silu(z)


def _chunk_mask(q_pos0, k_pos0, nq, nk):
    qp = q_pos0 + lax.broadcasted_iota(jnp.int32, (nq, nk), 0)
    kp = k_pos0 + lax.broadcasted_iota(jnp.int32, (nq, nk), 1)
    sh = CHUNK.bit_length() - 1
    return lax.shift_right_logical(kp, sh) <= lax.shift_right_logical(qp, sh)


def _attn_prompt_kernel(qn_ref, qr_ref, kn_ref, kr_ref, v_ref, z_ref, o_ref):
    tq = ATT_TQ
    for qi in range(SEQ // tq):
        lo, hi = qi * tq, (qi + 1) * tq
        qn = qn_ref[lo:hi, :]
        qr = qr_ref[lo:hi, :]
        sd = _dot_nt(qn, kn_ref[lo:hi, :]) + _dot_nt(qr, kr_ref[lo:hi, :])
        sd = jnp.where(_chunk_mask(lo, lo, tq, tq), sd, NEG_INF)
        parts = [(sd, v_ref[lo:hi, :])]
        if qi > 0:
            so = _dot_nt(qn, kn_ref[0:lo, :]) + _dot_nt(qr, kr_ref[0:lo, :])
            parts.append((so, v_ref[0:lo, :]))
        o_ref[lo:hi, :] = _softmax_pv(parts, z_ref[lo:hi, :]).astype(BF16)


def _attn_prompt(qn, qr, kn, krp, v, p):
    blk = lambda c0: pl.BlockSpec((SEQ, 128), lambda b, h: (b, c0 + h))
    return pl.pallas_call(
        _attn_prompt_kernel,
        grid=(BATCH, MLA_HEADS),
        in_specs=[blk(0), blk(0), blk(0),
                  pl.BlockSpec((SEQ, 128), lambda b, h: (b, 0)),
                  blk(0), blk(P_ZA // 128)],
        out_specs=blk(0),
        out_shape=jax.ShapeDtypeStruct((N_TOK, MLA_WIDTH), BF16),
        compiler_params=_params(("parallel", "parallel")),
        name="attn_prompt",
    )(qn, qr, kn, krp, v, p)


def _attn_sample_kernel(y_hbm, qn_ref, qr_ref, kc_ref, krc_ref, vc_ref, kn_ref, krn_ref, vn_ref,
                        z_ref, o_ref):
    del y_hbm
    qn = qn_ref[...]
    qr = qr_ref[...]
    sp = _dot_nt(qn, kc_ref[...]) + _dot_nt(qr, krc_ref[...])
    sp = jnp.where(_chunk_mask(PAST_LEN, 0, DEC_SEQ, PAST_LEN), sp, NEG_INF)
    sn = _dot_nt(qn, kn_ref[...]) + _dot_nt(qr, krn_ref[...])
    sn = jnp.where(_chunk_mask(PAST_LEN, PAST_LEN, DEC_SEQ, DEC_SEQ), sn, NEG_INF)
    o_ref[...] = _softmax_pv([(sp, vc_ref[...]), (sn, vn_ref[...])], z_ref[...]).astype(BF16)


def _attn_sample(y_mla, qn, qr, kc, krc, vc, kn, krp, v, p):
    r0 = N_PROMPT // DEC_SEQ
    new = lambda c0: pl.BlockSpec((DEC_SEQ, 128), lambda b, h: (r0 + b, c0 + h))
    past = pl.BlockSpec((PAST_LEN, 128), lambda b, h: (b, h))
    return pl.pallas_call(
        _attn_sample_kernel,
        grid=(DEC_BATCH, MLA_HEADS),
        in_specs=[pl.BlockSpec(memory_space=pl.ANY),
                  new(0), new(0),
                  past, pl.BlockSpec((PAST_LEN, 128), lambda b, h: (b, 0)), past,
                  new(0), pl.BlockSpec((DEC_SEQ, 128), lambda b, h: (r0 + b, 0)), new(0),
                  new(P_ZA // 128)],
        out_specs=new(0),
        out_shape=jax.ShapeDtypeStruct((N_TOK, MLA_WIDTH), BF16),
        input_output_aliases={0: 0},
        compiler_params=_params(("parallel", "parallel")),
        name="attn_sample",
    )(y_mla, qn, qr, kc, krc, vc, kn, krp, v, p)


def _mem_attn_kernel(*refs, aliased):
    if aliased:
        refs = refs[1:]
    q_ref, z_ref, k_ref, v_ref, o_ref = refs
    q = q_ref[...] * MEM_SCALE
    outs = []
    for h in range(MEM_HEADS):
        sl = slice(MEM_HEAD_DIM * h, MEM_HEAD_DIM * (h + 1))
        s = _dot_nt(q[:, sl].astype(BF16), k_ref[:, sl].astype(BF16))
        m = jnp.max(s, axis=1, keepdims=True)
        e = jnp.exp(s - m)
        l = jnp.sum(e, axis=1, keepdims=True)
        outs.append(_dot(e.astype(BF16), v_ref[:, sl].astype(BF16)) * (1.0 / l))
    o_ref[...] = (jnp.concatenate(outs, axis=1) * _silu(z_ref[...])).astype(BF16)


def _mem_attn(p, mem_k, mem_v, *, nb, tq, row0, y_prev=None):
    seq_blocks = (SEQ if y_prev is None else DEC_SEQ) // tq
    r0 = row0 // tq
    rowmap = lambda c: (lambda b, i: (r0 + b * seq_blocks + i, c))
    in_specs = [pl.BlockSpec((tq, MEM_WIDTH), rowmap(P_QM // MEM_WIDTH)),
                pl.BlockSpec((tq, MEM_WIDTH), rowmap(P_ZM // MEM_WIDTH)),
                pl.BlockSpec((N_MEM, MEM_WIDTH), lambda b, i: (b, 0)),
                pl.BlockSpec((N_MEM, MEM_WIDTH), lambda b, i: (b, 0))]
    args = [p, p, mem_k, mem_v]
    aliases = {}
    if y_prev is not None:
        in_specs = [pl.BlockSpec(memory_space=pl.ANY)] + in_specs
        args = [y_prev] + args
        aliases = {0: 0}
    return pl.pallas_call(
        functools.partial(_mem_attn_kernel, aliased=y_prev is not None),
        grid=(nb, seq_blocks),
        in_specs=in_specs,
        out_specs=pl.BlockSpec((tq, MEM_WIDTH), rowmap(0)),
        out_shape=jax.ShapeDtypeStruct((N_TOK, MEM_WIDTH), BF16),
        input_output_aliases=aliases,
        compiler_params=_params(("parallel", "parallel")),
        name="mem_attn_sample" if y_prev is not None else "mem_attn_prompt",
    )(*args)


def _s5_disc_kernel(are_ref, aim_ref, ldt_ref, bre_ref, bim_ref, abr_ref, abi_ref, bbr_ref, bbi_ref):
    dt = jnp.exp(ldt_ref[...])
    lr, li = are_ref[...], aim_ref[...]
    mag = jnp.exp(lr * dt)
    ab_re, ab_im = mag * jnp.cos(li * dt), mag * jnp.sin(li * dt)
    den = lr * lr + li * li
    nr, ni = ab_re - 1.0, ab_im
    f_re = (nr * lr + ni * li) / den
    f_im = (ni * lr - nr * li) / den
    abr_ref[...] = ab_re
    abi_ref[...] = ab_im
    br, bi = bre_ref[...], bim_ref[...]
    bbr_ref[...] = f_re * br - f_im * bi
    bbi_ref[...] = f_re * bi + f_im * br


def _s5_discretize(a_re, a_im, log_dt, b_re_t, b_im_t):
    G, P, C = SSM_GROUPS, SSM_STATE, SSM_GROUP
    ab_re, ab_im, bb_re_t, bb_im_t = pl.pallas_call(
        _s5_disc_kernel,
        out_shape=[jax.ShapeDtypeStruct((G, 1, P), F32), jax.ShapeDtypeStruct((G, 1, P), F32),
                   jax.ShapeDtypeStruct((G, C, P), F32), jax.ShapeDtypeStruct((G, C, P), F32)],
        name="s5_discretize",
    )(a_re.reshape(G, 1, P), a_im.reshape(G, 1, P), log_dt.reshape(G, 1, 1), b_re_t, b_im_t)
    return ab_re.reshape(G, P), ab_im.reshape(G, P), bb_re_t, bb_im_t


SSM_PAIRS = 4
SSM_HALF_STATE = 512


def _ssm_scan_kernel(lhs_ref, bmat_ref, cmat_ref, are_ref, aim_ref, h0_ref, y_ref, ht_ref,
                     st_ref, bu_ref, *, q, tt):
    c = pl.program_id(1)
    hs = SSM_HALF_STATE

    @pl.when(c == 0)
    def _():
        st_ref[...] = h0_ref[0]

    bu_ref[...] = _dot(lhs_ref[0], bmat_ref[0])
    ar = are_ref[0]
    ai = aim_ref[0]

    def step(t, carry):
        hr, hi = carry
        r0 = pl.multiple_of(t * q, q)
        bur = bu_ref[pl.ds(r0, q), 0:hs]
        bui = bu_ref[pl.ds(r0, q), hs:2 * hs]
        nr = ar * hr - ai * hi + bur
        ni = ar * hi + ai * hr + bui
        bu_ref[pl.ds(r0, q), 0:hs] = nr
        bu_ref[pl.ds(r0, q), hs:2 * hs] = ni
        return nr, ni

    hr, hi = lax.fori_loop(0, tt, step, (st_ref[:, 0:hs], st_ref[:, hs:2 * hs]), unroll=4)
    st_ref[:, 0:hs] = hr
    st_ref[:, hs:2 * hs] = hi

    y2 = _dot(bu_ref[...].astype(BF16), cmat_ref[0])
    half = lax.broadcasted_iota(jnp.int32, (tt * q, 128), 0) % 2
    y_ref[0] = jnp.where(half == 0, y2[:, 0:128], y2[:, 128:256])

    @pl.when(c == pl.num_programs(1) - 1)
    def _():
        ht_ref[0] = st_ref[...]


def _ssm_scan(lhs, bmat, cmat, a_re, a_im, h0, *, q, tt, name):
    rows = lhs.shape[1]
    nt = rows // (tt * q)
    pair = lambda p, c: (p, 0, 0)
    return pl.pallas_call(
        functools.partial(_ssm_scan_kernel, q=q, tt=tt),
        grid=(SSM_PAIRS, nt),
        in_specs=[pl.BlockSpec((1, tt * q, 256), lambda p, c: (p, c, 0)),
                  pl.BlockSpec((1, 256, 1024), pair),
                  pl.BlockSpec((1, 1024, 256), pair),
                  pl.BlockSpec((1, q, SSM_HALF_STATE), pair),
                  pl.BlockSpec((1, q, SSM_HALF_STATE), pair),
                  pl.BlockSpec((1, q, 2 * SSM_HALF_STATE), pair)],
        out_specs=[pl.BlockSpec((1, tt * q, 128), lambda p, c: (p, c, 0)),
                   pl.BlockSpec((1, q, 2 * SSM_HALF_STATE), pair)],
        out_shape=[jax.ShapeDtypeStruct((SSM_PAIRS, rows, 128), F32),
                   jax.ShapeDtypeStruct((SSM_PAIRS, q, 2 * SSM_HALF_STATE), F32)],
        scratch_shapes=[pltpu.VMEM((q, 2 * SSM_HALF_STATE), F32),
                        pltpu.VMEM((tt * q, 2 * SSM_HALF_STATE), F32)],
        compiler_params=_params(("parallel", "arbitrary")),
        name=name,
    )(lhs, bmat, cmat, a_re, a_im, h0)


def _ssm_lhs(u, nseq, s):
    u6 = u.reshape(nseq, s, SSM_PAIRS, 2, 1, 128).transpose(2, 1, 0, 3, 4, 5)
    eye = jnp.eye(2, dtype=u.dtype).reshape(1, 1, 1, 2, 2, 1)
    return (u6 * eye).astype(BF16).reshape(SSM_PAIRS, s * nseq * 2, 256)


def _ssm_unperm(y, nseq, s):
    return y.reshape(SSM_PAIRS, s, nseq, 2, 128).transpose(2, 1, 0, 3, 4).reshape(nseq * s, SSM_WIDTH)


def _ssm_state_in(h_re, h_im, nseq):
    def arr(h):
        return h.reshape(nseq, SSM_PAIRS, 2, SSM_HALF_STATE).transpose(1, 0, 2, 3).reshape(
            SSM_PAIRS, nseq * 2, SSM_HALF_STATE)
    return jnp.concatenate([arr(h_re), arr(h_im)], axis=-1)


def _ssm_state_out(ht, nseq):
    def arr(h):
        return h.reshape(SSM_PAIRS, nseq, 2, 8, SSM_STATE).transpose(1, 0, 2, 3, 4).reshape(
            nseq, SSM_GROUPS, SSM_STATE)
    return arr(ht[..., :SSM_HALF_STATE]), arr(ht[..., SSM_HALF_STATE:])


def _ssm_mats(ab_re, ab_im, bb_re_t, bb_im_t, c_re, c_im):
    eye8 = jnp.eye(8, dtype=F32)

    def bmat(bb):
        b6 = bb.reshape(SSM_PAIRS, 2, 8, SSM_GROUP, SSM_STATE)
        return jnp.einsum('ahgcp,gk->ahgckp', b6, eye8).reshape(SSM_PAIRS, 256, SSM_HALF_STATE)

    def cmat(cc):
        c6 = cc.reshape(SSM_PAIRS, 2, 8, SSM_GROUP, SSM_STATE)
        return jnp.einsum('ahgcp,gk->akphgc', c6, eye8).reshape(SSM_PAIRS, SSM_HALF_STATE, 256)

    b_all = jnp.concatenate([bmat(bb_re_t), bmat(bb_im_t)], axis=2).astype(BF16)
    c_all = jnp.concatenate([cmat(c_re), cmat(-c_im)], axis=1).astype(BF16)

    def a_rows(a, nseq):
        a3 = a.reshape(SSM_PAIRS, 1, 2, SSM_HALF_STATE)
        return jnp.broadcast_to(a3, (SSM_PAIRS, nseq, 2, SSM_HALF_STATE)).reshape(
            SSM_PAIRS, nseq * 2, SSM_HALF_STATE)

    return b_all, c_all, a_rows


POST_TM = 768
POST_TN = 512


def _gelu_tanh(x):
    return 0.5 * x * (1.0 + jnp.tanh(math.sqrt(2.0 / math.pi) * (x + 0.044715 * (x * x * x))))


def _ssm_post_kernel(y_ref, u_ref, d_ref, wa_ref, wb_ref, z_ref, o_ref, g_ref):
    @pl.when(pl.program_id(1) == 0)
    def _():
        g_ref[...] = _gelu_tanh(y_ref[...] + d_ref[...] * u_ref[...]).astype(BF16)

    g = g_ref[...]
    ga = _dot(g, wa_ref[...])
    gb = _dot(g, wb_ref[...])
    o_ref[...] = (ga * jax.nn.sigmoid(gb) * _silu(z_ref[...])).astype(BF16)


def _ssm_post(y, p, d, w_glu):
    tm, tn = POST_TM, POST_TN
    nj = SSM_WIDTH // tn
    return pl.pallas_call(
        _ssm_post_kernel,
        grid=(N_TOK // tm, nj),
        in_specs=[pl.BlockSpec((tm, SSM_WIDTH), lambda i, j: (i, 0)),
                  pl.BlockSpec((tm, SSM_WIDTH), lambda i, j: (i, P_U // SSM_WIDTH)),
                  pl.BlockSpec((1, SSM_WIDTH), lambda i, j: (0, 0)),
                  pl.BlockSpec((SSM_WIDTH, tn), lambda i, j: (0, j)),
                  pl.BlockSpec((SSM_WIDTH, tn), lambda i, j: (0, nj + j)),
                  pl.BlockSpec((tm, tn), lambda i, j: (i, P_ZS // tn + j))],
        out_specs=pl.BlockSpec((tm, tn), lambda i, j: (i, j)),
        out_shape=jax.ShapeDtypeStruct((N_TOK, SSM_WIDTH), BF16),
        scratch_shapes=[pltpu.VMEM((tm, SSM_WIDTH), BF16)],
        compiler_params=_params(("parallel", "arbitrary")),
        name="ssm_post",
    )(y, p, d, w_glu, w_glu, p)


OUT_TM = 256


def _merge_out_kernel(x_ref, ys_ref, ya_ref, ym_ref, g0_ref, g1_ref, g2_ref,
                      ws_ref, wa_ref, wm_ref, wo_ref, gp_ref, o_ref):
    merged = (jax.nn.sigmoid(g0_ref[...]) * _dot(ys_ref[...], ws_ref[...])
              + jax.nn.sigmoid(g1_ref[...]) * _dot(ya_ref[...], wa_ref[...])
              + jax.nn.sigmoid(g2_ref[...]) * _dot(ym_ref[...], wm_ref[...]))
    out = _dot(merged.astype(BF16), wo_ref[...])
    o_ref[...] = x_ref[...] + _rms(out, gp_ref[...])


def _merge_out(x, y_ssm, y_mla, y_mem, p, w_ssm_o, w_mla_o, w_mem_o, w_out, g_post):
    tm = OUT_TM
    row = lambda c: (lambda i: (i, c))
    const = lambda i: (0, 0)
    resident = lambda shape: pl.BlockSpec(shape, const, pipeline_mode=pl.Buffered(1))
    return pl.pallas_call(
        _merge_out_kernel,
        grid=(N_TOK // tm,),
        in_specs=[pl.BlockSpec((tm, D_MODEL), row(0)),
                  pl.BlockSpec((tm, SSM_WIDTH), row(0)),
                  pl.BlockSpec((tm, MLA_WIDTH), row(0)),
                  pl.BlockSpec((tm, MEM_WIDTH), row(0)),
                  pl.BlockSpec((tm, D_MODEL), row(0)),
                  pl.BlockSpec((tm, D_MODEL), row(1)),
                  pl.BlockSpec((tm, D_MODEL), row(2)),
                  resident((SSM_WIDTH, D_MODEL)),
                  resident((MLA_WIDTH, D_MODEL)),
                  resident((MEM_WIDTH, D_MODEL)),
                  resident((D_MODEL, D_MODEL)),
                  pl.BlockSpec((1, D_MODEL), const)],
        out_specs=pl.BlockSpec((tm, D_MODEL), row(0)),
        out_shape=jax.ShapeDtypeStruct((N_TOK, D_MODEL), F32),
        compiler_params=_params(("parallel",)),
        name="merge_out",
    )(x, y_ssm, y_mla, y_mem, p, p, p, w_ssm_o, w_mla_o, w_mem_o, w_out, g_post)


def _rot_half_cols(w):
    half = w.shape[-1] // 2
    return jnp.concatenate([w[..., half:], w[..., :half]], axis=-1)


def _prep_weights(w_in, w_uq, w_uk, w_uv):
    o = np.cumsum((0, SSM_WIDTH, SSM_WIDTH, Q_LORA, KV_LORA, QK_ROPE, MLA_WIDTH, MEM_WIDTH, MEM_WIDTH,
                   3 * D_MODEL))
    seg = lambda k: w_in[:, :, int(o[k]):int(o[k + 1])]
    w_main = jnp.concatenate([seg(8), seg(0), seg(1), seg(2), seg(3), seg(5), seg(6), seg(7)],
                             axis=-1).astype(BF16)
    wk = seg(4)
    w_kr = jnp.concatenate([wk, _rot_half_cols(wk)], axis=-1).astype(BF16)
    uq = w_uq.reshape(DEPTH, Q_LORA, MLA_HEADS, QK_NOPE + QK_ROPE)
    uq_nope = uq[..., :QK_NOPE].reshape(DEPTH, Q_LORA, MLA_WIDTH)
    uq_rope = uq[..., QK_NOPE:]
    uq_rr = jnp.concatenate([uq_rope, _rot_half_cols(uq_rope)], axis=-1).reshape(DEPTH, Q_LORA, MLA_WIDTH)
    w_uq2 = jnp.concatenate([uq_nope, uq_rr], axis=-1).astype(BF16)
    w_ukv = jnp.concatenate([w_uk, w_uv], axis=-1).astype(BF16)
    return w_main, w_kr, w_uq2, w_ukv


def _rope_table():
    half = QK_ROPE // 2
    pos = jnp.concatenate([jnp.tile(jnp.arange(SEQ, dtype=jnp.int32), BATCH),
                           jnp.tile(PAST_LEN + jnp.arange(DEC_SEQ, dtype=jnp.int32), DEC_BATCH)])
    inv = ROPE_THETA ** (-jnp.arange(half, dtype=F32) / half)
    ang = pos.astype(F32)[:, None] * inv[None, :]
    cos, sin = jnp.cos(ang), jnp.sin(ang)
    return jnp.concatenate([cos, cos, -sin, sin], axis=-1)


def kernel(x_prompt, x_sample, cache_mla_ckv, cache_mla_krope, cache_mem_k, cache_mem_v, state_ssm_re, state_ssm_im, mem_prompt, norm_pre, w_in, ssm_a_re, ssm_a_im, ssm_log_dt, ssm_b_re, ssm_b_im, ssm_c_re, ssm_c_im, ssm_d, w_glu, mla_q_norm, w_uq, mla_kv_norm, w_uk, w_uv, w_mem_k, w_mem_v, w_ssm_o, w_mla_o, w_mem_o, w_out, norm_post):
    w_main, w_kr, w_uq2, w_ukv = _prep_weights(w_in, w_uq, w_uk, w_uv)
    w_glu_b = w_glu.astype(BF16)
    w_ssm_o_b, w_mla_o_b, w_mem_o_b, w_out_b = (w.astype(BF16) for w in (w_ssm_o, w_mla_o, w_mem_o, w_out))
    cs = _rope_table()

    w_mem_kv = jnp.concatenate([w_mem_k, w_mem_v], axis=-1).astype(BF16)
    mem_rows = mem_prompt.reshape(BATCH * N_MEM, D_MODEL)

    krc_all = jnp.pad(cache_mla_krope.astype(BF16), ((0, 0), (0, 0), (0, 0), (0, 128 - QK_ROPE))).reshape(
        DEPTH, DEC_BATCH * PAST_LEN, 128)

    zeros_state = jnp.zeros((BATCH, SSM_GROUPS, SSM_STATE), F32)
    x = jnp.concatenate([x_prompt.reshape(N_PROMPT, D_MODEL), x_sample.reshape(N_SAMPLE, D_MODEL)], axis=0)

    p_re, p_im, p_ckv, p_kr, p_mk, p_mv = [], [], [], [], [], []
    s_re, s_im, s_ckv, s_kr = [], [], [], []
    for l in range(DEPTH):
        p, kr = _in_proj(x, norm_pre[l][None], w_main[l], w_kr[l])

        ab_re, ab_im, bb_re_t, bb_im_t = _s5_discretize(
            ssm_a_re[l], ssm_a_im[l], ssm_log_dt[l],
            ssm_b_re[l].transpose(0, 2, 1), ssm_b_im[l].transpose(0, 2, 1))
        bmat, cmat, a_rows = _ssm_mats(ab_re, ab_im, bb_re_t, bb_im_t, ssm_c_re[l], ssm_c_im[l])
        u = p[:, P_U:P_U + SSM_WIDTH]
        y_p, ht_p = _ssm_scan(_ssm_lhs(u[:N_PROMPT], BATCH, SEQ), bmat, cmat,
                              a_rows(ab_re, BATCH), a_rows(ab_im, BATCH),
                              _ssm_state_in(zeros_state, zeros_state, BATCH),
                              q=2 * BATCH, tt=128, name="ssm_scan_prompt")
        y_s, ht_s = _ssm_scan(_ssm_lhs(u[N_PROMPT:], DEC_BATCH, DEC_SEQ), bmat, cmat,
                              a_rows(ab_re, DEC_BATCH), a_rows(ab_im, DEC_BATCH),
                              _ssm_state_in(state_ssm_re[l], state_ssm_im[l], DEC_BATCH),
                              q=2 * DEC_BATCH, tt=DEC_SEQ, name="ssm_scan_sample")
        y_nat = jnp.concatenate([_ssm_unperm(y_p, BATCH, SEQ), _ssm_unperm(y_s, DEC_BATCH, DEC_SEQ)], axis=0)
        y_ssm = _ssm_post(y_nat, p, ssm_d[l][None], w_glu_b[l])
        hr, hi = _ssm_state_out(ht_p, BATCH)
        p_re.append(hr); p_im.append(hi)
        hr, hi = _ssm_state_out(ht_s, DEC_BATCH)
        s_re.append(hr); s_im.append(hi)

        qn, qr, ckv, kn, v, kro, krp = _mla_prep(p, kr, cs, mla_q_norm[l][None], mla_kv_norm[l][None],
                                                 w_uq2[l], w_ukv[l])
        kc, vc = _matmul(cache_mla_ckv[l].reshape(DEC_BATCH * PAST_LEN, KV_LORA), w_ukv[l],
                         [MLA_WIDTH, MLA_WIDTH], BF16, 1024, "kv_up_cache")
        y_mla = _attn_prompt(qn, qr, kn, krp, v, p)
        y_mla = _attn_sample(y_mla, qn, qr, kc, krc_all[l], vc, kn, krp, v, p)
        p_ckv.append(ckv[:N_PROMPT].reshape(BATCH, SEQ, KV_LORA))
        s_ckv.append(ckv[N_PROMPT:].reshape(DEC_BATCH, DEC_SEQ, KV_LORA))
        p_kr.append(kro[:N_PROMPT].reshape(BATCH, SEQ, QK_ROPE))
        s_kr.append(kro[N_PROMPT:].reshape(DEC_BATCH, DEC_SEQ, QK_ROPE))

        mk, mv = _matmul(mem_rows, w_mem_kv[l], [MEM_WIDTH, MEM_WIDTH], F32, 512, "mem_kv")
        y_mem = _mem_attn(p, mk, mv, nb=BATCH, tq=512, row0=0)
        y_mem = _mem_attn(p, cache_mem_k[l].reshape(DEC_BATCH * N_MEM, MEM_WIDTH),
                          cache_mem_v[l].reshape(DEC_BATCH * N_MEM, MEM_WIDTH),
                          nb=DEC_BATCH, tq=DEC_SEQ, row0=N_PROMPT, y_prev=y_mem)
        p_mk.append(mk.reshape(BATCH, N_MEM, MEM_HEADS, MEM_HEAD_DIM))
        p_mv.append(mv.reshape(BATCH, N_MEM, MEM_HEADS, MEM_HEAD_DIM))

        x = _merge_out(x, y_ssm, y_mla, y_mem, p, w_ssm_o_b[l], w_mla_o_b[l], w_mem_o_b[l], w_out_b[l],
                       norm_post[l][None])

    return (x[:N_PROMPT].reshape(BATCH, SEQ, D_MODEL), x[N_PROMPT:].reshape(DEC_BATCH, DEC_SEQ, D_MODEL),
            jnp.stack(p_re), jnp.stack(p_im), jnp.stack(p_ckv), jnp.stack(p_kr), jnp.stack(p_mk), jnp.stack(p_mv),
            jnp.stack(s_re), jnp.stack(s_im), jnp.stack(s_ckv), jnp.stack(s_kr))
```

```python
import functools
import math

import jax
import jax.numpy as jnp
import numpy as np
from jax import lax
from jax.experimental import pallas as pl
from jax.experimental.pallas import tpu as pltpu

D_MODEL = 2048
BATCH = 4
SEQ = 2048
DEPTH = 4
DEC_BATCH = 8
DEC_SEQ = 32
PAST_LEN = 2048
CHUNK = 64
N_MEM = 256
EPS = 1e-6
NEG_INF = -1e30
SSM_WIDTH = 1024
SSM_GROUP = 16
SSM_GROUPS = 64
SSM_STATE = 64
MLA_HEADS = 8
QK_NOPE = 128
QK_ROPE = 64
V_HEAD = 128
Q_LORA = 512
KV_LORA = 512
MLA_WIDTH = 1024
ROPE_THETA = 10000.0
MEM_HEADS = 4
MEM_HEAD_DIM = 128
MEM_WIDTH = 512

N_PROMPT = BATCH * SEQ
N_SAMPLE = DEC_BATCH * DEC_SEQ
N_TOK = N_PROMPT + N_SAMPLE

P_GATE = 0
P_U = 6144
P_ZS = 7168
P_CQ = 8192
P_CKV = 8704
P_ZA = 9216
P_QM = 10240
P_ZM = 10752
P_COLS = 11264
W_IN_COLS = 11328

MLA_SCALE = (QK_NOPE + QK_ROPE) ** -0.5
MEM_SCALE = MEM_HEAD_DIM ** -0.5

V7X_VMEM_LIMIT = 52 * 1024 * 1024

BF16 = jnp.bfloat16
F32 = jnp.float32


def _params(sem, vmem=V7X_VMEM_LIMIT):
    return pltpu.CompilerParams(dimension_semantics=sem, vmem_limit_bytes=vmem)


def _dot(a, b):
    return jnp.dot(a, b, preferred_element_type=F32)


def _dot_nt(a, b):
    return lax.dot_general(a, b, (((1,), (1,)), ((), ())), preferred_element_type=F32)


def _rms(x, g):
    return x * lax.rsqrt(jnp.mean(x * x, axis=-1, keepdims=True) + EPS) * g


def _silu(x):
    return x * jax.nn.sigmoid(x)


def _gelu_tanh(x):
    return 0.5 * x * (1.0 + jnp.tanh(math.sqrt(2.0 / math.pi) * (x + 0.044715 * (x * x * x))))


RL_TM = 512
IN_TM = 768
IN_TN = 1024


def _w_relayout_kernel(a_ref, b_ref, o_ref):
    j = pl.program_id(1)
    shifted = jnp.logical_or(j < 6, j >= 9)

    @pl.when(shifted)
    def _():
        full = jnp.concatenate([a_ref[...], b_ref[...]], axis=1)
        o_ref[...] = full[:, QK_ROPE:QK_ROPE + IN_TN].astype(BF16)

    @pl.when(jnp.logical_not(shifted))
    def _():
        o_ref[...] = a_ref[...].astype(BF16)


def _w_relayout(w_in2d):
    a_blk = lambda j: jnp.where(j < 6, 5 + j, j - 6)
    b_blk = lambda j: jnp.where(j < 6, 48 + 8 * j, jnp.where(j >= 9, 8 * j - 40, 0))
    rows = w_in2d.shape[0]
    return pl.pallas_call(
        _w_relayout_kernel,
        grid=(rows // RL_TM, P_COLS // IN_TN),
        in_specs=[pl.BlockSpec((RL_TM, IN_TN), lambda i, j: (i, a_blk(j))),
                  pl.BlockSpec((RL_TM, 128), lambda i, j: (i, b_blk(j)))],
        out_specs=pl.BlockSpec((RL_TM, IN_TN), lambda i, j: (i, j)),
        out_shape=jax.ShapeDtypeStruct((rows, P_COLS), BF16),
        compiler_params=_params(("parallel", "parallel")),
        name="w_in_relayout",
    )(w_in2d, w_in2d)


def _in_proj_kernel(x_ref, g_ref, w_ref, wkr_ref, p_ref, kr_ref, h_ref):
    @pl.when(pl.program_id(1) == 0)
    def _():
        h = _rms(x_ref[...], g_ref[...]).astype(BF16)
        h_ref[...] = h
        kr_ref[...] = _dot(h, wkr_ref[...])

    p_ref[...] = _dot(h_ref[...], w_ref[...])


def _in_proj(x, g_pre, w, wkr, l):
    return pl.pallas_call(
        _in_proj_kernel,
        grid=(N_TOK // IN_TM, P_COLS // IN_TN),
        in_specs=[
            pl.BlockSpec((IN_TM, D_MODEL), lambda i, j: (i, 0)),
            pl.BlockSpec((1, D_MODEL), lambda i, j: (0, 0)),
            pl.BlockSpec((D_MODEL, IN_TN), lambda i, j: (l, j)),
            pl.BlockSpec((D_MODEL, 128), lambda i, j: (l, 0)),
        ],
        out_specs=[
            pl.BlockSpec((IN_TM, IN_TN), lambda i, j: (i, j)),
            pl.BlockSpec((IN_TM, 128), lambda i, j: (i, 0)),
        ],
        out_shape=[
            jax.ShapeDtypeStruct((N_TOK, P_COLS), F32),
            jax.ShapeDtypeStruct((N_TOK, 128), F32),
        ],
        scratch_shapes=[pltpu.VMEM((IN_TM, D_MODEL), BF16)],
        compiler_params=_params(("parallel", "arbitrary")),
        name="in_proj",
    )(x, g_pre, w, wkr)


PREP_TM = 768


def _mla_prep_kernel(cq_ref, ckv_ref, kr_ref, cs_ref, qg_ref, kvg_ref, wuq_ref, wukv_ref,
                     qn_ref, qr_ref, ckvo_ref, kn_ref, v_ref, kro_ref, krp_ref):
    cs = cs_ref[...]
    hq = _rms(cq_ref[...], qg_ref[...]).astype(BF16)
    q = _dot(hq, wuq_ref[...])
    qn_ref[...] = (q[:, :MLA_WIDTH] * MLA_SCALE).astype(BF16)
    parts = []
    for h in range(MLA_HEADS):
        t = q[:, MLA_WIDTH + 128 * h:MLA_WIDTH + 128 * (h + 1)] * cs
        parts.append(((t + pltpu.roll(t, 64, axis=1)) * MLA_SCALE).astype(BF16))
    qr_ref[...] = jnp.concatenate(parts, axis=1)

    ckv = _rms(ckv_ref[...], kvg_ref[...])
    ckvo_ref[...] = ckv
    kv = _dot(ckv.astype(BF16), wukv_ref[...])
    kn_ref[...] = kv[:, :MLA_WIDTH].astype(BF16)
    v_ref[...] = kv[:, MLA_WIDTH:].astype(BF16)

    tk = kr_ref[...] * cs
    rk = tk + pltpu.roll(tk, 64, axis=1)
    kro_ref[...] = rk[:, :QK_ROPE]
    lane = lax.broadcasted_iota(jnp.int32, rk.shape, 1)
    krp_ref[...] = jnp.where(lane < QK_ROPE, rk, 0.0).astype(BF16)


def _mla_prep(p, kr, cs, qg, kvg, wuq, wukv, l):
    tm = PREP_TM
    row = lambda c: (lambda i: (i, c))
    const = lambda i: (0, 0)
    layer = lambda i: (l, 0)
    return pl.pallas_call(
        _mla_prep_kernel,
        grid=(N_TOK // tm,),
        in_specs=[
            pl.BlockSpec((tm, Q_LORA), row(P_CQ // Q_LORA)),
            pl.BlockSpec((tm, KV_LORA), row(P_CKV // KV_LORA)),
            pl.BlockSpec((tm, 128), row(0)),
            pl.BlockSpec((tm, 128), row(0)),
            pl.BlockSpec((1, Q_LORA), const),
            pl.BlockSpec((1, KV_LORA), const),
            pl.BlockSpec((Q_LORA, 2048), layer),
            pl.BlockSpec((KV_LORA, 2048), layer),
        ],
        out_specs=[
            pl.BlockSpec((tm, MLA_WIDTH), row(0)),
            pl.BlockSpec((tm, MLA_WIDTH), row(0)),
            pl.BlockSpec((tm, KV_LORA), row(0)),
            pl.BlockSpec((tm, MLA_WIDTH), row(0)),
            pl.BlockSpec((tm, MLA_WIDTH), row(0)),
            pl.BlockSpec((tm, QK_ROPE), row(0)),
            pl.BlockSpec((tm, 128), row(0)),
        ],
        out_shape=[
            jax.ShapeDtypeStruct((N_TOK, MLA_WIDTH), BF16),
            jax.ShapeDtypeStruct((N_TOK, MLA_WIDTH), BF16),
            jax.ShapeDtypeStruct((N_TOK, KV_LORA), F32),
            jax.ShapeDtypeStruct((N_TOK, MLA_WIDTH), BF16),
            jax.ShapeDtypeStruct((N_TOK, MLA_WIDTH), BF16),
            jax.ShapeDtypeStruct((N_TOK, QK_ROPE), F32),
            jax.ShapeDtypeStruct((N_TOK, 128), BF16),
        ],
        compiler_params=_params(("parallel",)),
        name="mla_prep",
    )(p, p, kr, cs, qg, kvg, wuq, wukv)


def _mm_kernel(x_ref, w_ref, *o_refs):
    y = _dot(x_ref[...].astype(BF16), w_ref[...])
    off = 0
    for o_ref in o_refs:
        n = o_ref.shape[1]
        o_ref[...] = y[:, off:off + n].astype(o_ref.dtype)
        off += n


def _matmul(x, w, out_widths, out_dtype, tm, name, *, m, x_row0=0, l=0):
    k = x.shape[1]
    n = w.shape[1]
    assert sum(out_widths) == n and m % tm == 0 and x_row0 % tm == 0
    r0 = x_row0 // tm
    return pl.pallas_call(
        _mm_kernel,
        grid=(m // tm,),
        in_specs=[pl.BlockSpec((tm, k), lambda i: (r0 + i, 0)),
                  pl.BlockSpec((k, n), lambda i: (l, 0))],
        out_specs=[pl.BlockSpec((tm, wd), lambda i: (i, 0)) for wd in out_widths],
        out_shape=[jax.ShapeDtypeStruct((m, wd), out_dtype) for wd in out_widths],
        compiler_params=_params(("parallel",)),
        name=name,
    )(x, w)


ATT_TQ = 512


def _softmax_pv(parts, z):
    m = functools.reduce(jnp.maximum, [jnp.max(s, axis=1, keepdims=True) for s, _ in parts])
    l = 0.0
    o = 0.0
    for s, v in parts:
        e = jnp.exp(s - m)
        l = l + jnp.sum(e, axis=1, keepdims=True)
        o = o + _dot(e.astype(BF16), v)
    return o * (1.0 / l) * _silu(z)


def _chunk_mask(q_pos0, k_pos0, nq, nk):
    qp = q_pos0 + lax.broadcasted_iota(jnp.int32, (nq, nk), 0)
    kp = k_pos0 + lax.broadcasted_iota(jnp.int32, (nq, nk), 1)
    sh = CHUNK.bit_length() - 1
    return lax.shift_right_logical(kp, sh) <= lax.shift_right_logical(qp, sh)


def _attn_prompt_kernel(qn_ref, qr_ref, kn_ref, kr_ref, v_ref, z_ref, o_ref):
    tq = ATT_TQ
    for qi in range(SEQ // tq):
        lo, hi = qi * tq, (qi + 1) * tq
        qn = qn_ref[lo:hi, :]
        qr = qr_ref[lo:hi, :]
        sd = _dot_nt(qn, kn_ref[lo:hi, :]) + _dot_nt(qr, kr_ref[lo:hi, :])
        sd = jnp.where(_chunk_mask(lo, lo, tq, tq), sd, NEG_INF)
        parts = [(sd, v_ref[lo:hi, :])]
        if qi > 0:
            so = _dot_nt(qn, kn_ref[0:lo, :]) + _dot_nt(qr, kr_ref[0:lo, :])
            parts.append((so, v_ref[0:lo, :]))
        o_ref[lo:hi, :] = _softmax_pv(parts, z_ref[lo:hi, :]).astype(BF16)


def _attn_prompt(qn, qr, kn, krp, v, p):
    blk = lambda c0: pl.BlockSpec((SEQ, 128), lambda b, h: (b, c0 + h))
    return pl.pallas_call(
        _attn_prompt_kernel,
        grid=(BATCH, MLA_HEADS),
        in_specs=[blk(0), blk(0), blk(0),
                  pl.BlockSpec((SEQ, 128), lambda b, h: (b, 0)),
                  blk(0), blk(P_ZA // 128)],
        out_specs=blk(0),
        out_shape=jax.ShapeDtypeStruct((N_TOK, MLA_WIDTH), BF16),
        compiler_params=_params(("parallel", "parallel")),
        name="attn_prompt",
    )(qn, qr, kn, krp, v, p)


def _attn_sample_kernel(y_hbm, qn_ref, qr_ref, kc_ref, krc_ref, vc_ref, kn_ref, krn_ref, vn_ref,
                        z_ref, o_ref):
    del y_hbm
    qn = qn_ref[...]
    qr = qr_ref[...]
    sp = _dot_nt(qn, kc_ref[...]) + _dot_nt(qr, krc_ref[...])
    sp = jnp.where(_chunk_mask(PAST_LEN, 0, DEC_SEQ, PAST_LEN), sp, NEG_INF)
    sn = _dot_nt(qn, kn_ref[...]) + _dot_nt(qr, krn_ref[...])
    sn = jnp.where(_chunk_mask(PAST_LEN, PAST_LEN, DEC_SEQ, DEC_SEQ), sn, NEG_INF)
    o_ref[...] = _softmax_pv([(sp, vc_ref[...]), (sn, vn_ref[...])], z_ref[...]).astype(BF16)


def _attn_sample(y_mla, qn, qr, kc, krc, vc, kn, krp, v, p, l):
    r0 = N_PROMPT // DEC_SEQ
    new = lambda c0: pl.BlockSpec((DEC_SEQ, 128), lambda b, h: (r0 + b, c0 + h))
    past = pl.BlockSpec((PAST_LEN, 128), lambda b, h: (b, h))
    return pl.pallas_call(
        _attn_sample_kernel,
        grid=(DEC_BATCH, MLA_HEADS),
        in_specs=[pl.BlockSpec(memory_space=pl.ANY),
                  new(0), new(0),
                  past, pl.BlockSpec((PAST_LEN, 128), lambda b, h: (l * DEC_BATCH + b, 0)), past,
                  new(0), pl.BlockSpec((DEC_SEQ, 128), lambda b, h: (r0 + b, 0)), new(0),
                  new(P_ZA // 128)],
        out_specs=new(0),
        out_shape=jax.ShapeDtypeStruct((N_TOK, MLA_WIDTH), BF16),
        input_output_aliases={0: 0},
        compiler_params=_params(("parallel", "parallel")),
        name="attn_sample",
    )(y_mla, qn, qr, kc, krc, vc, kn, krp, v, p)


def _mem_attn_kernel(*refs, aliased):
    if aliased:
        refs = refs[1:]
    q_ref, z_ref, k_ref, v_ref, o_ref = refs
    q = q_ref[...] * MEM_SCALE
    outs = []
    for h in range(MEM_HEADS):
        sl = slice(MEM_HEAD_DIM * h, MEM_HEAD_DIM * (h + 1))
        s = _dot_nt(q[:, sl].astype(BF16), k_ref[:, sl].astype(BF16))
        m = jnp.max(s, axis=1, keepdims=True)
        e = jnp.exp(s - m)
        l = jnp.sum(e, axis=1, keepdims=True)
        outs.append(_dot(e.astype(BF16), v_ref[:, sl].astype(BF16)) * (1.0 / l))
    o_ref[...] = (jnp.concatenate(outs, axis=1) * _silu(z_ref[...])).astype(BF16)


def _mem_attn(p, mem_k, mem_v, *, nb, tq, row0, mem_blk0=0, y_prev=None):
    seq_blocks = (SEQ if y_prev is None else DEC_SEQ) // tq
    r0 = row0 // tq
    rowmap = lambda c: (lambda b, i: (r0 + b * seq_blocks + i, c))
    mem = pl.BlockSpec((N_MEM, MEM_WIDTH), lambda b, i: (mem_blk0 + b, 0))
    in_specs = [pl.BlockSpec((tq, MEM_WIDTH), rowmap(P_QM // MEM_WIDTH)),
                pl.BlockSpec((tq, MEM_WIDTH), rowmap(P_ZM // MEM_WIDTH)),
                mem, mem]
    args = [p, p, mem_k, mem_v]
    aliases = {}
    if y_prev is not None:
        in_specs = [pl.BlockSpec(memory_space=pl.ANY)] + in_specs
        args = [y_prev] + args
        aliases = {0: 0}
    return pl.pallas_call(
        functools.partial(_mem_attn_kernel, aliased=y_prev is not None),
        grid=(nb, seq_blocks),
        in_specs=in_specs,
        out_specs=pl.BlockSpec((tq, MEM_WIDTH), rowmap(0)),
        out_shape=jax.ShapeDtypeStruct((N_TOK, MEM_WIDTH), BF16),
        input_output_aliases=aliases,
        compiler_params=_params(("parallel", "parallel")),
        name="mem_attn_sample" if y_prev is not None else "mem_attn_prompt",
    )(*args)


def _s5_disc_kernel(are_ref, aim_ref, ldt_ref, bre_ref, bim_ref, abr_ref, abi_ref, bbr_ref, bbi_ref):
    dt = jnp.exp(ldt_ref[...])
    lr, li = are_ref[...], aim_ref[...]
    mag = jnp.exp(lr * dt)
    ab_re, ab_im = mag * jnp.cos(li * dt), mag * jnp.sin(li * dt)
    den = lr * lr + li * li
    nr, ni = ab_re - 1.0, ab_im
    f_re = (nr * lr + ni * li) / den
    f_im = (ni * lr - nr * li) / den
    abr_ref[...] = ab_re
    abi_ref[...] = ab_im
    br, bi = bre_ref[...], bim_ref[...]
    bbr_ref[...] = f_re * br - f_im * bi
    bbi_ref[...] = f_re * bi + f_im * br


def _s5_discretize(a_re, a_im, log_dt, b_re_t, b_im_t):
    G, P, C = SSM_GROUPS, SSM_STATE, SSM_GROUP
    ab_re, ab_im, bb_re_t, bb_im_t = pl.pallas_call(
        _s5_disc_kernel,
        out_shape=[jax.ShapeDtypeStruct((G, 1, P), F32), jax.ShapeDtypeStruct((G, 1, P), F32),
                   jax.ShapeDtypeStruct((G, C, P), F32), jax.ShapeDtypeStruct((G, C, P), F32)],
        name="s5_discretize",
    )(a_re.reshape(G, 1, P), a_im.reshape(G, 1, P), log_dt.reshape(G, 1, 1), b_re_t, b_im_t)
    return ab_re.reshape(G, P), ab_im.reshape(G, P), bb_re_t, bb_im_t


SSM_PAIRS = 4
SSM_HALF_STATE = 512
SSM_SUB = 32


def _ssm_kernel(*refs, nseq, tt):
    u_refs, z_refs = refs[:nseq], refs[nseq:2 * nseq]
    (d_ref, wglu_ref, bmat_ref, cmat_ref, are_ref, aim_ref, h0_ref, pin_ref, pout_ref,
     y_ref, ht_ref, st_ref, hb_ref, g_ref) = refs[2 * nseq:]
    q = 2 * nseq
    r = nseq * SSM_SUB
    hs = SSM_HALF_STATE
    c = pl.program_id(0)

    @pl.when(c == 0)
    def _():
        st_ref[...] = h0_ref[...]

    lane = lax.broadcasted_iota(jnp.int32, (r, SSM_WIDTH), 1)
    low = (lane & 128) == 0
    rows2 = lax.broadcasted_iota(jnp.int32, (q * SSM_SUB, SSM_WIDTH), 0)
    lane2 = lax.broadcasted_iota(jnp.int32, (q * SSM_SUB, SSM_WIDTH), 1)
    keep = ((rows2 & 1) == 0) == ((lane2 & 128) == 0)

    def sub_tile(s, carry):
        t0 = pl.multiple_of(s * SSM_SUB, SSM_SUB)
        u = jnp.concatenate([u_refs[j][pl.ds(t0, SSM_SUB), :] for j in range(nseq)], axis=0)
        ub = u.astype(BF16)
        zero = jnp.zeros_like(ub)
        stacked = jnp.concatenate([jnp.where(low, ub, zero), jnp.where(low, zero, ub)], axis=0)
        lall = _dot(pin_ref[...], stacked).astype(BF16)
        for p in range(SSM_PAIRS):
            hb_ref[p] = _dot(lall[:, 256 * p:256 * (p + 1)], bmat_ref[p])
        for p in range(SSM_PAIRS):
            ar = are_ref[p]
            ai = aim_ref[p]

            def step(t, hc, p=p, ar=ar, ai=ai):
                hr, hi = hc
                r0 = pl.multiple_of(t * q, q)
                bur = hb_ref[p, pl.ds(r0, q), 0:hs]
                bui = hb_ref[p, pl.ds(r0, q), hs:2 * hs]
                nr = ar * hr - ai * hi + bur
                ni = ar * hi + ai * hr + bui
                hb_ref[p, pl.ds(r0, q), 0:hs] = nr
                hb_ref[p, pl.ds(r0, q), hs:2 * hs] = ni
                return nr, ni

            hr, hi = lax.fori_loop(0, SSM_SUB, step, (st_ref[p, :, 0:hs], st_ref[p, :, hs:2 * hs]), unroll=8)
            st_ref[p, :, 0:hs] = hr
            st_ref[p, :, hs:2 * hs] = hi
        y2 = jnp.concatenate([_dot(hb_ref[p].astype(BF16), cmat_ref[p]) for p in range(SSM_PAIRS)], axis=1)
        zf = jnp.where(keep, y2, 0.0)
        z_hi = zf.astype(BF16)
        r1 = zf - z_hi.astype(F32)
        z_mid = r1.astype(BF16)
        z_lo = (r1 - z_mid.astype(F32)).astype(BF16)
        pout = pout_ref[...]
        yn = _dot(pout, z_hi) + _dot(pout, z_mid) + _dot(pout, z_lo)
        g = _gelu_tanh(yn + d_ref[...] * u).astype(BF16)
        for j in range(nseq):
            g_ref[pl.ds(pl.multiple_of(j * tt + t0, SSM_SUB), SSM_SUB), :] = g[j * SSM_SUB:(j + 1) * SSM_SUB]
        return carry

    lax.fori_loop(0, tt // SSM_SUB, sub_tile, 0)

    g = g_ref[...]
    ga = _dot(g, wglu_ref[:, :SSM_WIDTH])
    gb = _dot(g, wglu_ref[:, SSM_WIDTH:])
    z = jnp.concatenate([z_refs[j][...] for j in range(nseq)], axis=0)
    out = (ga * jax.nn.sigmoid(gb) * _silu(z)).astype(BF16)
    for j in range(nseq):
        y_ref[j] = out[j * tt:(j + 1) * tt]

    @pl.when(c == pl.num_programs(0) - 1)
    def _():
        ht_ref[...] = st_ref[...]


def _ssm_perms(nseq):
    r, q = nseq * SSM_SUB, 2 * nseq
    pin = np.zeros((2 * r, 2 * r), np.float32)
    pout = np.zeros((r, 2 * r), np.float32)
    for t in range(SSM_SUB):
        for j in range(nseq):
            for h in range(2):
                pin[t * q + 2 * j + h, h * r + j * SSM_SUB + t] = 1.0
                pout[j * SSM_SUB + t, t * q + 2 * j + h] = 1.0
    return jnp.asarray(pin, BF16), jnp.asarray(pout, BF16)


def _ssm(p, d, w_glu, bmat, cmat, a_re, a_im, h0, *, nseq, s, tt, row0, l, name):
    nt = s // tt
    q = 2 * nseq
    rb0 = row0 // tt
    pin, pout = _ssm_perms(nseq)
    seq_spec = lambda j, col: pl.BlockSpec((tt, SSM_WIDTH), lambda c: (rb0 + j * nt + c, col))
    whole = lambda a: pl.BlockSpec(a.shape, lambda c: (0,) * a.ndim)
    in_specs = ([seq_spec(j, P_U // SSM_WIDTH) for j in range(nseq)]
                + [seq_spec(j, P_ZS // SSM_WIDTH) for j in range(nseq)]
                + [pl.BlockSpec((1, SSM_WIDTH), lambda c: (0, 0)),
                   pl.BlockSpec((SSM_WIDTH, 2 * SSM_WIDTH), lambda c: (l, 0)),
                   whole(bmat), whole(cmat), whole(a_re), whole(a_im), whole(h0), whole(pin), whole(pout)])
    return pl.pallas_call(
        functools.partial(_ssm_kernel, nseq=nseq, tt=tt),
        grid=(nt,),
        in_specs=in_specs,
        out_specs=[pl.BlockSpec((nseq, tt, SSM_WIDTH), lambda c: (0, c, 0)),
                   pl.BlockSpec((SSM_PAIRS, q, 2 * SSM_HALF_STATE), lambda c: (0, 0, 0))],
        out_shape=[jax.ShapeDtypeStruct((nseq, s, SSM_WIDTH), BF16),
                   jax.ShapeDtypeStruct((SSM_PAIRS, q, 2 * SSM_HALF_STATE), F32)],
        scratch_shapes=[pltpu.VMEM((SSM_PAIRS, q, 2 * SSM_HALF_STATE), F32),
                        pltpu.VMEM((SSM_PAIRS, q * SSM_SUB, 2 * SSM_HALF_STATE), F32),
                        pltpu.VMEM((nseq * tt, SSM_WIDTH), BF16)],
        compiler_params=_params(("arbitrary",)),
        name=name,
    )(*([p] * (2 * nseq)), d, w_glu, bmat, cmat, a_re, a_im, h0, pin, pout)


def _ssm_state_in(h_re, h_im, nseq):
    def arr(h):
        return h.reshape(nseq, SSM_PAIRS, 2, SSM_HALF_STATE).transpose(1, 0, 2, 3).reshape(
            SSM_PAIRS, nseq * 2, SSM_HALF_STATE)
    return jnp.concatenate([arr(h_re), arr(h_im)], axis=-1)


def _ssm_state_out(ht, nseq):
    def arr(h):
        return h.reshape(SSM_PAIRS, nseq, 2, 8, SSM_STATE).transpose(1, 0, 2, 3, 4).reshape(
            nseq, SSM_GROUPS, SSM_STATE)
    return arr(ht[..., :SSM_HALF_STATE]), arr(ht[..., SSM_HALF_STATE:])


def _ssm_mats(bb_re_t, bb_im_t, c_re, c_im):
    eye8 = jnp.eye(8, dtype=F32)

    def bmat(bb):
        b6 = bb.reshape(SSM_PAIRS, 2, 8, SSM_GROUP, SSM_STATE)
        return jnp.einsum('ahgcp,gk->ahgckp', b6, eye8).reshape(SSM_PAIRS, 256, SSM_HALF_STATE)

    def cmat(cc):
        c6 = cc.reshape(SSM_PAIRS, 2, 8, SSM_GROUP, SSM_STATE)
        return jnp.einsum('ahgcp,gk->akphgc', c6, eye8).reshape(SSM_PAIRS, SSM_HALF_STATE, 256)

    b_all = jnp.concatenate([bmat(bb_re_t), bmat(bb_im_t)], axis=2).astype(BF16)
    c_all = jnp.concatenate([cmat(c_re), cmat(-c_im)], axis=1).astype(BF16)
    return b_all, c_all


def _ssm_a_rows(a, nseq):
    a3 = a.reshape(SSM_PAIRS, 1, 2, SSM_HALF_STATE)
    return jnp.broadcast_to(a3, (SSM_PAIRS, nseq, 2, SSM_HALF_STATE)).reshape(
        SSM_PAIRS, nseq * 2, SSM_HALF_STATE)


OUT_TM = 256


def _merge_out_kernel(x_ref, ys_ref, ya_ref, ym_ref, g0_ref, g1_ref, g2_ref,
                      ws_ref, wa_ref, wm_ref, wo_ref, gp_ref, o_ref):
    merged = (jax.nn.sigmoid(g0_ref[...]) * _dot(ys_ref[...], ws_ref[...])
              + jax.nn.sigmoid(g1_ref[...]) * _dot(ya_ref[...], wa_ref[...])
              + jax.nn.sigmoid(g2_ref[...]) * _dot(ym_ref[...], wm_ref[...]))
    out = _dot(merged.astype(BF16), wo_ref[...])
    o_ref[...] = x_ref[...] + _rms(out, gp_ref[...])


def _merge_out(x, y_ssm, y_mla, y_mem, p, w_ssm_o, w_mla_o, w_mem_o, w_out, g_post, l):
    tm = OUT_TM
    row = lambda c: (lambda i: (i, c))
    const = lambda i: (0, 0)
    resident = lambda shape: pl.BlockSpec(shape, lambda i: (l, 0), pipeline_mode=pl.Buffered(1))
    return pl.pallas_call(
        _merge_out_kernel,
        grid=(N_TOK // tm,),
        in_specs=[pl.BlockSpec((tm, D_MODEL), row(0)),
                  pl.BlockSpec((tm, SSM_WIDTH), row(0)),
                  pl.BlockSpec((tm, MLA_WIDTH), row(0)),
                  pl.BlockSpec((tm, MEM_WIDTH), row(0)),
                  pl.BlockSpec((tm, D_MODEL), row(0)),
                  pl.BlockSpec((tm, D_MODEL), row(1)),
                  pl.BlockSpec((tm, D_MODEL), row(2)),
                  resident((SSM_WIDTH, D_MODEL)),
                  resident((MLA_WIDTH, D_MODEL)),
                  resident((MEM_WIDTH, D_MODEL)),
                  resident((D_MODEL, D_MODEL)),
                  pl.BlockSpec((1, D_MODEL), const)],
        out_specs=pl.BlockSpec((tm, D_MODEL), row(0)),
        out_shape=jax.ShapeDtypeStruct((N_TOK, D_MODEL), F32),
        compiler_params=_params(("parallel",)),
        name="merge_out",
    )(x, y_ssm, y_mla, y_mem, p, p, p, w_ssm_o, w_mla_o, w_mem_o, w_out, g_post)


def _rot_half_cols(w):
    half = w.shape[-1] // 2
    return jnp.concatenate([w[..., half:], w[..., :half]], axis=-1)


def _stack_rows(w):
    return w.astype(BF16).reshape(w.shape[0] * w.shape[1], w.shape[2])


def _prep_small_weights(w_in, w_uq, w_uk, w_uv):
    k0 = 2 * SSM_WIDTH + Q_LORA + KV_LORA
    wk = w_in[:, :, k0:k0 + QK_ROPE]
    w_kr = jnp.concatenate([wk, _rot_half_cols(wk)], axis=-1)
    uq = w_uq.reshape(DEPTH, Q_LORA, MLA_HEADS, QK_NOPE + QK_ROPE)
    uq_nope = uq[..., :QK_NOPE].reshape(DEPTH, Q_LORA, MLA_WIDTH)
    uq_rope = uq[..., QK_NOPE:]
    uq_rr = jnp.concatenate([uq_rope, _rot_half_cols(uq_rope)], axis=-1).reshape(DEPTH, Q_LORA, MLA_WIDTH)
    w_uq2 = jnp.concatenate([uq_nope, uq_rr], axis=-1)
    w_ukv = jnp.concatenate([w_uk, w_uv], axis=-1)
    return _stack_rows(w_kr), _stack_rows(w_uq2), _stack_rows(w_ukv)


def _rope_table():
    half = QK_ROPE // 2
    pos = jnp.concatenate([jnp.tile(jnp.arange(SEQ, dtype=jnp.int32), BATCH),
                           jnp.tile(PAST_LEN + jnp.arange(DEC_SEQ, dtype=jnp.int32), DEC_BATCH)])
    inv = ROPE_THETA ** (-jnp.arange(half, dtype=F32) / half)
    ang = pos.astype(F32)[:, None] * inv[None, :]
    cos, sin = jnp.cos(ang), jnp.sin(ang)
    return jnp.concatenate([cos, cos, -sin, sin], axis=-1)


def kernel(x_prompt, x_sample, cache_mla_ckv, cache_mla_krope, cache_mem_k, cache_mem_v, state_ssm_re, state_ssm_im, mem_prompt, norm_pre, w_in, ssm_a_re, ssm_a_im, ssm_log_dt, ssm_b_re, ssm_b_im, ssm_c_re, ssm_c_im, ssm_d, w_glu, mla_q_norm, w_uq, mla_kv_norm, w_uk, w_uv, w_mem_k, w_mem_v, w_ssm_o, w_mla_o, w_mem_o, w_out, norm_post):
    w_main = _w_relayout(w_in.reshape(DEPTH * D_MODEL, W_IN_COLS))
    w_kr, w_uq2, w_ukv = _prep_small_weights(w_in, w_uq, w_uk, w_uv)
    w_glu_b, w_ssm_o_b, w_mla_o_b, w_mem_o_b, w_out_b = (
        _stack_rows(w) for w in (w_glu, w_ssm_o, w_mla_o, w_mem_o, w_out))
    w_mem_kv = _stack_rows(jnp.concatenate([w_mem_k, w_mem_v], axis=-1))
    cs = _rope_table()
    mem_rows = mem_prompt.reshape(BATCH * N_MEM, D_MODEL)
    ckv_cache = cache_mla_ckv.reshape(DEPTH * DEC_BATCH * PAST_LEN, KV_LORA)
    krc_all = jnp.pad(cache_mla_krope.astype(BF16), ((0, 0), (0, 0), (0, 0), (0, 128 - QK_ROPE))).reshape(
        DEPTH * DEC_BATCH * PAST_LEN, 128)
    mem_k_cache = cache_mem_k.reshape(DEPTH * DEC_BATCH * N_MEM, MEM_WIDTH)
    mem_v_cache = cache_mem_v.reshape(DEPTH * DEC_BATCH * N_MEM, MEM_WIDTH)

    zeros_state = jnp.zeros((BATCH, SSM_GROUPS, SSM_STATE), F32)
    x = jnp.concatenate([x_prompt.reshape(N_PROMPT, D_MODEL), x_sample.reshape(N_SAMPLE, D_MODEL)], axis=0)

    p_re, p_im, p_ckv, p_kr, p_mk, p_mv = [], [], [], [], [], []
    s_re, s_im, s_ckv, s_kr = [], [], [], []
    for l in range(DEPTH):
        p, kr = _in_proj(x, norm_pre[l][None], w_main, w_kr, l)

        ab_re, ab_im, bb_re_t, bb_im_t = _s5_discretize(
            ssm_a_re[l], ssm_a_im[l], ssm_log_dt[l],
            ssm_b_re[l].transpose(0, 2, 1), ssm_b_im[l].transpose(0, 2, 1))
        bmat, cmat = _ssm_mats(bb_re_t, bb_im_t, ssm_c_re[l], ssm_c_im[l])
        d = ssm_d[l][None]
        ys_p, ht_p = _ssm(p, d, w_glu_b, bmat, cmat, _ssm_a_rows(ab_re, BATCH), _ssm_a_rows(ab_im, BATCH),
                          _ssm_state_in(zeros_state, zeros_state, BATCH),
                          nseq=BATCH, s=SEQ, tt=128, row0=0, l=l, name="ssm_prompt")
        ys_s, ht_s = _ssm(p, d, w_glu_b, bmat, cmat, _ssm_a_rows(ab_re, DEC_BATCH), _ssm_a_rows(ab_im, DEC_BATCH),
                          _ssm_state_in(state_ssm_re[l], state_ssm_im[l], DEC_BATCH),
                          nseq=DEC_BATCH, s=DEC_SEQ, tt=DEC_SEQ, row0=N_PROMPT, l=l, name="ssm_sample")
        y_ssm = jnp.concatenate([ys_p.reshape(N_PROMPT, SSM_WIDTH), ys_s.reshape(N_SAMPLE, SSM_WIDTH)], axis=0)
        hr, hi = _ssm_state_out(ht_p, BATCH)
        p_re.append(hr); p_im.append(hi)
        hr, hi = _ssm_state_out(ht_s, DEC_BATCH)
        s_re.append(hr); s_im.append(hi)

        qn, qr, ckv, kn, v, kro, krp = _mla_prep(p, kr, cs, mla_q_norm[l][None], mla_kv_norm[l][None],
                                                 w_uq2, w_ukv, l)
        kc, vc = _matmul(ckv_cache, w_ukv, [MLA_WIDTH, MLA_WIDTH], BF16, 1024, "kv_up_cache",
                         m=DEC_BATCH * PAST_LEN, x_row0=l * DEC_BATCH * PAST_LEN, l=l)
        y_mla = _attn_prompt(qn, qr, kn, krp, v, p)
        y_mla = _attn_sample(y_mla, qn, qr, kc, krc_all, vc, kn, krp, v, p, l)
        p_ckv.append(ckv[:N_PROMPT].reshape(BATCH, SEQ, KV_LORA))
        s_ckv.append(ckv[N_PROMPT:].reshape(DEC_BATCH, DEC_SEQ, KV_LORA))
        p_kr.append(kro[:N_PROMPT].reshape(BATCH, SEQ, QK_ROPE))
        s_kr.append(kro[N_PROMPT:].reshape(DEC_BATCH, DEC_SEQ, QK_ROPE))

        mk, mv = _matmul(mem_rows, w_mem_kv, [MEM_WIDTH, MEM_WIDTH], F32, 512, "mem_kv",
                         m=BATCH * N_MEM, l=l)
        y_mem = _mem_attn(p, mk, mv, nb=BATCH, tq=512, row0=0)
        y_mem = _mem_attn(p, mem_k_cache, mem_v_cache, nb=DEC_BATCH, tq=DEC_SEQ, row0=N_PROMPT,
                          mem_blk0=l * DEC_BATCH, y_prev=y_mem)
        p_mk.append(mk.reshape(BATCH, N_MEM, MEM_HEADS, MEM_HEAD_DIM))
        p_mv.append(mv.reshape(BATCH, N_MEM, MEM_HEADS, MEM_HEAD_DIM))

        x = _merge_out(x, y_ssm, y_mla, y_mem, p, w_ssm_o_b, w_mla_o_b, w_mem_o_b, w_out_b,
                       norm_post[l][None], l)

    return (x[:N_PROMPT].reshape(BATCH, SEQ, D_MODEL), x[N_PROMPT:].reshape(DEC_BATCH, DEC_SEQ, D_MODEL),
            jnp.stack(p_re), jnp.stack(p_im), jnp.stack(p_ckv), jnp.stack(p_kr), jnp.stack(p_mk), jnp.stack(p_mv),
            jnp.stack(s_re), jnp.stack(s_im), jnp.stack(s_ckv), jnp.stack(s_kr))
```

```python
import functools
import math

import jax
import jax.numpy as jnp
import numpy as np
from jax import lax
from jax.experimental import pallas as pl
from jax.experimental.pallas import tpu as pltpu

D_MODEL = 2048
BATCH = 4
SEQ = 2048
DEPTH = 4
DEC_BATCH = 8
DEC_SEQ = 32
PAST_LEN = 2048
CHUNK = 64
N_MEM = 256
EPS = 1e-6
NEG_INF = -1e30
SSM_WIDTH = 1024
SSM_GROUP = 16
SSM_GROUPS = 64
SSM_STATE = 64
MLA_HEADS = 8
QK_NOPE = 128
QK_ROPE = 64
V_HEAD = 128
Q_LORA = 512
KV_LORA = 512
MLA_WIDTH = 1024
ROPE_THETA = 10000.0
MEM_HEADS = 4
MEM_HEAD_DIM = 128
MEM_WIDTH = 512

N_PROMPT = BATCH * SEQ
N_SAMPLE = DEC_BATCH * DEC_SEQ
N_TOK = N_PROMPT + N_SAMPLE

P_GATE = 0
P_U = 6144
P_ZS = 7168
P_CQ = 8192
P_CKV = 8704
P_ZA = 9216
P_QM = 10240
P_ZM = 10752
P_COLS = 11264
W_IN_COLS = 11328

MLA_SCALE = (QK_NOPE + QK_ROPE) ** -0.5
MEM_SCALE = MEM_HEAD_DIM ** -0.5

V7X_VMEM_LIMIT = 52 * 1024 * 1024

BF16 = jnp.bfloat16
F32 = jnp.float32


def _params(sem, vmem=V7X_VMEM_LIMIT):
    return pltpu.CompilerParams(dimension_semantics=sem, vmem_limit_bytes=vmem)


def _dot(a, b):
    return jnp.dot(a, b, preferred_element_type=F32)


def _dot_nt(a, b):
    return lax.dot_general(a, b, (((1,), (1,)), ((), ())), preferred_element_type=F32)


def _rms(x, g):
    return x * lax.rsqrt(jnp.mean(x * x, axis=-1, keepdims=True) + EPS) * g


def _silu(x):
    return x * jax.nn.sigmoid(x)


def _gelu_tanh(x):
    return 0.5 * x * (1.0 + jnp.tanh(math.sqrt(2.0 / math.pi) * (x + 0.044715 * (x * x * x))))


RL_TM = 512
IN_TM = 768
IN_TN = 1024


RL_KR_TILE = 9


def _w_relayout_kernel(a_ref, b_ref, o_ref, kr_ref):
    j = pl.program_id(1)
    shifted = jnp.logical_or(j < 6, j >= RL_KR_TILE)

    @pl.when(shifted)
    def _():
        full = jnp.concatenate([a_ref[...], b_ref[...]], axis=1)
        o_ref[...] = full[:, QK_ROPE:QK_ROPE + IN_TN].astype(BF16)

    @pl.when(jnp.logical_not(shifted))
    def _():
        o_ref[...] = a_ref[...].astype(BF16)

    @pl.when(j == RL_KR_TILE)
    def _():
        kr_ref[...] = a_ref[:, :128]


def _w_relayout(w_in2d):
    a_blk = lambda j: jnp.where(j < 6, 5 + j, j - 6)
    b_blk = lambda j: jnp.where(j < 6, 48 + 8 * j, jnp.where(j >= RL_KR_TILE, 8 * j - 40, 0))
    rows = w_in2d.shape[0]
    return pl.pallas_call(
        _w_relayout_kernel,
        grid=(rows // RL_TM, P_COLS // IN_TN),
        in_specs=[pl.BlockSpec((RL_TM, IN_TN), lambda i, j: (i, a_blk(j))),
                  pl.BlockSpec((RL_TM, 128), lambda i, j: (i, b_blk(j)))],
        out_specs=[pl.BlockSpec((RL_TM, IN_TN), lambda i, j: (i, j)),
                   pl.BlockSpec((RL_TM, 128), lambda i, j: (i, 0))],
        out_shape=[jax.ShapeDtypeStruct((rows, P_COLS), BF16),
                   jax.ShapeDtypeStruct((rows, 128), F32)],
        compiler_params=_params(("parallel", "arbitrary")),
        name="w_in_relayout",
    )(w_in2d, w_in2d)


def _in_proj_kernel(x_ref, g_ref, w_ref, wkr_ref, p_ref, kr_ref, h_ref):
    @pl.when(pl.program_id(1) == 0)
    def _():
        h = _rms(x_ref[...], g_ref[...]).astype(BF16)
        h_ref[...] = h
        kr_ref[...] = _dot(h, wkr_ref[...])

    p_ref[...] = _dot(h_ref[...], w_ref[...])


def _in_proj(x, g_pre, w, wkr, l):
    return pl.pallas_call(
        _in_proj_kernel,
        grid=(N_TOK // IN_TM, P_COLS // IN_TN),
        in_specs=[
            pl.BlockSpec((IN_TM, D_MODEL), lambda i, j: (i, 0)),
            pl.BlockSpec((1, D_MODEL), lambda i, j: (0, 0)),
            pl.BlockSpec((D_MODEL, IN_TN), lambda i, j: (l, j)),
            pl.BlockSpec((D_MODEL, 128), lambda i, j: (l, 0)),
        ],
        out_specs=[
            pl.BlockSpec((IN_TM, IN_TN), lambda i, j: (i, j)),
            pl.BlockSpec((IN_TM, 128), lambda i, j: (i, 0)),
        ],
        out_shape=[
            jax.ShapeDtypeStruct((N_TOK, P_COLS), F32),
            jax.ShapeDtypeStruct((N_TOK, 128), F32),
        ],
        scratch_shapes=[pltpu.VMEM((IN_TM, D_MODEL), BF16)],
        compiler_params=_params(("parallel", "arbitrary")),
        name="in_proj",
    )(x, g_pre, w, wkr)


PREP_TM = 256
PREP_PROMPT_TILES = N_PROMPT // PREP_TM


def _mla_prep_kernel(*refs, n_alias):
    (cq_ref, ckv_ref, kr_ref, cs_ref, qg_ref, kvg_ref, wuq_ref, wukv_ref,
     qn_ref, qr_ref, kn_ref, v_ref, krp_ref, ckvp_ref, ckvs_ref, krop_ref, kros_ref) = refs[n_alias:]
    i = pl.program_id(0)
    cs = cs_ref[...]
    hq = _rms(cq_ref[...], qg_ref[...]).astype(BF16)
    q = _dot(hq, wuq_ref[...])
    qn_ref[...] = (q[:, :MLA_WIDTH] * MLA_SCALE).astype(BF16)
    parts = []
    for h in range(MLA_HEADS):
        t = q[:, MLA_WIDTH + 128 * h:MLA_WIDTH + 128 * (h + 1)] * cs
        parts.append(((t + pltpu.roll(t, 64, axis=1)) * MLA_SCALE).astype(BF16))
    qr_ref[...] = jnp.concatenate(parts, axis=1)

    ckv = _rms(ckv_ref[...], kvg_ref[...])
    kv = _dot(ckv.astype(BF16), wukv_ref[...])
    kn_ref[...] = kv[:, :MLA_WIDTH].astype(BF16)
    v_ref[...] = kv[:, MLA_WIDTH:].astype(BF16)

    tk = kr_ref[...] * cs
    rk = tk + pltpu.roll(tk, 64, axis=1)
    lane = lax.broadcasted_iota(jnp.int32, rk.shape, 1)
    krp_ref[...] = jnp.where(lane < QK_ROPE, rk, 0.0).astype(BF16)

    @pl.when(i < PREP_PROMPT_TILES)
    def _():
        ckvp_ref[...] = ckv
        krop_ref[...] = rk[:, :QK_ROPE]

    @pl.when(i == PREP_PROMPT_TILES)
    def _():
        ckvs_ref[...] = ckv
        kros_ref[...] = rk[:, :QK_ROPE]


def _mla_prep(p, kr, cs, qg, kvg, wuq, wukv, l, stacked):
    tm = PREP_TM
    row = lambda c: (lambda i: (i, c))
    const = lambda i: (0, 0)
    layer = lambda i: (l, 0)
    prompt_blk = lambda i: (l * PREP_PROMPT_TILES + jnp.minimum(i, PREP_PROMPT_TILES - 1), 0)
    n_alias = 0 if stacked is None else 4
    in_specs = [
        pl.BlockSpec((tm, Q_LORA), row(P_CQ // Q_LORA)),
        pl.BlockSpec((tm, KV_LORA), row(P_CKV // KV_LORA)),
        pl.BlockSpec((tm, 128), row(0)),
        pl.BlockSpec((tm, 128), row(0)),
        pl.BlockSpec((1, Q_LORA), const),
        pl.BlockSpec((1, KV_LORA), const),
        pl.BlockSpec((Q_LORA, 2048), layer),
        pl.BlockSpec((KV_LORA, 2048), layer),
    ]
    args = [p, p, kr, cs, qg, kvg, wuq, wukv]
    if stacked is not None:
        in_specs = [pl.BlockSpec(memory_space=pl.ANY)] * 4 + in_specs
        args = list(stacked) + args
    return pl.pallas_call(
        functools.partial(_mla_prep_kernel, n_alias=n_alias),
        grid=(N_TOK // tm,),
        in_specs=in_specs,
        out_specs=[
            pl.BlockSpec((tm, MLA_WIDTH), row(0)),
            pl.BlockSpec((tm, MLA_WIDTH), row(0)),
            pl.BlockSpec((tm, MLA_WIDTH), row(0)),
            pl.BlockSpec((tm, MLA_WIDTH), row(0)),
            pl.BlockSpec((tm, 128), row(0)),
            pl.BlockSpec((tm, KV_LORA), prompt_blk),
            pl.BlockSpec((tm, KV_LORA), layer),
            pl.BlockSpec((tm, QK_ROPE), prompt_blk),
            pl.BlockSpec((tm, QK_ROPE), layer),
        ],
        out_shape=[
            jax.ShapeDtypeStruct((N_TOK, MLA_WIDTH), BF16),
            jax.ShapeDtypeStruct((N_TOK, MLA_WIDTH), BF16),
            jax.ShapeDtypeStruct((N_TOK, MLA_WIDTH), BF16),
            jax.ShapeDtypeStruct((N_TOK, MLA_WIDTH), BF16),
            jax.ShapeDtypeStruct((N_TOK, 128), BF16),
            jax.ShapeDtypeStruct((DEPTH * N_PROMPT, KV_LORA), F32),
            jax.ShapeDtypeStruct((DEPTH * N_SAMPLE, KV_LORA), F32),
            jax.ShapeDtypeStruct((DEPTH * N_PROMPT, QK_ROPE), F32),
            jax.ShapeDtypeStruct((DEPTH * N_SAMPLE, QK_ROPE), F32),
        ],
        input_output_aliases={k: 5 + k for k in range(n_alias)},
        compiler_params=_params(("arbitrary",)),
        name="mla_prep",
    )(*args)


def _mm_kernel(x_ref, w_ref, *o_refs):
    y = _dot(x_ref[...].astype(BF16), w_ref[...])
    off = 0
    for o_ref in o_refs:
        n = o_ref.shape[1]
        o_ref[...] = y[:, off:off + n].astype(o_ref.dtype)
        off += n


def _mem_kv(mem_rows, w_mem_kv):
    tm = 512
    nr = BATCH * N_MEM // tm
    out = pl.BlockSpec((tm, MEM_WIDTH), lambda l, i: (l * nr + i, 0))
    return pl.pallas_call(
        _mm_kernel,
        grid=(DEPTH, nr),
        in_specs=[pl.BlockSpec((tm, D_MODEL), lambda l, i: (i, 0)),
                  pl.BlockSpec((D_MODEL, 2 * MEM_WIDTH), lambda l, i: (l, 0))],
        out_specs=[out, out],
        out_shape=[jax.ShapeDtypeStruct((DEPTH * BATCH * N_MEM, MEM_WIDTH), F32)] * 2,
        compiler_params=_params(("parallel", "parallel")),
        name="mem_kv",
    )(mem_rows, w_mem_kv)


ATT_TQ = 512


def _softmax_pv(parts, z):
    m = functools.reduce(jnp.maximum, [jnp.max(s, axis=1, keepdims=True) for s, _ in parts])
    l = 0.0
    o = 0.0
    for s, v in parts:
        e = jnp.exp(s - m)
        l = l + jnp.sum(e, axis=1, keepdims=True)
        o = o + _dot(e.astype(BF16), v)
    return o * (1.0 / l) * _silu(z)


def _chunk_mask(q_pos0, k_pos0, nq, nk):
    qp = q_pos0 + lax.broadcasted_iota(jnp.int32, (nq, nk), 0)
    kp = k_pos0 + lax.broadcasted_iota(jnp.int32, (nq, nk), 1)
    sh = CHUNK.bit_length() - 1
    return lax.shift_right_logical(kp, sh) <= lax.shift_right_logical(qp, sh)


def _attn_prompt_kernel(qn_ref, qr_ref, kn_ref, kr_ref, v_ref, z_ref, o_ref):
    tq = ATT_TQ
    for qi in range(SEQ // tq):
        lo, hi = qi * tq, (qi + 1) * tq
        qn = qn_ref[lo:hi, :]
        qr = qr_ref[lo:hi, :]
        sd = _dot_nt(qn, kn_ref[lo:hi, :]) + _dot_nt(qr, kr_ref[lo:hi, :])
        sd = jnp.where(_chunk_mask(lo, lo, tq, tq), sd, NEG_INF)
        parts = [(sd, v_ref[lo:hi, :])]
        if qi > 0:
            so = _dot_nt(qn, kn_ref[0:lo, :]) + _dot_nt(qr, kr_ref[0:lo, :])
            parts.append((so, v_ref[0:lo, :]))
        o_ref[lo:hi, :] = _softmax_pv(parts, z_ref[lo:hi, :]).astype(BF16)


def _attn_prompt(qn, qr, kn, krp, v, p):
    blk = lambda c0: pl.BlockSpec((SEQ, 128), lambda b, h: (b, c0 + h))
    return pl.pallas_call(
        _attn_prompt_kernel,
        grid=(BATCH, MLA_HEADS),
        in_specs=[blk(0), blk(0), blk(0),
                  pl.BlockSpec((SEQ, 128), lambda b, h: (b, 0)),
                  blk(0), blk(P_ZA // 128)],
        out_specs=blk(0),
        out_shape=jax.ShapeDtypeStruct((N_TOK, MLA_WIDTH), BF16),
        compiler_params=_params(("parallel", "parallel")),
        name="attn_prompt",
    )(qn, qr, kn, krp, v, p)


def _attn_sample_kernel(y_hbm, qn_ref, qr_ref, ckvc_ref, krc_ref, ckvn_ref, krn_ref, wukv_ref, z_ref, o_ref):
    del y_hbm
    nh, d = MLA_HEADS, 128
    qa = jnp.concatenate(
        [_dot_nt(qn_ref[:, d * h:d * (h + 1)], wukv_ref[:, d * h:d * (h + 1)]) for h in range(nh)],
        axis=0).astype(BF16)
    qr = jnp.concatenate([qr_ref[:, d * h:d * (h + 1)] for h in range(nh)], axis=0)
    ckvc = ckvc_ref[...].astype(BF16)
    ckvn = ckvn_ref[...].astype(BF16)
    rows = nh * DEC_SEQ

    def mask(k_pos0, nk):
        qp = PAST_LEN + (lax.broadcasted_iota(jnp.int32, (rows, nk), 0) & (DEC_SEQ - 1))
        kp = k_pos0 + lax.broadcasted_iota(jnp.int32, (rows, nk), 1)
        sh = CHUNK.bit_length() - 1
        return lax.shift_right_logical(kp, sh) <= lax.shift_right_logical(qp, sh)

    sp = jnp.where(mask(0, PAST_LEN), _dot_nt(qa, ckvc) + _dot_nt(qr, krc_ref[...]), NEG_INF)
    sn = jnp.where(mask(PAST_LEN, DEC_SEQ), _dot_nt(qa, ckvn) + _dot_nt(qr, krn_ref[...]), NEG_INF)
    m = jnp.maximum(jnp.max(sp, axis=1, keepdims=True), jnp.max(sn, axis=1, keepdims=True))
    ep, en = jnp.exp(sp - m), jnp.exp(sn - m)
    l = jnp.sum(ep, axis=1, keepdims=True) + jnp.sum(en, axis=1, keepdims=True)
    lat = ((_dot(ep.astype(BF16), ckvc) + _dot(en.astype(BF16), ckvn)) * (1.0 / l)).astype(BF16)
    o = jnp.concatenate(
        [_dot(lat[DEC_SEQ * h:DEC_SEQ * (h + 1)], wukv_ref[:, MLA_WIDTH + d * h:MLA_WIDTH + d * (h + 1)])
         for h in range(nh)], axis=1)
    o_ref[...] = (o * _silu(z_ref[...])).astype(BF16)


def _attn_sample(y_mla, qn, qr, ckv_cache, krc, ckv_s, krp, wukv, p, l):
    assert DEC_SEQ & (DEC_SEQ - 1) == 0
    r0 = N_PROMPT // DEC_SEQ
    new = lambda width, c: pl.BlockSpec((DEC_SEQ, width), lambda b: (r0 + b, c))
    cache = lambda width: pl.BlockSpec((PAST_LEN, width), lambda b: (l * DEC_BATCH + b, 0))
    return pl.pallas_call(
        _attn_sample_kernel,
        grid=(DEC_BATCH,),
        in_specs=[pl.BlockSpec(memory_space=pl.ANY),
                  new(MLA_WIDTH, 0), new(MLA_WIDTH, 0),
                  cache(KV_LORA), cache(128),
                  pl.BlockSpec((DEC_SEQ, KV_LORA), lambda b: (l * DEC_BATCH + b, 0)),
                  new(128, 0),
                  pl.BlockSpec((KV_LORA, 2048), lambda b: (l, 0)),
                  new(MLA_WIDTH, P_ZA // MLA_WIDTH)],
        out_specs=new(MLA_WIDTH, 0),
        out_shape=jax.ShapeDtypeStruct((N_TOK, MLA_WIDTH), BF16),
        input_output_aliases={0: 0},
        compiler_params=_params(("parallel",)),
        name="attn_sample",
    )(y_mla, qn, qr, ckv_cache, krc, ckv_s, krp, wukv, p)


def _mem_attn_kernel(*refs, aliased):
    if aliased:
        refs = refs[1:]
    q_ref, z_ref, k_ref, v_ref, o_ref = refs
    q = q_ref[...] * MEM_SCALE
    outs = []
    for h in range(MEM_HEADS):
        sl = slice(MEM_HEAD_DIM * h, MEM_HEAD_DIM * (h + 1))
        s = _dot_nt(q[:, sl].astype(BF16), k_ref[:, sl].astype(BF16))
        m = jnp.max(s, axis=1, keepdims=True)
        e = jnp.exp(s - m)
        l = jnp.sum(e, axis=1, keepdims=True)
        outs.append(_dot(e.astype(BF16), v_ref[:, sl].astype(BF16)) * (1.0 / l))
    o_ref[...] = (jnp.concatenate(outs, axis=1) * _silu(z_ref[...])).astype(BF16)


def _mem_attn(p, mem_k, mem_v, *, nb, tq, row0, mem_blk0=0, y_prev=None):
    seq_blocks = (SEQ if y_prev is None else DEC_SEQ) // tq
    r0 = row0 // tq
    rowmap = lambda c: (lambda b, i: (r0 + b * seq_blocks + i, c))
    mem = pl.BlockSpec((N_MEM, MEM_WIDTH), lambda b, i: (mem_blk0 + b, 0))
    in_specs = [pl.BlockSpec((tq, MEM_WIDTH), rowmap(P_QM // MEM_WIDTH)),
                pl.BlockSpec((tq, MEM_WIDTH), rowmap(P_ZM // MEM_WIDTH)),
                mem, mem]
    args = [p, p, mem_k, mem_v]
    aliases = {}
    if y_prev is not None:
        in_specs = [pl.BlockSpec(memory_space=pl.ANY)] + in_specs
        args = [y_prev] + args
        aliases = {0: 0}
    return pl.pallas_call(
        functools.partial(_mem_attn_kernel, aliased=y_prev is not None),
        grid=(nb, seq_blocks),
        in_specs=in_specs,
        out_specs=pl.BlockSpec((tq, MEM_WIDTH), rowmap(0)),
        out_shape=jax.ShapeDtypeStruct((N_TOK, MEM_WIDTH), BF16),
        input_output_aliases=aliases,
        compiler_params=_params(("parallel", "parallel")),
        name="mem_attn_sample" if y_prev is not None else "mem_attn_prompt",
    )(*args)


def _s5_disc_kernel(are_ref, aim_ref, ldt_ref, bre_ref, bim_ref, abr_ref, abi_ref, bbr_ref, bbi_ref):
    dt = jnp.exp(ldt_ref[...])
    lr, li = are_ref[...], aim_ref[...]
    mag = jnp.exp(lr * dt)
    ab_re, ab_im = mag * jnp.cos(li * dt), mag * jnp.sin(li * dt)
    den = lr * lr + li * li
    nr, ni = ab_re - 1.0, ab_im
    f_re = (nr * lr + ni * li) / den
    f_im = (ni * lr - nr * li) / den
    abr_ref[...] = ab_re
    abi_ref[...] = ab_im
    br, bi = bre_ref[...], bim_ref[...]
    bbr_ref[...] = f_re * br - f_im * bi
    bbi_ref[...] = f_re * bi + f_im * br


def _s5_discretize(a_re, a_im, log_dt, b_re_t, b_im_t):
    G, P, C = SSM_GROUPS, SSM_STATE, SSM_GROUP
    ab_re, ab_im, bb_re_t, bb_im_t = pl.pallas_call(
        _s5_disc_kernel,
        out_shape=[jax.ShapeDtypeStruct((G, 1, P), F32), jax.ShapeDtypeStruct((G, 1, P), F32),
                   jax.ShapeDtypeStruct((G, C, P), F32), jax.ShapeDtypeStruct((G, C, P), F32)],
        name="s5_discretize",
    )(a_re.reshape(G, 1, P), a_im.reshape(G, 1, P), log_dt.reshape(G, 1, 1), b_re_t, b_im_t)
    return ab_re.reshape(G, P), ab_im.reshape(G, P), bb_re_t, bb_im_t


SSM_PAIRS = 4
SSM_HALF_STATE = 512
SSM_SUB = 32


def _ssm_kernel(*refs, nseq, tt):
    u_refs, z_refs = refs[:nseq], refs[nseq:2 * nseq]
    (d_ref, wglu_ref, bmat_ref, cmat_ref, are_ref, aim_ref, h0_ref, pin_ref, pout_ref,
     y_ref, ht_ref, st_ref, hb_ref, g_ref) = refs[2 * nseq:]
    q = 2 * nseq
    r = nseq * SSM_SUB
    hs = SSM_HALF_STATE
    c = pl.program_id(0)

    @pl.when(c == 0)
    def _():
        st_ref[...] = h0_ref[...]

    lane = lax.broadcasted_iota(jnp.int32, (r, SSM_WIDTH), 1)
    low = (lane & 128) == 0
    rows2 = lax.broadcasted_iota(jnp.int32, (q * SSM_SUB, SSM_WIDTH), 0)
    lane2 = lax.broadcasted_iota(jnp.int32, (q * SSM_SUB, SSM_WIDTH), 1)
    keep = ((rows2 & 1) == 0) == ((lane2 & 128) == 0)

    def sub_tile(s, carry):
        t0 = pl.multiple_of(s * SSM_SUB, SSM_SUB)
        u = jnp.concatenate([u_refs[j][pl.ds(t0, SSM_SUB), :] for j in range(nseq)], axis=0)
        ub = u.astype(BF16)
        zero = jnp.zeros_like(ub)
        stacked = jnp.concatenate([jnp.where(low, ub, zero), jnp.where(low, zero, ub)], axis=0)
        lall = _dot(pin_ref[...], stacked).astype(BF16)
        for p in range(SSM_PAIRS):
            hb_ref[p] = _dot(lall[:, 256 * p:256 * (p + 1)], bmat_ref[p])
        for p in range(SSM_PAIRS):
            ar = are_ref[p]
            ai = aim_ref[p]

            def step(t, hc, p=p, ar=ar, ai=ai):
                hr, hi = hc
                r0 = pl.multiple_of(t * q, q)
                bur = hb_ref[p, pl.ds(r0, q), 0:hs]
                bui = hb_ref[p, pl.ds(r0, q), hs:2 * hs]
                nr = ar * hr - ai * hi + bur
                ni = ar * hi + ai * hr + bui
                hb_ref[p, pl.ds(r0, q), 0:hs] = nr
                hb_ref[p, pl.ds(r0, q), hs:2 * hs] = ni
                return nr, ni

            hr, hi = lax.fori_loop(0, SSM_SUB, step, (st_ref[p, :, 0:hs], st_ref[p, :, hs:2 * hs]), unroll=8)
            st_ref[p, :, 0:hs] = hr
            st_ref[p, :, hs:2 * hs] = hi
        y2 = jnp.concatenate([_dot(hb_ref[p].astype(BF16), cmat_ref[p]) for p in range(SSM_PAIRS)], axis=1)
        zf = jnp.where(keep, y2, 0.0)
        z_hi = zf.astype(BF16)
        r1 = zf - z_hi.astype(F32)
        z_mid = r1.astype(BF16)
        z_lo = (r1 - z_mid.astype(F32)).astype(BF16)
        pout = pout_ref[...]
        yn = _dot(pout, z_hi) + _dot(pout, z_mid) + _dot(pout, z_lo)
        g = _gelu_tanh(yn + d_ref[...] * u).astype(BF16)
        for j in range(nseq):
            g_ref[pl.ds(pl.multiple_of(j * tt + t0, SSM_SUB), SSM_SUB), :] = g[j * SSM_SUB:(j + 1) * SSM_SUB]
        return carry

    lax.fori_loop(0, tt // SSM_SUB, sub_tile, 0)

    g = g_ref[...]
    ga = _dot(g, wglu_ref[:, :SSM_WIDTH])
    gb = _dot(g, wglu_ref[:, SSM_WIDTH:])
    z = jnp.concatenate([z_refs[j][...] for j in range(nseq)], axis=0)
    out = (ga * jax.nn.sigmoid(gb) * _silu(z)).astype(BF16)
    for j in range(nseq):
        y_ref[j] = out[j * tt:(j + 1) * tt]

    @pl.when(c == pl.num_programs(0) - 1)
    def _():
        ht_ref[...] = st_ref[...]


def _ssm_perms(nseq):
    r, q = nseq * SSM_SUB, 2 * nseq
    pin = np.zeros((2 * r, 2 * r), np.float32)
    pout = np.zeros((r, 2 * r), np.float32)
    for t in range(SSM_SUB):
        for j in range(nseq):
            for h in range(2):
                pin[t * q + 2 * j + h, h * r + j * SSM_SUB + t] = 1.0
                pout[j * SSM_SUB + t, t * q + 2 * j + h] = 1.0
    return jnp.asarray(pin, BF16), jnp.asarray(pout, BF16)


def _ssm(p, d, w_glu, bmat, cmat, a_re, a_im, h0, *, nseq, s, tt, row0, l, name):
    nt = s // tt
    q = 2 * nseq
    rb0 = row0 // tt
    pin, pout = _ssm_perms(nseq)
    seq_spec = lambda j, col: pl.BlockSpec((tt, SSM_WIDTH), lambda c: (rb0 + j * nt + c, col))
    whole = lambda a: pl.BlockSpec(a.shape, lambda c: (0,) * a.ndim)
    in_specs = ([seq_spec(j, P_U // SSM_WIDTH) for j in range(nseq)]
                + [seq_spec(j, P_ZS // SSM_WIDTH) for j in range(nseq)]
                + [pl.BlockSpec((1, SSM_WIDTH), lambda c: (0, 0)),
                   pl.BlockSpec((SSM_WIDTH, 2 * SSM_WIDTH), lambda c: (l, 0)),
                   whole(bmat), whole(cmat), whole(a_re), whole(a_im), whole(h0), whole(pin), whole(pout)])
    return pl.pallas_call(
        functools.partial(_ssm_kernel, nseq=nseq, tt=tt),
        grid=(nt,),
        in_specs=in_specs,
        out_specs=[pl.BlockSpec((nseq, tt, SSM_WIDTH), lambda c: (0, c, 0)),
                   pl.BlockSpec((SSM_PAIRS, q, 2 * SSM_HALF_STATE), lambda c: (0, 0, 0))],
        out_shape=[jax.ShapeDtypeStruct((nseq, s, SSM_WIDTH), BF16),
                   jax.ShapeDtypeStruct((SSM_PAIRS, q, 2 * SSM_HALF_STATE), F32)],
        scratch_shapes=[pltpu.VMEM((SSM_PAIRS, q, 2 * SSM_HALF_STATE), F32),
                        pltpu.VMEM((SSM_PAIRS, q * SSM_SUB, 2 * SSM_HALF_STATE), F32),
                        pltpu.VMEM((nseq * tt, SSM_WIDTH), BF16)],
        compiler_params=_params(("arbitrary",)),
        name=name,
    )(*([p] * (2 * nseq)), d, w_glu, bmat, cmat, a_re, a_im, h0, pin, pout)


def _ssm_state_in(h_re, h_im, nseq):
    def arr(h):
        return h.reshape(nseq, SSM_PAIRS, 2, SSM_HALF_STATE).transpose(1, 0, 2, 3).reshape(
            SSM_PAIRS, nseq * 2, SSM_HALF_STATE)
    return jnp.concatenate([arr(h_re), arr(h_im)], axis=-1)


def _ssm_state_out(ht, nseq):
    def arr(h):
        return h.reshape(SSM_PAIRS, nseq, 2, 8, SSM_STATE).transpose(1, 0, 2, 3, 4).reshape(
            nseq, SSM_GROUPS, SSM_STATE)
    return arr(ht[..., :SSM_HALF_STATE]), arr(ht[..., SSM_HALF_STATE:])


def _ssm_mats(bb_re_t, bb_im_t, c_re, c_im):
    eye8 = jnp.eye(8, dtype=F32)

    def bmat(bb):
        b6 = bb.reshape(SSM_PAIRS, 2, 8, SSM_GROUP, SSM_STATE)
        return jnp.einsum('ahgcp,gk->ahgckp', b6, eye8).reshape(SSM_PAIRS, 256, SSM_HALF_STATE)

    def cmat(cc):
        c6 = cc.reshape(SSM_PAIRS, 2, 8, SSM_GROUP, SSM_STATE)
        return jnp.einsum('ahgcp,gk->akphgc', c6, eye8).reshape(SSM_PAIRS, SSM_HALF_STATE, 256)

    b_all = jnp.concatenate([bmat(bb_re_t), bmat(bb_im_t)], axis=2).astype(BF16)
    c_all = jnp.concatenate([cmat(c_re), cmat(-c_im)], axis=1).astype(BF16)
    return b_all, c_all


def _ssm_a_rows(a, nseq):
    a3 = a.reshape(SSM_PAIRS, 1, 2, SSM_HALF_STATE)
    return jnp.broadcast_to(a3, (SSM_PAIRS, nseq, 2, SSM_HALF_STATE)).reshape(
        SSM_PAIRS, nseq * 2, SSM_HALF_STATE)


OUT_TM = 256


OUT_PROMPT_TILES = N_PROMPT // OUT_TM


def _merge_out_kernel(x_ref, ys_ref, ya_ref, ym_ref, g0_ref, g1_ref, g2_ref,
                      ws_ref, wa_ref, wm_ref, wo_ref, gp_ref, *o_refs):
    merged = (jax.nn.sigmoid(g0_ref[...]) * _dot(ys_ref[...], ws_ref[...])
              + jax.nn.sigmoid(g1_ref[...]) * _dot(ya_ref[...], wa_ref[...])
              + jax.nn.sigmoid(g2_ref[...]) * _dot(ym_ref[...], wm_ref[...]))
    out = _dot(merged.astype(BF16), wo_ref[...])
    y = x_ref[...] + _rms(out, gp_ref[...])
    if len(o_refs) == 1:
        o_refs[0][...] = y
    else:
        i = pl.program_id(0)

        @pl.when(i < OUT_PROMPT_TILES)
        def _():
            o_refs[0][...] = y

        @pl.when(i == OUT_PROMPT_TILES)
        def _():
            o_refs[1][...] = y


def _merge_out(x, y_ssm, y_mla, y_mem, p, w_ssm_o, w_mla_o, w_mem_o, w_out, g_post, l, split):
    tm = OUT_TM
    row = lambda c: (lambda i: (i, c))
    const = lambda i: (0, 0)
    resident = lambda shape: pl.BlockSpec(shape, lambda i: (l, 0), pipeline_mode=pl.Buffered(1))
    if split:
        out_specs = [pl.BlockSpec((tm, D_MODEL), lambda i: (jnp.minimum(i, OUT_PROMPT_TILES - 1), 0)),
                     pl.BlockSpec((tm, D_MODEL), const)]
        out_shape = [jax.ShapeDtypeStruct((N_PROMPT, D_MODEL), F32),
                     jax.ShapeDtypeStruct((N_SAMPLE, D_MODEL), F32)]
    else:
        out_specs = pl.BlockSpec((tm, D_MODEL), row(0))
        out_shape = jax.ShapeDtypeStruct((N_TOK, D_MODEL), F32)
    return pl.pallas_call(
        _merge_out_kernel,
        grid=(N_TOK // tm,),
        in_specs=[pl.BlockSpec((tm, D_MODEL), row(0)),
                  pl.BlockSpec((tm, SSM_WIDTH), row(0)),
                  pl.BlockSpec((tm, MLA_WIDTH), row(0)),
                  pl.BlockSpec((tm, MEM_WIDTH), row(0)),
                  pl.BlockSpec((tm, D_MODEL), row(0)),
                  pl.BlockSpec((tm, D_MODEL), row(1)),
                  pl.BlockSpec((tm, D_MODEL), row(2)),
                  resident((SSM_WIDTH, D_MODEL)),
                  resident((MLA_WIDTH, D_MODEL)),
                  resident((MEM_WIDTH, D_MODEL)),
                  resident((D_MODEL, D_MODEL)),
                  pl.BlockSpec((1, D_MODEL), const)],
        out_specs=out_specs,
        out_shape=out_shape,
        compiler_params=_params(("arbitrary",)),
        name="merge_out",
    )(x, y_ssm, y_mla, y_mem, p, p, p, w_ssm_o, w_mla_o, w_mem_o, w_out, g_post)


def _rot_half_cols(w):
    half = w.shape[-1] // 2
    return jnp.concatenate([w[..., half:], w[..., :half]], axis=-1)


def _stack_rows(w):
    return w.astype(BF16).reshape(w.shape[0] * w.shape[1], w.shape[2])


def _prep_small_weights(kr_block, w_uq, w_uk, w_uv):
    wk = kr_block[:, :QK_ROPE]
    w_kr = jnp.concatenate([wk, _rot_half_cols(wk)], axis=-1).astype(BF16)
    uq = w_uq.reshape(DEPTH, Q_LORA, MLA_HEADS, QK_NOPE + QK_ROPE)
    uq_nope = uq[..., :QK_NOPE].reshape(DEPTH, Q_LORA, MLA_WIDTH)
    uq_rope = uq[..., QK_NOPE:]
    uq_rr = jnp.concatenate([uq_rope, _rot_half_cols(uq_rope)], axis=-1).reshape(DEPTH, Q_LORA, MLA_WIDTH)
    w_uq2 = jnp.concatenate([uq_nope, uq_rr], axis=-1)
    w_ukv = jnp.concatenate([w_uk, w_uv], axis=-1)
    return w_kr, _stack_rows(w_uq2), _stack_rows(w_ukv)


def _rope_table():
    half = QK_ROPE // 2
    pos = jnp.concatenate([jnp.tile(jnp.arange(SEQ, dtype=jnp.int32), BATCH),
                           jnp.tile(PAST_LEN + jnp.arange(DEC_SEQ, dtype=jnp.int32), DEC_BATCH)])
    inv = ROPE_THETA ** (-jnp.arange(half, dtype=F32) / half)
    ang = pos.astype(F32)[:, None] * inv[None, :]
    cos, sin = jnp.cos(ang), jnp.sin(ang)
    return jnp.concatenate([cos, cos, -sin, sin], axis=-1)


def kernel(x_prompt, x_sample, cache_mla_ckv, cache_mla_krope, cache_mem_k, cache_mem_v, state_ssm_re, state_ssm_im, mem_prompt, norm_pre, w_in, ssm_a_re, ssm_a_im, ssm_log_dt, ssm_b_re, ssm_b_im, ssm_c_re, ssm_c_im, ssm_d, w_glu, mla_q_norm, w_uq, mla_kv_norm, w_uk, w_uv, w_mem_k, w_mem_v, w_ssm_o, w_mla_o, w_mem_o, w_out, norm_post):
    w_main, kr_block = _w_relayout(w_in.reshape(DEPTH * D_MODEL, W_IN_COLS))
    w_kr, w_uq2, w_ukv = _prep_small_weights(kr_block, w_uq, w_uk, w_uv)
    w_glu_b, w_ssm_o_b, w_mla_o_b, w_mem_o_b, w_out_b = (
        _stack_rows(w) for w in (w_glu, w_ssm_o, w_mla_o, w_mem_o, w_out))
    w_mem_kv = _stack_rows(jnp.concatenate([w_mem_k, w_mem_v], axis=-1))
    cs = _rope_table()
    mem_rows = mem_prompt.reshape(BATCH * N_MEM, D_MODEL)
    ckv_cache = cache_mla_ckv.reshape(DEPTH * DEC_BATCH * PAST_LEN, KV_LORA)
    krc_all = jnp.pad(cache_mla_krope.astype(BF16), ((0, 0), (0, 0), (0, 0), (0, 128 - QK_ROPE))).reshape(
        DEPTH * DEC_BATCH * PAST_LEN, 128)
    mem_k_cache = cache_mem_k.reshape(DEPTH * DEC_BATCH * N_MEM, MEM_WIDTH)
    mem_v_cache = cache_mem_v.reshape(DEPTH * DEC_BATCH * N_MEM, MEM_WIDTH)

    zeros_state = jnp.zeros((BATCH, SSM_GROUPS, SSM_STATE), F32)
    x = jnp.concatenate([x_prompt.reshape(N_PROMPT, D_MODEL), x_sample.reshape(N_SAMPLE, D_MODEL)], axis=0)

    mk_all, mv_all = _mem_kv(mem_rows, w_mem_kv)

    p_re, p_im, s_re, s_im = [], [], [], []
    stacked = None
    for l in range(DEPTH):
        p, kr = _in_proj(x, norm_pre[l][None], w_main, w_kr, l)

        ab_re, ab_im, bb_re_t, bb_im_t = _s5_discretize(
            ssm_a_re[l], ssm_a_im[l], ssm_log_dt[l],
            ssm_b_re[l].transpose(0, 2, 1), ssm_b_im[l].transpose(0, 2, 1))
        bmat, cmat = _ssm_mats(bb_re_t, bb_im_t, ssm_c_re[l], ssm_c_im[l])
        d = ssm_d[l][None]
        ys_p, ht_p = _ssm(p, d, w_glu_b, bmat, cmat, _ssm_a_rows(ab_re, BATCH), _ssm_a_rows(ab_im, BATCH),
                          _ssm_state_in(zeros_state, zeros_state, BATCH),
                          nseq=BATCH, s=SEQ, tt=128, row0=0, l=l, name="ssm_prompt")
        ys_s, ht_s = _ssm(p, d, w_glu_b, bmat, cmat, _ssm_a_rows(ab_re, DEC_BATCH), _ssm_a_rows(ab_im, DEC_BATCH),
                          _ssm_state_in(state_ssm_re[l], state_ssm_im[l], DEC_BATCH),
                          nseq=DEC_BATCH, s=DEC_SEQ, tt=DEC_SEQ, row0=N_PROMPT, l=l, name="ssm_sample")
        y_ssm = jnp.concatenate([ys_p.reshape(N_PROMPT, SSM_WIDTH), ys_s.reshape(N_SAMPLE, SSM_WIDTH)], axis=0)
        hr, hi = _ssm_state_out(ht_p, BATCH)
        p_re.append(hr); p_im.append(hi)
        hr, hi = _ssm_state_out(ht_s, DEC_BATCH)
        s_re.append(hr); s_im.append(hi)

        qn, qr, kn, v, krp, *stacked = _mla_prep(p, kr, cs, mla_q_norm[l][None], mla_kv_norm[l][None],
                                                 w_uq2, w_ukv, l, stacked)
        y_mla = _attn_prompt(qn, qr, kn, krp, v, p)
        y_mla = _attn_sample(y_mla, qn, qr, ckv_cache, krc_all, stacked[1], krp, w_ukv, p, l)

        y_mem = _mem_attn(p, mk_all, mv_all, nb=BATCH, tq=512, row0=0, mem_blk0=l * BATCH)
        y_mem = _mem_attn(p, mem_k_cache, mem_v_cache, nb=DEC_BATCH, tq=DEC_SEQ, row0=N_PROMPT,
                          mem_blk0=l * DEC_BATCH, y_prev=y_mem)

        x = _merge_out(x, y_ssm, y_mla, y_mem, p, w_ssm_o_b, w_mla_o_b, w_mem_o_b, w_out_b,
                       norm_post[l][None], l, split=l == DEPTH - 1)

    x_p, x_s = x
    ckv_p, ckv_s, kro_p, kro_s = stacked
    mem_shape = (DEPTH, BATCH, N_MEM, MEM_HEADS, MEM_HEAD_DIM)
    return (x_p.reshape(BATCH, SEQ, D_MODEL), x_s.reshape(DEC_BATCH, DEC_SEQ, D_MODEL),
            jnp.stack(p_re), jnp.stack(p_im),
            ckv_p.reshape(DEPTH, BATCH, SEQ, KV_LORA), kro_p.reshape(DEPTH, BATCH, SEQ, QK_ROPE),
            mk_all.reshape(mem_shape), mv_all.reshape(mem_shape),
            jnp.stack(s_re), jnp.stack(s_im),
            ckv_s.reshape(DEPTH, DEC_BATCH, DEC_SEQ, KV_LORA), kro_s.reshape(DEPTH, DEC_BATCH, DEC_SEQ, QK_ROPE))
```

```python
import functools
import math

import jax
import jax.numpy as jnp
import numpy as np
from jax import lax
from jax.experimental import pallas as pl
from jax.experimental.pallas import tpu as pltpu

D_MODEL = 2048
BATCH = 4
SEQ = 2048
DEPTH = 4
DEC_BATCH = 8
DEC_SEQ = 32
PAST_LEN = 2048
CHUNK = 64
N_MEM = 256
EPS = 1e-6
NEG_INF = -1e30
SSM_WIDTH = 1024
SSM_GROUP = 16
SSM_GROUPS = 64
SSM_STATE = 64
MLA_HEADS = 8
QK_NOPE = 128
QK_ROPE = 64
V_HEAD = 128
Q_LORA = 512
KV_LORA = 512
MLA_WIDTH = 1024
ROPE_THETA = 10000.0
MEM_HEADS = 4
MEM_HEAD_DIM = 128
MEM_WIDTH = 512

N_PROMPT = BATCH * SEQ
N_SAMPLE = DEC_BATCH * DEC_SEQ
N_TOK = N_PROMPT + N_SAMPLE

P_GATE = 0
P_U = 6144
P_ZS = 7168
P_CQ = 8192
P_CKV = 8704
P_ZA = 9216
P_QM = 10240
P_ZM = 10752
P_COLS = 11264
W_IN_COLS = 11328

MLA_SCALE = (QK_NOPE + QK_ROPE) ** -0.5
MEM_SCALE = MEM_HEAD_DIM ** -0.5

V7X_VMEM_LIMIT = 52 * 1024 * 1024

BF16 = jnp.bfloat16
F32 = jnp.float32


def _params(sem, vmem=V7X_VMEM_LIMIT):
    return pltpu.CompilerParams(dimension_semantics=sem, vmem_limit_bytes=vmem)


def _dot(a, b):
    return jnp.dot(a, b, preferred_element_type=F32)


def _dot_nt(a, b):
    return lax.dot_general(a, b, (((1,), (1,)), ((), ())), preferred_element_type=F32)


def _rms(x, g):
    return x * lax.rsqrt(jnp.mean(x * x, axis=-1, keepdims=True) + EPS) * g


def _silu(x):
    return x * jax.nn.sigmoid(x)


def _gelu_tanh(x):
    return 0.5 * x * (1.0 + jnp.tanh(math.sqrt(2.0 / math.pi) * (x + 0.044715 * (x * x * x))))


IN_TM = 1408
IN_TN = 512


RL_TN = 1024
W_IN_K_ROPE = 2 * SSM_WIDTH + Q_LORA + KV_LORA
_W_GATE0 = W_IN_K_ROPE + QK_ROPE + MLA_WIDTH + 2 * MEM_WIDTH


_W_UNIT = 64


def _w_src_row(l, j):
    u = lambda c: c // _W_UNIT
    assert all(c % _W_UNIT == 0 for c in (_W_GATE0, RL_TN, W_IN_K_ROPE + QK_ROPE, W_IN_COLS))
    start = jnp.where(j < 6, u(_W_GATE0) + u(RL_TN) * j,
                      jnp.where(j < 9, u(RL_TN) * (j - 6), u(W_IN_K_ROPE + QK_ROPE) + u(RL_TN) * (j - 9)))
    return (l * u(W_IN_COLS) + start) * _W_UNIT


def _w_relayout_kernel(a_ref, o_ref):
    o_ref[...] = a_ref[...].T.astype(BF16)


def _w_relayout(w_in_t):
    return pl.pallas_call(
        _w_relayout_kernel,
        grid=(DEPTH, P_COLS // RL_TN),
        in_specs=[pl.BlockSpec((pl.Element(RL_TN), pl.Element(D_MODEL)),
                               lambda l, j: (_w_src_row(l, j), 0))],
        out_specs=pl.BlockSpec((D_MODEL, RL_TN), lambda l, j: (l, j)),
        out_shape=jax.ShapeDtypeStruct((DEPTH * D_MODEL, P_COLS), BF16),
        compiler_params=_params(("parallel", "parallel")),
        name="w_in_relayout",
    )(w_in_t)


def _in_proj_kernel(x_ref, g_ref, w_ref, wkr_ref, p_ref, kr_ref, h_ref):
    @pl.when(pl.program_id(1) == 0)
    def _():
        h = _rms(x_ref[...], g_ref[...]).astype(BF16)
        h_ref[...] = h
        kr_ref[...] = _dot(h, wkr_ref[...])

    p_ref[...] = _dot(h_ref[...], w_ref[...])


def _in_proj(x, g_pre, w, wkr, l):
    return pl.pallas_call(
        _in_proj_kernel,
        grid=(N_TOK // IN_TM, P_COLS // IN_TN),
        in_specs=[
            pl.BlockSpec((IN_TM, D_MODEL), lambda i, j: (i, 0)),
            pl.BlockSpec((1, D_MODEL), lambda i, j: (0, 0)),
            pl.BlockSpec((D_MODEL, IN_TN), lambda i, j: (l, j)),
            pl.BlockSpec((D_MODEL, 128), lambda i, j: (l, 0)),
        ],
        out_specs=[
            pl.BlockSpec((IN_TM, IN_TN), lambda i, j: (i, j)),
            pl.BlockSpec((IN_TM, 128), lambda i, j: (i, 0)),
        ],
        out_shape=[
            jax.ShapeDtypeStruct((N_TOK, P_COLS), F32),
            jax.ShapeDtypeStruct((N_TOK, 128), F32),
        ],
        scratch_shapes=[pltpu.VMEM((IN_TM, D_MODEL), BF16)],
        compiler_params=_params(("parallel", "arbitrary")),
        name="in_proj",
    )(x, g_pre, w, wkr)


PREP_TM = 256
PREP_PROMPT_TILES = N_PROMPT // PREP_TM


def _mla_prep_kernel(*refs, n_alias):
    (cq_ref, ckv_ref, kr_ref, cs_ref, qg_ref, kvg_ref, wuq_ref, wukv_ref,
     q_ref, k_ref, v_ref, krp_ref, ckvp_ref, ckvs_ref, krop_ref, kros_ref) = refs[n_alias:]
    i = pl.program_id(0)
    d = 128
    cs = cs_ref[...]
    hq = _rms(cq_ref[...], qg_ref[...]).astype(BF16)
    q = _dot(hq, wuq_ref[...])

    ckv = _rms(ckv_ref[...], kvg_ref[...])
    kv = _dot(ckv.astype(BF16), wukv_ref[...])
    v_ref[...] = kv[:, MLA_WIDTH:].astype(BF16)

    tk = kr_ref[...] * cs
    rk = tk + pltpu.roll(tk, 64, axis=1)
    lane = lax.broadcasted_iota(jnp.int32, rk.shape, 1)
    krp = jnp.where(lane < QK_ROPE, rk, 0.0).astype(BF16)
    krp_ref[...] = krp

    for h in range(MLA_HEADS):
        t = q[:, MLA_WIDTH + d * h:MLA_WIDTH + d * (h + 1)] * cs
        q_ref[:, 2 * d * h:2 * d * h + d] = (q[:, d * h:d * (h + 1)] * MLA_SCALE).astype(BF16)
        q_ref[:, 2 * d * h + d:2 * d * (h + 1)] = ((t + pltpu.roll(t, 64, axis=1)) * MLA_SCALE).astype(BF16)
        k_ref[:, 2 * d * h:2 * d * h + d] = kv[:, d * h:d * (h + 1)].astype(BF16)
        k_ref[:, 2 * d * h + d:2 * d * (h + 1)] = krp

    @pl.when(i < PREP_PROMPT_TILES)
    def _():
        ckvp_ref[...] = ckv
        krop_ref[...] = rk[:, :QK_ROPE]

    @pl.when(i == PREP_PROMPT_TILES)
    def _():
        ckvs_ref[...] = ckv
        kros_ref[...] = rk[:, :QK_ROPE]


def _mla_prep(p, kr, cs, qg, kvg, wuq, wukv, l, stacked):
    tm = PREP_TM
    row = lambda c: (lambda i: (i, c))
    const = lambda i: (0, 0)
    layer = lambda i: (l, 0)
    prompt_blk = lambda i: (l * PREP_PROMPT_TILES + jnp.minimum(i, PREP_PROMPT_TILES - 1), 0)
    n_alias = 0 if stacked is None else 4
    in_specs = [
        pl.BlockSpec((tm, Q_LORA), row(P_CQ // Q_LORA)),
        pl.BlockSpec((tm, KV_LORA), row(P_CKV // KV_LORA)),
        pl.BlockSpec((tm, 128), row(0)),
        pl.BlockSpec((tm, 128), row(0)),
        pl.BlockSpec((1, Q_LORA), const),
        pl.BlockSpec((1, KV_LORA), const),
        pl.BlockSpec((Q_LORA, 2048), layer),
        pl.BlockSpec((KV_LORA, 2048), layer),
    ]
    args = [p, p, kr, cs, qg, kvg, wuq, wukv]
    if stacked is not None:
        in_specs = [pl.BlockSpec(memory_space=pl.ANY)] * 4 + in_specs
        args = list(stacked) + args
    return pl.pallas_call(
        functools.partial(_mla_prep_kernel, n_alias=n_alias),
        grid=(N_TOK // tm,),
        in_specs=in_specs,
        out_specs=[
            pl.BlockSpec((tm, 2 * MLA_WIDTH), row(0)),
            pl.BlockSpec((tm, 2 * MLA_WIDTH), row(0)),
            pl.BlockSpec((tm, MLA_WIDTH), row(0)),
            pl.BlockSpec((tm, 128), row(0)),
            pl.BlockSpec((tm, KV_LORA), prompt_blk),
            pl.BlockSpec((tm, KV_LORA), layer),
            pl.BlockSpec((tm, QK_ROPE), prompt_blk),
            pl.BlockSpec((tm, QK_ROPE), layer),
        ],
        out_shape=[
            jax.ShapeDtypeStruct((N_TOK, 2 * MLA_WIDTH), BF16),
            jax.ShapeDtypeStruct((N_TOK, 2 * MLA_WIDTH), BF16),
            jax.ShapeDtypeStruct((N_TOK, MLA_WIDTH), BF16),
            jax.ShapeDtypeStruct((N_TOK, 128), BF16),
            jax.ShapeDtypeStruct((DEPTH * N_PROMPT, KV_LORA), F32),
            jax.ShapeDtypeStruct((DEPTH * N_SAMPLE, KV_LORA), F32),
            jax.ShapeDtypeStruct((DEPTH * N_PROMPT, QK_ROPE), F32),
            jax.ShapeDtypeStruct((DEPTH * N_SAMPLE, QK_ROPE), F32),
        ],
        input_output_aliases={k: 4 + k for k in range(n_alias)},
        compiler_params=_params(("arbitrary",)),
        name="mla_prep",
    )(*args)


def _mm_kernel(x_ref, w_ref, *o_refs):
    y = _dot(x_ref[...].astype(BF16), w_ref[...])
    off = 0
    for o_ref in o_refs:
        n = o_ref.shape[1]
        o_ref[...] = y[:, off:off + n].astype(o_ref.dtype)
        off += n


def _mem_kv(mem_rows, w_mem_kv):
    tm = 512
    nr = BATCH * N_MEM // tm
    out = pl.BlockSpec((tm, MEM_WIDTH), lambda l, i: (l * nr + i, 0))
    return pl.pallas_call(
        _mm_kernel,
        grid=(DEPTH, nr),
        in_specs=[pl.BlockSpec((tm, D_MODEL), lambda l, i: (i, 0)),
                  pl.BlockSpec((D_MODEL, 2 * MEM_WIDTH), lambda l, i: (l, 0))],
        out_specs=[out, out],
        out_shape=[jax.ShapeDtypeStruct((DEPTH * BATCH * N_MEM, MEM_WIDTH), F32)] * 2,
        compiler_params=_params(("parallel", "parallel")),
        name="mem_kv",
    )(mem_rows, w_mem_kv)


ATT_TQ = 512


def _softmax_pv(parts, z):
    m = functools.reduce(jnp.maximum, [jnp.max(s, axis=1, keepdims=True) for s, _ in parts])
    l = 0.0
    o = 0.0
    for s, v in parts:
        e = jnp.exp(s - m)
        l = l + jnp.sum(e, axis=1, keepdims=True)
        o = o + _dot(e.astype(BF16), v)
    return o * (1.0 / l) * _silu(z)


def _chunk_mask(q_pos0, k_pos0, nq, nk):
    qp = q_pos0 + lax.broadcasted_iota(jnp.int32, (nq, nk), 0)
    kp = k_pos0 + lax.broadcasted_iota(jnp.int32, (nq, nk), 1)
    sh = CHUNK.bit_length() - 1
    return lax.shift_right_logical(kp, sh) <= lax.shift_right_logical(qp, sh)


def _attn_prompt_kernel(q_ref, k_ref, v_ref, z_ref, o_ref):
    tq = ATT_TQ
    for qi in range(SEQ // tq):
        lo, hi = qi * tq, (qi + 1) * tq
        q = q_ref[lo:hi, :]
        sd = jnp.where(_chunk_mask(lo, lo, tq, tq), _dot_nt(q, k_ref[lo:hi, :]), NEG_INF)
        parts = [(sd, v_ref[lo:hi, :])]
        if qi > 0:
            parts.append((_dot_nt(q, k_ref[0:lo, :]), v_ref[0:lo, :]))
        o_ref[lo:hi, :] = _softmax_pv(parts, z_ref[lo:hi, :]).astype(BF16)


def _attn_prompt(q, k, v, p):
    qk = pl.BlockSpec((SEQ, 256), lambda b, h: (b, h))
    blk = lambda c0: pl.BlockSpec((SEQ, 128), lambda b, h: (b, c0 + h))
    return pl.pallas_call(
        _attn_prompt_kernel,
        grid=(BATCH, MLA_HEADS),
        in_specs=[qk, qk, blk(0), blk(P_ZA // 128)],
        out_specs=blk(0),
        out_shape=jax.ShapeDtypeStruct((N_PROMPT, MLA_WIDTH), BF16),
        compiler_params=_params(("parallel", "parallel")),
        name="attn_prompt",
    )(q, k, v, p)


def _attn_sample_kernel(q_ref, ckvc_ref, krct_ref, ckvn_ref, krn_ref, wukv_ref, z_ref, o_ref):
    nh, d = MLA_HEADS, 128
    qa = jnp.concatenate(
        [_dot_nt(q_ref[:, 2 * d * h:2 * d * h + d], wukv_ref[:, d * h:d * (h + 1)]) for h in range(nh)],
        axis=0).astype(BF16)
    qr = jnp.concatenate([q_ref[:, 2 * d * h + d:2 * d * (h + 1)] for h in range(nh)], axis=0)
    ckvc = ckvc_ref[...].astype(BF16)
    krct = krct_ref[...].astype(BF16)
    ckvn = ckvn_ref[...].astype(BF16)
    rows = nh * DEC_SEQ

    def mask(k_pos0, nk):
        qp = PAST_LEN + (lax.broadcasted_iota(jnp.int32, (rows, nk), 0) & (DEC_SEQ - 1))
        kp = k_pos0 + lax.broadcasted_iota(jnp.int32, (rows, nk), 1)
        sh = CHUNK.bit_length() - 1
        return lax.shift_right_logical(kp, sh) <= lax.shift_right_logical(qp, sh)

    sp = jnp.where(mask(0, PAST_LEN), _dot_nt(qa, ckvc) + _dot(qr[:, :QK_ROPE], krct), NEG_INF)
    sn = jnp.where(mask(PAST_LEN, DEC_SEQ), _dot_nt(qa, ckvn) + _dot_nt(qr, krn_ref[...]), NEG_INF)
    m = jnp.maximum(jnp.max(sp, axis=1, keepdims=True), jnp.max(sn, axis=1, keepdims=True))
    ep, en = jnp.exp(sp - m), jnp.exp(sn - m)
    l = jnp.sum(ep, axis=1, keepdims=True) + jnp.sum(en, axis=1, keepdims=True)
    lat = ((_dot(ep.astype(BF16), ckvc) + _dot(en.astype(BF16), ckvn)) * (1.0 / l)).astype(BF16)
    o = jnp.concatenate(
        [_dot(lat[DEC_SEQ * h:DEC_SEQ * (h + 1)], wukv_ref[:, MLA_WIDTH + d * h:MLA_WIDTH + d * (h + 1)])
         for h in range(nh)], axis=1)
    o_ref[...] = (o * _silu(z_ref[...])).astype(BF16)


def _attn_sample(q, ckv_cache, krc_t, ckv_s, krp, wukv, p, l):
    assert DEC_SEQ & (DEC_SEQ - 1) == 0
    r0 = N_PROMPT // DEC_SEQ
    new = lambda width, c: pl.BlockSpec((DEC_SEQ, width), lambda b: (r0 + b, c))
    stream = lambda rows, width: pl.BlockSpec((rows, width), lambda b: (l * DEC_BATCH + b, 0))
    return pl.pallas_call(
        _attn_sample_kernel,
        grid=(DEC_BATCH,),
        in_specs=[new(2 * MLA_WIDTH, 0),
                  stream(PAST_LEN, KV_LORA), stream(QK_ROPE, PAST_LEN),
                  stream(DEC_SEQ, KV_LORA),
                  new(128, 0),
                  pl.BlockSpec((KV_LORA, 2048), lambda b: (l, 0)),
                  new(MLA_WIDTH, P_ZA // MLA_WIDTH)],
        out_specs=pl.BlockSpec((DEC_SEQ, MLA_WIDTH), lambda b: (b, 0)),
        out_shape=jax.ShapeDtypeStruct((N_SAMPLE, MLA_WIDTH), BF16),
        compiler_params=_params(("parallel",)),
        name="attn_sample",
    )(q, ckv_cache, krc_t, ckv_s, krp, wukv, p)


def _mem_attn_kernel(q_ref, z_ref, k_ref, v_ref, o_ref):
    q = q_ref[...] * MEM_SCALE
    outs = []
    for h in range(MEM_HEADS):
        sl = slice(MEM_HEAD_DIM * h, MEM_HEAD_DIM * (h + 1))
        s = _dot_nt(q[:, sl].astype(BF16), k_ref[:, sl].astype(BF16))
        m = jnp.max(s, axis=1, keepdims=True)
        e = jnp.exp(s - m)
        l = jnp.sum(e, axis=1, keepdims=True)
        outs.append(_dot(e.astype(BF16), v_ref[:, sl].astype(BF16)) * (1.0 / l))
    o_ref[...] = (jnp.concatenate(outs, axis=1) * _silu(z_ref[...])).astype(BF16)


def _mem_attn(p, mem_k, mem_v, *, nb, seq, tq, row0, mem_blk0, name):
    seq_blocks = seq // tq
    r0 = row0 // tq
    rowmap = lambda c: (lambda b, i: (r0 + b * seq_blocks + i, c))
    mem = pl.BlockSpec((N_MEM, MEM_WIDTH), lambda b, i: (mem_blk0 + b, 0))
    return pl.pallas_call(
        _mem_attn_kernel,
        grid=(nb, seq_blocks),
        in_specs=[pl.BlockSpec((tq, MEM_WIDTH), rowmap(P_QM // MEM_WIDTH)),
                  pl.BlockSpec((tq, MEM_WIDTH), rowmap(P_ZM // MEM_WIDTH)),
                  mem, mem],
        out_specs=pl.BlockSpec((tq, MEM_WIDTH), lambda b, i: (b * seq_blocks + i, 0)),
        out_shape=jax.ShapeDtypeStruct((nb * seq, MEM_WIDTH), BF16),
        compiler_params=_params(("parallel", "parallel")),
        name=name,
    )(p, p, mem_k, mem_v)


def _s5_disc_kernel(are_ref, aim_ref, ldt_ref, bre_ref, bim_ref, abr_ref, abi_ref, bbr_ref, bbi_ref):
    dt = jnp.exp(ldt_ref[...])
    lr, li = are_ref[...], aim_ref[...]
    mag = jnp.exp(lr * dt)
    ab_re, ab_im = mag * jnp.cos(li * dt), mag * jnp.sin(li * dt)
    den = lr * lr + li * li
    nr, ni = ab_re - 1.0, ab_im
    f_re = (nr * lr + ni * li) / den
    f_im = (ni * lr - nr * li) / den
    abr_ref[...] = ab_re
    abi_ref[...] = ab_im
    br, bi = bre_ref[...], bim_ref[...]
    bbr_ref[...] = f_re * br - f_im * bi
    bbi_ref[...] = f_re * bi + f_im * br


def _s5_discretize(a_re, a_im, log_dt, b_re_t, b_im_t):
    G, P, C = SSM_GROUPS, SSM_STATE, SSM_GROUP
    ab_re, ab_im, bb_re_t, bb_im_t = pl.pallas_call(
        _s5_disc_kernel,
        out_shape=[jax.ShapeDtypeStruct((G, 1, P), F32), jax.ShapeDtypeStruct((G, 1, P), F32),
                   jax.ShapeDtypeStruct((G, C, P), F32), jax.ShapeDtypeStruct((G, C, P), F32)],
        name="s5_discretize",
    )(a_re.reshape(G, 1, P), a_im.reshape(G, 1, P), log_dt.reshape(G, 1, 1), b_re_t, b_im_t)
    return ab_re.reshape(G, P), ab_im.reshape(G, P), bb_re_t, bb_im_t


SSM_PAIRS = 4
SSM_HALF_STATE = 512
SSM_SUB = 32


def _ssm_kernel(*refs, nseq, tt):
    u_refs, z_refs = refs[:nseq], refs[nseq:2 * nseq]
    (d_ref, wglu_ref, bmat_ref, cmat_ref, are_ref, aim_ref, h0_ref, pin_ref, pout_ref,
     y_ref, ht_ref, st_ref, hb_ref, g_ref) = refs[2 * nseq:]
    q = 2 * nseq
    r = nseq * SSM_SUB
    hs = SSM_HALF_STATE
    c = pl.program_id(0)

    @pl.when(c == 0)
    def _():
        st_ref[...] = h0_ref[...]

    lane = lax.broadcasted_iota(jnp.int32, (r, SSM_WIDTH), 1)
    low = (lane & 128) == 0
    rows2 = lax.broadcasted_iota(jnp.int32, (q * SSM_SUB, SSM_WIDTH), 0)
    lane2 = lax.broadcasted_iota(jnp.int32, (q * SSM_SUB, SSM_WIDTH), 1)
    keep = ((rows2 & 1) == 0) == ((lane2 & 128) == 0)

    def sub_tile(s, carry):
        t0 = pl.multiple_of(s * SSM_SUB, SSM_SUB)
        u = jnp.concatenate([u_refs[j][pl.ds(t0, SSM_SUB), :] for j in range(nseq)], axis=0)
        ub = u.astype(BF16)
        zero = jnp.zeros_like(ub)
        stacked = jnp.concatenate([jnp.where(low, ub, zero), jnp.where(low, zero, ub)], axis=0)
        lall = _dot(pin_ref[...], stacked).astype(BF16)
        for p in range(SSM_PAIRS):
            hb_ref[p] = _dot(lall[:, 256 * p:256 * (p + 1)], bmat_ref[p])
        for p in range(SSM_PAIRS):
            ar = are_ref[p]
            ai = aim_ref[p]

            def step(t, hc, p=p, ar=ar, ai=ai):
                hr, hi = hc
                r0 = pl.multiple_of(t * q, q)
                bur = hb_ref[p, pl.ds(r0, q), 0:hs]
                bui = hb_ref[p, pl.ds(r0, q), hs:2 * hs]
                nr = ar * hr - ai * hi + bur
                ni = ar * hi + ai * hr + bui
                hb_ref[p, pl.ds(r0, q), 0:hs] = nr
                hb_ref[p, pl.ds(r0, q), hs:2 * hs] = ni
                return nr, ni

            hr, hi = lax.fori_loop(0, SSM_SUB, step, (st_ref[p, :, 0:hs], st_ref[p, :, hs:2 * hs]), unroll=8)
            st_ref[p, :, 0:hs] = hr
            st_ref[p, :, hs:2 * hs] = hi
        y2 = jnp.concatenate([_dot(hb_ref[p].astype(BF16), cmat_ref[p]) for p in range(SSM_PAIRS)], axis=1)
        zf = jnp.where(keep, y2, 0.0)
        z_hi = zf.astype(BF16)
        r1 = zf - z_hi.astype(F32)
        z_mid = r1.astype(BF16)
        z_lo = (r1 - z_mid.astype(F32)).astype(BF16)
        pout = pout_ref[...]
        yn = _dot(pout, z_hi) + _dot(pout, z_mid) + _dot(pout, z_lo)
        g = _gelu_tanh(yn + d_ref[...] * u).astype(BF16)
        for j in range(nseq):
            g_ref[pl.ds(pl.multiple_of(j * tt + t0, SSM_SUB), SSM_SUB), :] = g[j * SSM_SUB:(j + 1) * SSM_SUB]
        return carry

    lax.fori_loop(0, tt // SSM_SUB, sub_tile, 0)

    g = g_ref[...]
    ga = _dot(g, wglu_ref[:, :SSM_WIDTH])
    gb = _dot(g, wglu_ref[:, SSM_WIDTH:])
    z = jnp.concatenate([z_refs[j][...] for j in range(nseq)], axis=0)
    out = (ga * jax.nn.sigmoid(gb) * _silu(z)).astype(BF16)
    for j in range(nseq):
        y_ref[j] = out[j * tt:(j + 1) * tt]

    @pl.when(c == pl.num_programs(0) - 1)
    def _():
        ht_ref[...] = st_ref[...]


def _ssm_perms(nseq):
    r, q = nseq * SSM_SUB, 2 * nseq
    pin = np.zeros((2 * r, 2 * r), np.float32)
    pout = np.zeros((r, 2 * r), np.float32)
    for t in range(SSM_SUB):
        for j in range(nseq):
            for h in range(2):
                pin[t * q + 2 * j + h, h * r + j * SSM_SUB + t] = 1.0
                pout[j * SSM_SUB + t, t * q + 2 * j + h] = 1.0
    return jnp.asarray(pin, BF16), jnp.asarray(pout, BF16)


def _ssm(p, d, w_glu, bmat, cmat, a_re, a_im, h0, *, nseq, s, tt, row0, l, name):
    nt = s // tt
    q = 2 * nseq
    rb0 = row0 // tt
    pin, pout = _ssm_perms(nseq)
    seq_spec = lambda j, col: pl.BlockSpec((tt, SSM_WIDTH), lambda c: (rb0 + j * nt + c, col))
    whole = lambda a: pl.BlockSpec(a.shape, lambda c: (0,) * a.ndim)
    in_specs = ([seq_spec(j, P_U // SSM_WIDTH) for j in range(nseq)]
                + [seq_spec(j, P_ZS // SSM_WIDTH) for j in range(nseq)]
                + [pl.BlockSpec((1, SSM_WIDTH), lambda c: (0, 0)),
                   pl.BlockSpec((SSM_WIDTH, 2 * SSM_WIDTH), lambda c: (l, 0)),
                   whole(bmat), whole(cmat), whole(a_re), whole(a_im), whole(h0), whole(pin), whole(pout)])
    return pl.pallas_call(
        functools.partial(_ssm_kernel, nseq=nseq, tt=tt),
        grid=(nt,),
        in_specs=in_specs,
        out_specs=[pl.BlockSpec((nseq, tt, SSM_WIDTH), lambda c: (0, c, 0)),
                   pl.BlockSpec((SSM_PAIRS, q, 2 * SSM_HALF_STATE), lambda c: (0, 0, 0))],
        out_shape=[jax.ShapeDtypeStruct((nseq, s, SSM_WIDTH), BF16),
                   jax.ShapeDtypeStruct((SSM_PAIRS, q, 2 * SSM_HALF_STATE), F32)],
        scratch_shapes=[pltpu.VMEM((SSM_PAIRS, q, 2 * SSM_HALF_STATE), F32),
                        pltpu.VMEM((SSM_PAIRS, q * SSM_SUB, 2 * SSM_HALF_STATE), F32),
                        pltpu.VMEM((nseq * tt, SSM_WIDTH), BF16)],
        compiler_params=_params(("arbitrary",)),
        name=name,
    )(*([p] * (2 * nseq)), d, w_glu, bmat, cmat, a_re, a_im, h0, pin, pout)


def _ssm_state_in(h_re, h_im, nseq):
    def arr(h):
        return h.reshape(nseq, SSM_PAIRS, 2, SSM_HALF_STATE).transpose(1, 0, 2, 3).reshape(
            SSM_PAIRS, nseq * 2, SSM_HALF_STATE)
    return jnp.concatenate([arr(h_re), arr(h_im)], axis=-1)


def _ssm_state_out(ht, nseq):
    def arr(h):
        return h.reshape(SSM_PAIRS, nseq, 2, 8, SSM_STATE).transpose(1, 0, 2, 3, 4).reshape(
            nseq, SSM_GROUPS, SSM_STATE)
    return arr(ht[..., :SSM_HALF_STATE]), arr(ht[..., SSM_HALF_STATE:])


def _ssm_mats(bb_re_t, bb_im_t, c_re, c_im):
    eye8 = jnp.eye(8, dtype=F32)

    def bmat(bb):
        b6 = bb.reshape(SSM_PAIRS, 2, 8, SSM_GROUP, SSM_STATE)
        return jnp.einsum('ahgcp,gk->ahgckp', b6, eye8).reshape(SSM_PAIRS, 256, SSM_HALF_STATE)

    def cmat(cc):
        c6 = cc.reshape(SSM_PAIRS, 2, 8, SSM_GROUP, SSM_STATE)
        return jnp.einsum('ahgcp,gk->akphgc', c6, eye8).reshape(SSM_PAIRS, SSM_HALF_STATE, 256)

    b_all = jnp.concatenate([bmat(bb_re_t), bmat(bb_im_t)], axis=2).astype(BF16)
    c_all = jnp.concatenate([cmat(c_re), cmat(-c_im)], axis=1).astype(BF16)
    return b_all, c_all


def _ssm_a_rows(a, nseq):
    a3 = a.reshape(SSM_PAIRS, 1, 2, SSM_HALF_STATE)
    return jnp.broadcast_to(a3, (SSM_PAIRS, nseq, 2, SSM_HALF_STATE)).reshape(
        SSM_PAIRS, nseq * 2, SSM_HALF_STATE)


OUT_TM = 256


OUT_PROMPT_TILES = N_PROMPT // OUT_TM


def _merge_out_kernel(x_ref, ysp_ref, yss_ref, yap_ref, yas_ref, ymp_ref, yms_ref, g0_ref, g1_ref, g2_ref,
                      ws_ref, wa_ref, wm_ref, wo_ref, gp_ref, *o_refs):
    i = pl.program_id(0)
    pick = lambda prompt_ref, sample_ref: jnp.where(i < OUT_PROMPT_TILES, prompt_ref[...], sample_ref[...])
    merged = (jax.nn.sigmoid(g0_ref[...]) * _dot(pick(ysp_ref, yss_ref), ws_ref[...])
              + jax.nn.sigmoid(g1_ref[...]) * _dot(pick(yap_ref, yas_ref), wa_ref[...])
              + jax.nn.sigmoid(g2_ref[...]) * _dot(pick(ymp_ref, yms_ref), wm_ref[...]))
    out = _dot(merged.astype(BF16), wo_ref[...])
    y = x_ref[...] + _rms(out, gp_ref[...])
    if len(o_refs) == 1:
        o_refs[0][...] = y
    else:
        @pl.when(i < OUT_PROMPT_TILES)
        def _():
            o_refs[0][...] = y

        @pl.when(i == OUT_PROMPT_TILES)
        def _():
            o_refs[1][...] = y


def _merge_out(x, y_ssm, y_mla, y_mem, p, w_ssm_o, w_mla_o, w_mem_o, w_out, g_post, l, split):
    tm = OUT_TM
    assert N_SAMPLE == tm
    row = lambda c: (lambda i: (i, c))
    const = lambda i: (0, 0)
    prompt_row = lambda i: (jnp.minimum(i, OUT_PROMPT_TILES - 1), 0)
    resident = lambda shape: pl.BlockSpec(shape, lambda i: (l, 0), pipeline_mode=pl.Buffered(1))
    branch = lambda width: [pl.BlockSpec((tm, width), prompt_row), pl.BlockSpec((tm, width), const)]
    if split:
        out_specs = [pl.BlockSpec((tm, D_MODEL), prompt_row),
                     pl.BlockSpec((tm, D_MODEL), const)]
        out_shape = [jax.ShapeDtypeStruct((N_PROMPT, D_MODEL), F32),
                     jax.ShapeDtypeStruct((N_SAMPLE, D_MODEL), F32)]
    else:
        out_specs = pl.BlockSpec((tm, D_MODEL), row(0))
        out_shape = jax.ShapeDtypeStruct((N_TOK, D_MODEL), F32)
    return pl.pallas_call(
        _merge_out_kernel,
        grid=(N_TOK // tm,),
        in_specs=[pl.BlockSpec((tm, D_MODEL), row(0)),
                  *branch(SSM_WIDTH), *branch(MLA_WIDTH), *branch(MEM_WIDTH),
                  pl.BlockSpec((tm, D_MODEL), row(0)),
                  pl.BlockSpec((tm, D_MODEL), row(1)),
                  pl.BlockSpec((tm, D_MODEL), row(2)),
                  resident((SSM_WIDTH, D_MODEL)),
                  resident((MLA_WIDTH, D_MODEL)),
                  resident((MEM_WIDTH, D_MODEL)),
                  resident((D_MODEL, D_MODEL)),
                  pl.BlockSpec((1, D_MODEL), const)],
        out_specs=out_specs,
        out_shape=out_shape,
        compiler_params=_params(("arbitrary",)),
        name="merge_out",
    )(x, *y_ssm, *y_mla, *y_mem, p, p, p, w_ssm_o, w_mla_o, w_mem_o, w_out, g_post)


def _rot_half_cols(w):
    half = w.shape[-1] // 2
    return jnp.concatenate([w[..., half:], w[..., :half]], axis=-1)


def _stack_rows(w):
    return w.astype(BF16).reshape(w.shape[0] * w.shape[1], w.shape[2])


def _prep_small_weights(w_in_t, w_uq, w_uk, w_uv):
    wk = jnp.swapaxes(w_in_t[:, W_IN_K_ROPE:W_IN_K_ROPE + QK_ROPE, :], 1, 2)
    w_kr = _stack_rows(jnp.concatenate([wk, _rot_half_cols(wk)], axis=-1))
    uq = w_uq.reshape(DEPTH, Q_LORA, MLA_HEADS, QK_NOPE + QK_ROPE)
    uq_nope = uq[..., :QK_NOPE].reshape(DEPTH, Q_LORA, MLA_WIDTH)
    uq_rope = uq[..., QK_NOPE:]
    uq_rr = jnp.concatenate([uq_rope, _rot_half_cols(uq_rope)], axis=-1).reshape(DEPTH, Q_LORA, MLA_WIDTH)
    w_uq2 = jnp.concatenate([uq_nope, uq_rr], axis=-1)
    w_ukv = jnp.concatenate([w_uk, w_uv], axis=-1)
    return w_kr, _stack_rows(w_uq2), _stack_rows(w_ukv)


def _rope_table():
    half = QK_ROPE // 2
    pos = jnp.concatenate([jnp.tile(jnp.arange(SEQ, dtype=jnp.int32), BATCH),
                           jnp.tile(PAST_LEN + jnp.arange(DEC_SEQ, dtype=jnp.int32), DEC_BATCH)])
    inv = ROPE_THETA ** (-jnp.arange(half, dtype=F32) / half)
    ang = pos.astype(F32)[:, None] * inv[None, :]
    cos, sin = jnp.cos(ang), jnp.sin(ang)
    return jnp.concatenate([cos, cos, -sin, sin], axis=-1)


def kernel(x_prompt, x_sample, cache_mla_ckv, cache_mla_krope, cache_mem_k, cache_mem_v, state_ssm_re, state_ssm_im, mem_prompt, norm_pre, w_in, ssm_a_re, ssm_a_im, ssm_log_dt, ssm_b_re, ssm_b_im, ssm_c_re, ssm_c_im, ssm_d, w_glu, mla_q_norm, w_uq, mla_kv_norm, w_uk, w_uv, w_mem_k, w_mem_v, w_ssm_o, w_mla_o, w_mem_o, w_out, norm_post):
    w_in_t = jnp.swapaxes(w_in, 1, 2)
    w_main = _w_relayout(w_in_t.reshape(DEPTH * W_IN_COLS, D_MODEL))
    w_kr, w_uq2, w_ukv = _prep_small_weights(w_in_t, w_uq, w_uk, w_uv)
    w_glu_b, w_ssm_o_b, w_mla_o_b, w_mem_o_b, w_out_b = (
        _stack_rows(w) for w in (w_glu, w_ssm_o, w_mla_o, w_mem_o, w_out))
    w_mem_kv = _stack_rows(jnp.concatenate([w_mem_k, w_mem_v], axis=-1))
    cs = _rope_table()
    mem_rows = mem_prompt.reshape(BATCH * N_MEM, D_MODEL)
    ckv_cache = cache_mla_ckv.reshape(DEPTH * DEC_BATCH * PAST_LEN, KV_LORA)
    krc_t = jnp.swapaxes(cache_mla_krope, 2, 3).reshape(DEPTH * DEC_BATCH * QK_ROPE, PAST_LEN)
    mem_k_cache = cache_mem_k.reshape(DEPTH * DEC_BATCH * N_MEM, MEM_WIDTH)
    mem_v_cache = cache_mem_v.reshape(DEPTH * DEC_BATCH * N_MEM, MEM_WIDTH)

    zeros_state = jnp.zeros((BATCH, SSM_GROUPS, SSM_STATE), F32)
    x = jnp.concatenate([x_prompt.reshape(N_PROMPT, D_MODEL), x_sample.reshape(N_SAMPLE, D_MODEL)], axis=0)

    mk_all, mv_all = _mem_kv(mem_rows, w_mem_kv)

    p_re, p_im, s_re, s_im = [], [], [], []
    stacked = None
    for l in range(DEPTH):
        p, kr = _in_proj(x, norm_pre[l][None], w_main, w_kr, l)

        ab_re, ab_im, bb_re_t, bb_im_t = _s5_discretize(
            ssm_a_re[l], ssm_a_im[l], ssm_log_dt[l],
            ssm_b_re[l].transpose(0, 2, 1), ssm_b_im[l].transpose(0, 2, 1))
        bmat, cmat = _ssm_mats(bb_re_t, bb_im_t, ssm_c_re[l], ssm_c_im[l])
        d = ssm_d[l][None]
        ys_p, ht_p = _ssm(p, d, w_glu_b, bmat, cmat, _ssm_a_rows(ab_re, BATCH), _ssm_a_rows(ab_im, BATCH),
                          _ssm_state_in(zeros_state, zeros_state, BATCH),
                          nseq=BATCH, s=SEQ, tt=128, row0=0, l=l, name="ssm_prompt")
        ys_s, ht_s = _ssm(p, d, w_glu_b, bmat, cmat, _ssm_a_rows(ab_re, DEC_BATCH), _ssm_a_rows(ab_im, DEC_BATCH),
                          _ssm_state_in(state_ssm_re[l], state_ssm_im[l], DEC_BATCH),
                          nseq=DEC_BATCH, s=DEC_SEQ, tt=DEC_SEQ, row0=N_PROMPT, l=l, name="ssm_sample")
        y_ssm = (ys_p.reshape(N_PROMPT, SSM_WIDTH), ys_s.reshape(N_SAMPLE, SSM_WIDTH))
        hr, hi = _ssm_state_out(ht_p, BATCH)
        p_re.append(hr); p_im.append(hi)
        hr, hi = _ssm_state_out(ht_s, DEC_BATCH)
        s_re.append(hr); s_im.append(hi)

        q, k, v, krp, *stacked = _mla_prep(p, kr, cs, mla_q_norm[l][None], mla_kv_norm[l][None],
                                           w_uq2, w_ukv, l, stacked)
        y_mla = (_attn_prompt(q, k, v, p),
                 _attn_sample(q, ckv_cache, krc_t, stacked[1], krp, w_ukv, p, l))

        y_mem = (_mem_attn(p, mk_all, mv_all, nb=BATCH, seq=SEQ, tq=512, row0=0, mem_blk0=l * BATCH,
                           name="mem_attn_prompt"),
                 _mem_attn(p, mem_k_cache, mem_v_cache, nb=DEC_BATCH, seq=DEC_SEQ, tq=DEC_SEQ, row0=N_PROMPT,
                           mem_blk0=l * DEC_BATCH, name="mem_attn_sample"))

        x = _merge_out(x, y_ssm, y_mla, y_mem, p, w_ssm_o_b, w_mla_o_b, w_mem_o_b, w_out_b,
                       norm_post[l][None], l, split=l == DEPTH - 1)

    x_p, x_s = x
    ckv_p, ckv_s, kro_p, kro_s = stacked
    mem_shape = (DEPTH, BATCH, N_MEM, MEM_HEADS, MEM_HEAD_DIM)
    return (x_p.reshape(BATCH, SEQ, D_MODEL), x_s.reshape(DEC_BATCH, DEC_SEQ, D_MODEL),
            jnp.stack(p_re), jnp.stack(p_im),
            ckv_p.reshape(DEPTH, BATCH, SEQ, KV_LORA), kro_p.reshape(DEPTH, BATCH, SEQ, QK_ROPE),
            mk_all.reshape(mem_shape), mv_all.reshape(mem_shape),
            jnp.stack(s_re), jnp.stack(s_im),
            ckv_s.reshape(DEPTH, DEC_BATCH, DEC_SEQ, KV_LORA), kro_s.reshape(DEPTH, DEC_BATCH, DEC_SEQ, QK_ROPE))
```

```python
import functools
import math

import jax
import jax.numpy as jnp
import numpy as np
from jax import lax
from jax.experimental import pallas as pl
from jax.experimental.pallas import tpu as pltpu

D_MODEL = 2048
BATCH = 4
SEQ = 2048
DEPTH = 4
DEC_BATCH = 8
DEC_SEQ = 32
PAST_LEN = 2048
CHUNK = 64
N_MEM = 256
EPS = 1e-6
NEG_INF = -1e30
SSM_WIDTH = 1024
SSM_GROUP = 16
SSM_GROUPS = 64
SSM_STATE = 64
MLA_HEADS = 8
QK_NOPE = 128
QK_ROPE = 64
V_HEAD = 128
Q_LORA = 512
KV_LORA = 512
MLA_WIDTH = 1024
ROPE_THETA = 10000.0
MEM_HEADS = 4
MEM_HEAD_DIM = 128
MEM_WIDTH = 512

N_PROMPT = BATCH * SEQ
N_SAMPLE = DEC_BATCH * DEC_SEQ
N_TOK = N_PROMPT + N_SAMPLE

P_GATE = 0
P_U = 6144
P_ZS = 7168
P_CQ = 8192
P_CKV = 8704
P_ZA = 9216
P_QM = 10240
P_ZM = 10752
P_COLS = 11264
W_IN_COLS = 11328

MLA_SCALE = (QK_NOPE + QK_ROPE) ** -0.5
MEM_SCALE = MEM_HEAD_DIM ** -0.5

V7X_VMEM_LIMIT = 52 * 1024 * 1024

BF16 = jnp.bfloat16
F32 = jnp.float32


def _params(sem, vmem=V7X_VMEM_LIMIT):
    return pltpu.CompilerParams(dimension_semantics=sem, vmem_limit_bytes=vmem)


def _dot(a, b):
    return jnp.dot(a, b, preferred_element_type=F32)


def _dot_nt(a, b):
    return lax.dot_general(a, b, (((1,), (1,)), ((), ())), preferred_element_type=F32)


def _rms(x, g):
    return x * lax.rsqrt(jnp.mean(x * x, axis=-1, keepdims=True) + EPS) * g


def _silu(x):
    return x * jax.nn.sigmoid(x)


def _gelu_tanh(x):
    return 0.5 * x * (1.0 + jnp.tanh(math.sqrt(2.0 / math.pi) * (x + 0.044715 * (x * x * x))))


IN_TM = 1408
IN_TN = 512


RL_TN = 1024
W_IN_K_ROPE = 2 * SSM_WIDTH + Q_LORA + KV_LORA
_W_GATE0 = W_IN_K_ROPE + QK_ROPE + MLA_WIDTH + 2 * MEM_WIDTH


_W_UNIT = 64


def _w_src_row(l, j):
    u = lambda c: c // _W_UNIT
    assert all(c % _W_UNIT == 0 for c in (_W_GATE0, RL_TN, W_IN_K_ROPE + QK_ROPE, W_IN_COLS))
    start = jnp.where(j < 6, u(_W_GATE0) + u(RL_TN) * j,
                      jnp.where(j < 9, u(RL_TN) * (j - 6), u(W_IN_K_ROPE + QK_ROPE) + u(RL_TN) * (j - 9)))
    return (l * u(W_IN_COLS) + start) * _W_UNIT


def _w_relayout_kernel(a_ref, o_ref):
    o_ref[...] = a_ref[...].T.astype(BF16)


def _w_relayout(w_in_t):
    return pl.pallas_call(
        _w_relayout_kernel,
        grid=(DEPTH, P_COLS // RL_TN),
        in_specs=[pl.BlockSpec((pl.Element(RL_TN), pl.Element(D_MODEL)),
                               lambda l, j: (_w_src_row(l, j), 0))],
        out_specs=pl.BlockSpec((D_MODEL, RL_TN), lambda l, j: (l, j)),
        out_shape=jax.ShapeDtypeStruct((DEPTH * D_MODEL, P_COLS), BF16),
        compiler_params=_params(("parallel", "parallel")),
        name="w_in_relayout",
    )(w_in_t)


def _in_proj_kernel(x_ref, g_ref, w_ref, wkr_ref, p_ref, kr_ref, h_ref):
    @pl.when(pl.program_id(1) == 0)
    def _():
        h = _rms(x_ref[...], g_ref[...]).astype(BF16)
        h_ref[...] = h
        kr_ref[...] = _dot(h, wkr_ref[...])

    p_ref[...] = _dot(h_ref[...], w_ref[...])


def _in_proj(x, g_pre, w, wkr, l):
    return pl.pallas_call(
        _in_proj_kernel,
        grid=(N_TOK // IN_TM, P_COLS // IN_TN),
        in_specs=[
            pl.BlockSpec((IN_TM, D_MODEL), lambda i, j: (i, 0)),
            pl.BlockSpec((1, D_MODEL), lambda i, j: (0, 0)),
            pl.BlockSpec((D_MODEL, IN_TN), lambda i, j: (l, j)),
            pl.BlockSpec((D_MODEL, 128), lambda i, j: (l, 0)),
        ],
        out_specs=[
            pl.BlockSpec((IN_TM, IN_TN), lambda i, j: (i, j)),
            pl.BlockSpec((IN_TM, 128), lambda i, j: (i, 0)),
        ],
        out_shape=[
            jax.ShapeDtypeStruct((N_TOK, P_COLS), F32),
            jax.ShapeDtypeStruct((N_TOK, 128), F32),
        ],
        scratch_shapes=[pltpu.VMEM((IN_TM, D_MODEL), BF16)],
        compiler_params=_params(("parallel", "arbitrary")),
        name="in_proj",
    )(x, g_pre, w, wkr)


PREP_TM = 256
PREP_PROMPT_TILES = N_PROMPT // PREP_TM


def _mla_prep_kernel(*refs, n_alias):
    (cq_ref, ckv_ref, kr_ref, cs_ref, qg_ref, kvg_ref, wuq_ref, wukv_ref,
     q_ref, k_ref, v_ref, krp_ref, ckvp_ref, ckvs_ref, krop_ref, kros_ref) = refs[n_alias:]
    i = pl.program_id(0)
    d = 128
    cs = cs_ref[...]
    hq = _rms(cq_ref[...], qg_ref[...]).astype(BF16)
    q = _dot(hq, wuq_ref[...])

    ckv = _rms(ckv_ref[...], kvg_ref[...])
    kv = _dot(ckv.astype(BF16), wukv_ref[...])
    v_ref[...] = kv[:, MLA_WIDTH:].astype(BF16)

    tk = kr_ref[...] * cs
    rk = tk + pltpu.roll(tk, 64, axis=1)
    lane = lax.broadcasted_iota(jnp.int32, rk.shape, 1)
    krp = jnp.where(lane < QK_ROPE, rk, 0.0).astype(BF16)
    krp_ref[...] = krp

    for h in range(MLA_HEADS):
        t = q[:, MLA_WIDTH + d * h:MLA_WIDTH + d * (h + 1)] * cs
        q_ref[:, 2 * d * h:2 * d * h + d] = (q[:, d * h:d * (h + 1)] * MLA_SCALE).astype(BF16)
        q_ref[:, 2 * d * h + d:2 * d * (h + 1)] = ((t + pltpu.roll(t, 64, axis=1)) * MLA_SCALE).astype(BF16)
        k_ref[:, 2 * d * h:2 * d * h + d] = kv[:, d * h:d * (h + 1)].astype(BF16)
        k_ref[:, 2 * d * h + d:2 * d * (h + 1)] = krp

    @pl.when(i < PREP_PROMPT_TILES)
    def _():
        ckvp_ref[...] = ckv
        krop_ref[...] = rk[:, :QK_ROPE]

    @pl.when(i == PREP_PROMPT_TILES)
    def _():
        ckvs_ref[...] = ckv
        kros_ref[...] = rk[:, :QK_ROPE]


def _mla_prep(p, kr, cs, qg, kvg, wuq, wukv, l, stacked):
    tm = PREP_TM
    row = lambda c: (lambda i: (i, c))
    const = lambda i: (0, 0)
    layer = lambda i: (l, 0)
    prompt_blk = lambda i: (l * PREP_PROMPT_TILES + jnp.minimum(i, PREP_PROMPT_TILES - 1), 0)
    n_alias = 0 if stacked is None else 4
    in_specs = [
        pl.BlockSpec((tm, Q_LORA), row(P_CQ // Q_LORA)),
        pl.BlockSpec((tm, KV_LORA), row(P_CKV // KV_LORA)),
        pl.BlockSpec((tm, 128), row(0)),
        pl.BlockSpec((tm, 128), row(0)),
        pl.BlockSpec((1, Q_LORA), const),
        pl.BlockSpec((1, KV_LORA), const),
        pl.BlockSpec((Q_LORA, 2048), layer),
        pl.BlockSpec((KV_LORA, 2048), layer),
    ]
    args = [p, p, kr, cs, qg, kvg, wuq, wukv]
    if stacked is not None:
        in_specs = [pl.BlockSpec(memory_space=pl.ANY)] * 4 + in_specs
        args = list(stacked) + args
    return pl.pallas_call(
        functools.partial(_mla_prep_kernel, n_alias=n_alias),
        grid=(N_TOK // tm,),
        in_specs=in_specs,
        out_specs=[
            pl.BlockSpec((tm, 2 * MLA_WIDTH), row(0)),
            pl.BlockSpec((tm, 2 * MLA_WIDTH), row(0)),
            pl.BlockSpec((tm, MLA_WIDTH), row(0)),
            pl.BlockSpec((tm, 128), row(0)),
            pl.BlockSpec((tm, KV_LORA), prompt_blk),
            pl.BlockSpec((tm, KV_LORA), layer),
            pl.BlockSpec((tm, QK_ROPE), prompt_blk),
            pl.BlockSpec((tm, QK_ROPE), layer),
        ],
        out_shape=[
            jax.ShapeDtypeStruct((N_TOK, 2 * MLA_WIDTH), BF16),
            jax.ShapeDtypeStruct((N_TOK, 2 * MLA_WIDTH), BF16),
            jax.ShapeDtypeStruct((N_TOK, MLA_WIDTH), BF16),
            jax.ShapeDtypeStruct((N_TOK, 128), BF16),
            jax.ShapeDtypeStruct((DEPTH * N_PROMPT, KV_LORA), F32),
            jax.ShapeDtypeStruct((DEPTH * N_SAMPLE, KV_LORA), F32),
            jax.ShapeDtypeStruct((DEPTH * N_PROMPT, QK_ROPE), F32),
            jax.ShapeDtypeStruct((DEPTH * N_SAMPLE, QK_ROPE), F32),
        ],
        input_output_aliases={k: 4 + k for k in range(n_alias)},
        compiler_params=_params(("arbitrary",)),
        name="mla_prep",
    )(*args)


def _mm_kernel(x_ref, w_ref, *o_refs):
    y = _dot(x_ref[...].astype(BF16), w_ref[...])
    off = 0
    for o_ref in o_refs:
        n = o_ref.shape[1]
        o_ref[...] = y[:, off:off + n].astype(o_ref.dtype)
        off += n


def _mem_kv(mem_rows, w_mem_kv):
    tm = 512
    nr = BATCH * N_MEM // tm
    out = pl.BlockSpec((tm, MEM_WIDTH), lambda l, i: (l * nr + i, 0))
    return pl.pallas_call(
        _mm_kernel,
        grid=(DEPTH, nr),
        in_specs=[pl.BlockSpec((tm, D_MODEL), lambda l, i: (i, 0)),
                  pl.BlockSpec((D_MODEL, 2 * MEM_WIDTH), lambda l, i: (l, 0))],
        out_specs=[out, out],
        out_shape=[jax.ShapeDtypeStruct((DEPTH * BATCH * N_MEM, MEM_WIDTH), F32)] * 2,
        compiler_params=_params(("parallel", "parallel")),
        name="mem_kv",
    )(mem_rows, w_mem_kv)


ATT_TQ = 512


def _softmax_pv(parts, z):
    m = functools.reduce(jnp.maximum, [jnp.max(s, axis=1, keepdims=True) for s, _ in parts])
    l = 0.0
    o = 0.0
    for s, v in parts:
        e = jnp.exp(s - m)
        l = l + jnp.sum(e, axis=1, keepdims=True)
        o = o + _dot(e.astype(BF16), v)
    return o * (1.0 / l) * _silu(z)


def _chunk_mask(q_pos0, k_pos0, nq, nk):
    qp = q_pos0 + lax.broadcasted_iota(jnp.int32, (nq, nk), 0)
    kp = k_pos0 + lax.broadcasted_iota(jnp.int32, (nq, nk), 1)
    sh = CHUNK.bit_length() - 1
    return lax.shift_right_logical(kp, sh) <= lax.shift_right_logical(qp, sh)


def _attn_prompt_kernel(q_ref, k_ref, v_ref, z_ref, o_ref):
    tq = ATT_TQ
    for qi in range(SEQ // tq):
        lo, hi = qi * tq, (qi + 1) * tq
        q = q_ref[lo:hi, :]
        sd = jnp.where(_chunk_mask(lo, lo, tq, tq), _dot_nt(q, k_ref[lo:hi, :]), NEG_INF)
        parts = [(sd, v_ref[lo:hi, :])]
        if qi > 0:
            parts.append((_dot_nt(q, k_ref[0:lo, :]), v_ref[0:lo, :]))
        o_ref[lo:hi, :] = _softmax_pv(parts, z_ref[lo:hi, :]).astype(BF16)


def _attn_prompt(q, k, v, p):
    qk = pl.BlockSpec((SEQ, 256), lambda b, h: (b, h))
    blk = lambda c0: pl.BlockSpec((SEQ, 128), lambda b, h: (b, c0 + h))
    return pl.pallas_call(
        _attn_prompt_kernel,
        grid=(BATCH, MLA_HEADS),
        in_specs=[qk, qk, blk(0), blk(P_ZA // 128)],
        out_specs=blk(0),
        out_shape=jax.ShapeDtypeStruct((N_PROMPT, MLA_WIDTH), BF16),
        compiler_params=_params(("parallel", "parallel")),
        name="attn_prompt",
    )(q, k, v, p)


def _attn_sample_kernel(q_ref, ckvc_ref, krct_ref, ckvn_ref, krn_ref, wukv_ref, z_ref, o_ref):
    nh, d = MLA_HEADS, 128
    qa = jnp.concatenate(
        [_dot_nt(q_ref[:, 2 * d * h:2 * d * h + d], wukv_ref[:, d * h:d * (h + 1)]) for h in range(nh)],
        axis=0).astype(BF16)
    qr = jnp.concatenate([q_ref[:, 2 * d * h + d:2 * d * (h + 1)] for h in range(nh)], axis=0)
    ckvc = ckvc_ref[...].astype(BF16)
    krct = krct_ref[...].astype(BF16)
    ckvn = ckvn_ref[...].astype(BF16)
    rows = nh * DEC_SEQ

    def mask(k_pos0, nk):
        qp = PAST_LEN + (lax.broadcasted_iota(jnp.int32, (rows, nk), 0) & (DEC_SEQ - 1))
        kp = k_pos0 + lax.broadcasted_iota(jnp.int32, (rows, nk), 1)
        sh = CHUNK.bit_length() - 1
        return lax.shift_right_logical(kp, sh) <= lax.shift_right_logical(qp, sh)

    sp = jnp.where(mask(0, PAST_LEN), _dot_nt(qa, ckvc) + _dot(qr[:, :QK_ROPE], krct), NEG_INF)
    sn = jnp.where(mask(PAST_LEN, DEC_SEQ), _dot_nt(qa, ckvn) + _dot_nt(qr, krn_ref[...]), NEG_INF)
    m = jnp.maximum(jnp.max(sp, axis=1, keepdims=True), jnp.max(sn, axis=1, keepdims=True))
    ep, en = jnp.exp(sp - m), jnp.exp(sn - m)
    l = jnp.sum(ep, axis=1, keepdims=True) + jnp.sum(en, axis=1, keepdims=True)
    lat = ((_dot(ep.astype(BF16), ckvc) + _dot(en.astype(BF16), ckvn)) * (1.0 / l)).astype(BF16)
    o = jnp.concatenate(
        [_dot(lat[DEC_SEQ * h:DEC_SEQ * (h + 1)], wukv_ref[:, MLA_WIDTH + d * h:MLA_WIDTH + d * (h + 1)])
         for h in range(nh)], axis=1)
    o_ref[...] = (o * _silu(z_ref[...])).astype(BF16)


def _attn_sample(q, ckv_cache, krc_t, ckv_s, krp, wukv, p, l):
    assert DEC_SEQ & (DEC_SEQ - 1) == 0
    r0 = N_PROMPT // DEC_SEQ
    new = lambda width, c: pl.BlockSpec((DEC_SEQ, width), lambda b: (r0 + b, c))
    stream = lambda rows, width: pl.BlockSpec((rows, width), lambda b: (l * DEC_BATCH + b, 0))
    return pl.pallas_call(
        _attn_sample_kernel,
        grid=(DEC_BATCH,),
        in_specs=[new(2 * MLA_WIDTH, 0),
                  stream(PAST_LEN, KV_LORA), stream(QK_ROPE, PAST_LEN),
                  stream(DEC_SEQ, KV_LORA),
                  new(128, 0),
                  pl.BlockSpec((KV_LORA, 2048), lambda b: (l, 0)),
                  new(MLA_WIDTH, P_ZA // MLA_WIDTH)],
        out_specs=pl.BlockSpec((DEC_SEQ, MLA_WIDTH), lambda b: (b, 0)),
        out_shape=jax.ShapeDtypeStruct((N_SAMPLE, MLA_WIDTH), BF16),
        compiler_params=_params(("parallel",)),
        name="attn_sample",
    )(q, ckv_cache, krc_t, ckv_s, krp, wukv, p)


def _mem_attn_kernel(q_ref, z_ref, k_ref, v_ref, o_ref):
    q = q_ref[...] * MEM_SCALE
    outs = []
    for h in range(MEM_HEADS):
        sl = slice(MEM_HEAD_DIM * h, MEM_HEAD_DIM * (h + 1))
        s = _dot_nt(q[:, sl].astype(BF16), k_ref[:, sl].astype(BF16))
        m = jnp.max(s, axis=1, keepdims=True)
        e = jnp.exp(s - m)
        l = jnp.sum(e, axis=1, keepdims=True)
        outs.append(_dot(e.astype(BF16), v_ref[:, sl].astype(BF16)) * (1.0 / l))
    o_ref[...] = (jnp.concatenate(outs, axis=1) * _silu(z_ref[...])).astype(BF16)


def _mem_attn(p, mem_k, mem_v, *, nb, seq, tq, row0, mem_blk0, name):
    seq_blocks = seq // tq
    r0 = row0 // tq
    rowmap = lambda c: (lambda b, i: (r0 + b * seq_blocks + i, c))
    mem = pl.BlockSpec((N_MEM, MEM_WIDTH), lambda b, i: (mem_blk0 + b, 0))
    return pl.pallas_call(
        _mem_attn_kernel,
        grid=(nb, seq_blocks),
        in_specs=[pl.BlockSpec((tq, MEM_WIDTH), rowmap(P_QM // MEM_WIDTH)),
                  pl.BlockSpec((tq, MEM_WIDTH), rowmap(P_ZM // MEM_WIDTH)),
                  mem, mem],
        out_specs=pl.BlockSpec((tq, MEM_WIDTH), lambda b, i: (b * seq_blocks + i, 0)),
        out_shape=jax.ShapeDtypeStruct((nb * seq, MEM_WIDTH), BF16),
        compiler_params=_params(("parallel", "parallel")),
        name=name,
    )(p, p, mem_k, mem_v)


def _s5_disc_kernel(are_ref, aim_ref, ldt_ref, bre_ref, bim_ref, abr_ref, abi_ref, bbr_ref, bbi_ref):
    dt = jnp.exp(ldt_ref[...])
    lr, li = are_ref[...], aim_ref[...]
    mag = jnp.exp(lr * dt)
    ab_re, ab_im = mag * jnp.cos(li * dt), mag * jnp.sin(li * dt)
    den = lr * lr + li * li
    nr, ni = ab_re - 1.0, ab_im
    f_re = (nr * lr + ni * li) / den
    f_im = (ni * lr - nr * li) / den
    abr_ref[...] = ab_re
    abi_ref[...] = ab_im
    br, bi = bre_ref[...], bim_ref[...]
    bbr_ref[...] = f_re * br - f_im * bi
    bbi_ref[...] = f_re * bi + f_im * br


def _s5_discretize(a_re, a_im, log_dt, b_re_t, b_im_t):
    G, P, C = a_re.shape[0], SSM_STATE, SSM_GROUP
    ab_re, ab_im, bb_re_t, bb_im_t = pl.pallas_call(
        _s5_disc_kernel,
        out_shape=[jax.ShapeDtypeStruct((G, 1, P), F32), jax.ShapeDtypeStruct((G, 1, P), F32),
                   jax.ShapeDtypeStruct((G, C, P), F32), jax.ShapeDtypeStruct((G, C, P), F32)],
        name="s5_discretize",
    )(a_re.reshape(G, 1, P), a_im.reshape(G, 1, P), log_dt.reshape(G, 1, 1), b_re_t, b_im_t)
    return ab_re.reshape(G, P), ab_im.reshape(G, P), bb_re_t, bb_im_t


SSM_PAIRS = 4
SSM_HALF_STATE = 512
SSM_SUB = 32


def _ssm_kernel(*refs, nseq, tt):
    u_refs, z_refs = refs[:nseq], refs[nseq:2 * nseq]
    (d_ref, wglu_ref, bmat_ref, cmat_ref, are_ref, aim_ref, h0_ref, pin_ref, pout_ref,
     y_ref, ht_ref, st_ref, hb_ref, g_ref) = refs[2 * nseq:]
    q = 2 * nseq
    r = nseq * SSM_SUB
    hs = SSM_HALF_STATE
    c = pl.program_id(0)

    @pl.when(c == 0)
    def _():
        st_ref[...] = h0_ref[...]

    lane = lax.broadcasted_iota(jnp.int32, (r, SSM_WIDTH), 1)
    low = (lane & 128) == 0
    rows2 = lax.broadcasted_iota(jnp.int32, (q * SSM_SUB, SSM_WIDTH), 0)
    lane2 = lax.broadcasted_iota(jnp.int32, (q * SSM_SUB, SSM_WIDTH), 1)
    keep = ((rows2 & 1) == 0) == ((lane2 & 128) == 0)

    def sub_tile(s, carry):
        t0 = pl.multiple_of(s * SSM_SUB, SSM_SUB)
        u = jnp.concatenate([u_refs[j][pl.ds(t0, SSM_SUB), :] for j in range(nseq)], axis=0)
        ub = u.astype(BF16)
        zero = jnp.zeros_like(ub)
        stacked = jnp.concatenate([jnp.where(low, ub, zero), jnp.where(low, zero, ub)], axis=0)
        lall = _dot(pin_ref[...], stacked).astype(BF16)
        for p in range(SSM_PAIRS):
            hb_ref[p] = _dot(lall[:, 256 * p:256 * (p + 1)], bmat_ref[p])
        for p in range(SSM_PAIRS):
            ar = are_ref[p]
            ai = aim_ref[p]

            def step(t, hc, p=p, ar=ar, ai=ai):
                hr, hi = hc
                r0 = pl.multiple_of(t * q, q)
                bur = hb_ref[p, pl.ds(r0, q), 0:hs]
                bui = hb_ref[p, pl.ds(r0, q), hs:2 * hs]
                nr = ar * hr - ai * hi + bur
                ni = ar * hi + ai * hr + bui
                hb_ref[p, pl.ds(r0, q), 0:hs] = nr
                hb_ref[p, pl.ds(r0, q), hs:2 * hs] = ni
                return nr, ni

            hr, hi = lax.fori_loop(0, SSM_SUB, step, (st_ref[p, :, 0:hs], st_ref[p, :, hs:2 * hs]), unroll=True)
            st_ref[p, :, 0:hs] = hr
            st_ref[p, :, hs:2 * hs] = hi
        y2 = jnp.concatenate([_dot(hb_ref[p].astype(BF16), cmat_ref[p]) for p in range(SSM_PAIRS)], axis=1)
        zf = jnp.where(keep, y2, 0.0)
        z_hi = zf.astype(BF16)
        r1 = zf - z_hi.astype(F32)
        z_mid = r1.astype(BF16)
        z_lo = (r1 - z_mid.astype(F32)).astype(BF16)
        pout = pout_ref[...]
        yn = _dot(pout, z_hi) + _dot(pout, z_mid) + _dot(pout, z_lo)
        g = _gelu_tanh(yn + d_ref[...] * u).astype(BF16)
        for j in range(nseq):
            g_ref[pl.ds(pl.multiple_of(j * tt + t0, SSM_SUB), SSM_SUB), :] = g[j * SSM_SUB:(j + 1) * SSM_SUB]
        return carry

    lax.fori_loop(0, tt // SSM_SUB, sub_tile, 0)

    g = g_ref[...]
    ga = _dot(g, wglu_ref[:, :SSM_WIDTH])
    gb = _dot(g, wglu_ref[:, SSM_WIDTH:])
    z = jnp.concatenate([z_refs[j][...] for j in range(nseq)], axis=0)
    out = (ga * jax.nn.sigmoid(gb) * _silu(z)).astype(BF16)
    for j in range(nseq):
        y_ref[j] = out[j * tt:(j + 1) * tt]

    @pl.when(c == pl.num_programs(0) - 1)
    def _():
        ht_ref[...] = st_ref[...]


def _ssm_perms(nseq):
    r, q = nseq * SSM_SUB, 2 * nseq
    pin = np.zeros((2 * r, 2 * r), np.float32)
    pout = np.zeros((r, 2 * r), np.float32)
    for t in range(SSM_SUB):
        for j in range(nseq):
            for h in range(2):
                pin[t * q + 2 * j + h, h * r + j * SSM_SUB + t] = 1.0
                pout[j * SSM_SUB + t, t * q + 2 * j + h] = 1.0
    return jnp.asarray(pin, BF16), jnp.asarray(pout, BF16)


def _ssm(p, d, w_glu, bmat, cmat, a_re, a_im, h0, *, nseq, s, tt, row0, l, h0_layer, name):
    nt = s // tt
    q = 2 * nseq
    rb0 = row0 // tt
    pin, pout = _ssm_perms(nseq)
    seq_spec = lambda j, col: pl.BlockSpec((tt, SSM_WIDTH), lambda c: (rb0 + j * nt + c, col))
    whole = lambda a: pl.BlockSpec(a.shape, lambda c: (0,) * a.ndim)
    layer = lambda a, lay: pl.BlockSpec((SSM_PAIRS,) + a.shape[1:], lambda c: (lay, 0, 0))
    in_specs = ([seq_spec(j, P_U // SSM_WIDTH) for j in range(nseq)]
                + [seq_spec(j, P_ZS // SSM_WIDTH) for j in range(nseq)]
                + [pl.BlockSpec((1, SSM_WIDTH), lambda c: (0, 0)),
                   pl.BlockSpec((SSM_WIDTH, 2 * SSM_WIDTH), lambda c: (l, 0)),
                   layer(bmat, l), layer(cmat, l), layer(a_re, l), layer(a_im, l), layer(h0, h0_layer),
                   whole(pin), whole(pout)])
    return pl.pallas_call(
        functools.partial(_ssm_kernel, nseq=nseq, tt=tt),
        grid=(nt,),
        in_specs=in_specs,
        out_specs=[pl.BlockSpec((nseq, tt, SSM_WIDTH), lambda c: (0, c, 0)),
                   pl.BlockSpec((SSM_PAIRS, q, 2 * SSM_HALF_STATE), lambda c: (0, 0, 0))],
        out_shape=[jax.ShapeDtypeStruct((nseq, s, SSM_WIDTH), BF16),
                   jax.ShapeDtypeStruct((SSM_PAIRS, q, 2 * SSM_HALF_STATE), F32)],
        scratch_shapes=[pltpu.VMEM((SSM_PAIRS, q, 2 * SSM_HALF_STATE), F32),
                        pltpu.VMEM((SSM_PAIRS, q * SSM_SUB, 2 * SSM_HALF_STATE), F32),
                        pltpu.VMEM((nseq * tt, SSM_WIDTH), BF16)],
        compiler_params=_params(("arbitrary",)),
        name=name,
    )(*([p] * (2 * nseq)), d, w_glu, bmat, cmat, a_re, a_im, h0, pin, pout)


def _ssm_state_in(h_re, h_im):
    nl, nseq = h_re.shape[:2]

    def arr(h):
        return h.reshape(nl, nseq, SSM_PAIRS, 2, SSM_HALF_STATE).transpose(0, 2, 1, 3, 4).reshape(
            nl * SSM_PAIRS, nseq * 2, SSM_HALF_STATE)
    return jnp.concatenate([arr(h_re), arr(h_im)], axis=-1)


def _ssm_state_out(ht, nseq):
    nl = ht.shape[0]

    def arr(h):
        return h.reshape(nl, SSM_PAIRS, nseq, 2, 8, SSM_STATE).transpose(0, 2, 1, 3, 4, 5).reshape(
            nl, nseq, SSM_GROUPS, SSM_STATE)
    return arr(ht[..., :SSM_HALF_STATE]), arr(ht[..., SSM_HALF_STATE:])


def _ssm_mats(bb_re_t, bb_im_t, c_re, c_im):
    eye8 = jnp.eye(8, dtype=F32)

    def bmat(bb):
        b6 = bb.reshape(-1, 2, 8, SSM_GROUP, SSM_STATE)
        return jnp.einsum('ahgcp,gk->ahgckp', b6, eye8).reshape(-1, 256, SSM_HALF_STATE)

    def cmat(cc):
        c6 = cc.reshape(-1, 2, 8, SSM_GROUP, SSM_STATE)
        return jnp.einsum('ahgcp,gk->akphgc', c6, eye8).reshape(-1, SSM_HALF_STATE, 256)

    b_all = jnp.concatenate([bmat(bb_re_t), bmat(bb_im_t)], axis=2).astype(BF16)
    c_all = jnp.concatenate([cmat(c_re), cmat(-c_im)], axis=1).astype(BF16)
    return b_all, c_all


def _ssm_a_rows(a, nseq):
    a3 = a.reshape(-1, 1, 2, SSM_HALF_STATE)
    return jnp.broadcast_to(a3, (a3.shape[0], nseq, 2, SSM_HALF_STATE)).reshape(
        a3.shape[0], nseq * 2, SSM_HALF_STATE)


OUT_TM = 256


OUT_PROMPT_TILES = N_PROMPT // OUT_TM


def _merge_out_kernel(x_ref, ysp_ref, yss_ref, yap_ref, yas_ref, ymp_ref, yms_ref, g0_ref, g1_ref, g2_ref,
                      ws_ref, wa_ref, wm_ref, wo_ref, gp_ref, *o_refs):
    i = pl.program_id(0)
    pick = lambda prompt_ref, sample_ref: jnp.where(i < OUT_PROMPT_TILES, prompt_ref[...], sample_ref[...])
    merged = (jax.nn.sigmoid(g0_ref[...]) * _dot(pick(ysp_ref, yss_ref), ws_ref[...])
              + jax.nn.sigmoid(g1_ref[...]) * _dot(pick(yap_ref, yas_ref), wa_ref[...])
              + jax.nn.sigmoid(g2_ref[...]) * _dot(pick(ymp_ref, yms_ref), wm_ref[...]))
    out = _dot(merged.astype(BF16), wo_ref[...])
    y = x_ref[...] + _rms(out, gp_ref[...])
    if len(o_refs) == 1:
        o_refs[0][...] = y
    else:
        @pl.when(i < OUT_PROMPT_TILES)
        def _():
            o_refs[0][...] = y

        @pl.when(i == OUT_PROMPT_TILES)
        def _():
            o_refs[1][...] = y


def _merge_out(x, y_ssm, y_mla, y_mem, p, w_ssm_o, w_mla_o, w_mem_o, w_out, g_post, l, split):
    tm = OUT_TM
    assert N_SAMPLE == tm
    row = lambda c: (lambda i: (i, c))
    const = lambda i: (0, 0)
    prompt_row = lambda i: (jnp.minimum(i, OUT_PROMPT_TILES - 1), 0)
    resident = lambda shape: pl.BlockSpec(shape, lambda i: (l, 0), pipeline_mode=pl.Buffered(1))
    branch = lambda width: [pl.BlockSpec((tm, width), prompt_row), pl.BlockSpec((tm, width), const)]
    if split:
        out_specs = [pl.BlockSpec((tm, D_MODEL), prompt_row),
                     pl.BlockSpec((tm, D_MODEL), const)]
        out_shape = [jax.ShapeDtypeStruct((N_PROMPT, D_MODEL), F32),
                     jax.ShapeDtypeStruct((N_SAMPLE, D_MODEL), F32)]
    else:
        out_specs = pl.BlockSpec((tm, D_MODEL), row(0))
        out_shape = jax.ShapeDtypeStruct((N_TOK, D_MODEL), F32)
    return pl.pallas_call(
        _merge_out_kernel,
        grid=(N_TOK // tm,),
        in_specs=[pl.BlockSpec((tm, D_MODEL), row(0)),
                  *branch(SSM_WIDTH), *branch(MLA_WIDTH), *branch(MEM_WIDTH),
                  pl.BlockSpec((tm, D_MODEL), row(0)),
                  pl.BlockSpec((tm, D_MODEL), row(1)),
                  pl.BlockSpec((tm, D_MODEL), row(2)),
                  resident((SSM_WIDTH, D_MODEL)),
                  resident((MLA_WIDTH, D_MODEL)),
                  resident((MEM_WIDTH, D_MODEL)),
                  resident((D_MODEL, D_MODEL)),
                  pl.BlockSpec((1, D_MODEL), const)],
        out_specs=out_specs,
        out_shape=out_shape,
        compiler_params=_params(("arbitrary",)),
        name="merge_out",
    )(x, *y_ssm, *y_mla, *y_mem, p, p, p, w_ssm_o, w_mla_o, w_mem_o, w_out, g_post)


def _rot_half_cols(w):
    half = w.shape[-1] // 2
    return jnp.concatenate([w[..., half:], w[..., :half]], axis=-1)


def _stack_rows(w):
    return w.astype(BF16).reshape(w.shape[0] * w.shape[1], w.shape[2])


def _prep_small_weights(w_in_t, w_uq, w_uk, w_uv):
    wk = jnp.swapaxes(w_in_t[:, W_IN_K_ROPE:W_IN_K_ROPE + QK_ROPE, :], 1, 2)
    w_kr = _stack_rows(jnp.concatenate([wk, _rot_half_cols(wk)], axis=-1))
    uq = w_uq.reshape(DEPTH, Q_LORA, MLA_HEADS, QK_NOPE + QK_ROPE)
    uq_nope = uq[..., :QK_NOPE].reshape(DEPTH, Q_LORA, MLA_WIDTH)
    uq_rope = uq[..., QK_NOPE:]
    uq_rr = jnp.concatenate([uq_rope, _rot_half_cols(uq_rope)], axis=-1).reshape(DEPTH, Q_LORA, MLA_WIDTH)
    w_uq2 = jnp.concatenate([uq_nope, uq_rr], axis=-1)
    w_ukv = jnp.concatenate([w_uk, w_uv], axis=-1)
    return w_kr, _stack_rows(w_uq2), _stack_rows(w_ukv)


def _rope_table():
    half = QK_ROPE // 2
    pos = jnp.concatenate([jnp.tile(jnp.arange(SEQ, dtype=jnp.int32), BATCH),
                           jnp.tile(PAST_LEN + jnp.arange(DEC_SEQ, dtype=jnp.int32), DEC_BATCH)])
    inv = ROPE_THETA ** (-jnp.arange(half, dtype=F32) / half)
    ang = pos.astype(F32)[:, None] * inv[None, :]
    cos, sin = jnp.cos(ang), jnp.sin(ang)
    return jnp.concatenate([cos, cos, -sin, sin], axis=-1)


def kernel(x_prompt, x_sample, cache_mla_ckv, cache_mla_krope, cache_mem_k, cache_mem_v, state_ssm_re, state_ssm_im, mem_prompt, norm_pre, w_in, ssm_a_re, ssm_a_im, ssm_log_dt, ssm_b_re, ssm_b_im, ssm_c_re, ssm_c_im, ssm_d, w_glu, mla_q_norm, w_uq, mla_kv_norm, w_uk, w_uv, w_mem_k, w_mem_v, w_ssm_o, w_mla_o, w_mem_o, w_out, norm_post):
    w_in_t = jnp.swapaxes(w_in, 1, 2)
    w_main = _w_relayout(w_in_t.reshape(DEPTH * W_IN_COLS, D_MODEL))
    w_kr, w_uq2, w_ukv = _prep_small_weights(w_in_t, w_uq, w_uk, w_uv)
    w_glu_b, w_ssm_o_b, w_mla_o_b, w_mem_o_b, w_out_b = (
        _stack_rows(w) for w in (w_glu, w_ssm_o, w_mla_o, w_mem_o, w_out))
    w_mem_kv = _stack_rows(jnp.concatenate([w_mem_k, w_mem_v], axis=-1))
    cs = _rope_table()
    mem_rows = mem_prompt.reshape(BATCH * N_MEM, D_MODEL)
    ckv_cache = cache_mla_ckv.reshape(DEPTH * DEC_BATCH * PAST_LEN, KV_LORA)
    krc_t = jnp.swapaxes(cache_mla_krope, 2, 3).reshape(DEPTH * DEC_BATCH * QK_ROPE, PAST_LEN)
    mem_k_cache = cache_mem_k.reshape(DEPTH * DEC_BATCH * N_MEM, MEM_WIDTH)
    mem_v_cache = cache_mem_v.reshape(DEPTH * DEC_BATCH * N_MEM, MEM_WIDTH)

    zeros_state = jnp.zeros((BATCH, SSM_GROUPS, SSM_STATE), F32)
    x = jnp.concatenate([x_prompt.reshape(N_PROMPT, D_MODEL), x_sample.reshape(N_SAMPLE, D_MODEL)], axis=0)

    mk_all, mv_all = _mem_kv(mem_rows, w_mem_kv)

    lg = DEPTH * SSM_GROUPS
    flat = lambda a: a.reshape((lg,) + a.shape[2:])
    ab_re, ab_im, bb_re_t, bb_im_t = _s5_discretize(
        flat(ssm_a_re), flat(ssm_a_im), ssm_log_dt.reshape(lg),
        flat(jnp.swapaxes(ssm_b_re, 2, 3)), flat(jnp.swapaxes(ssm_b_im, 2, 3)))
    bmat, cmat = _ssm_mats(bb_re_t, bb_im_t, flat(ssm_c_re), flat(ssm_c_im))
    a_rows_p = (_ssm_a_rows(ab_re, BATCH), _ssm_a_rows(ab_im, BATCH))
    a_rows_s = (_ssm_a_rows(ab_re, DEC_BATCH), _ssm_a_rows(ab_im, DEC_BATCH))
    h0_p = _ssm_state_in(zeros_state[None], zeros_state[None])
    h0_s = _ssm_state_in(state_ssm_re, state_ssm_im)

    ht_p, ht_s = [], []
    stacked = None
    for l in range(DEPTH):
        p, kr = _in_proj(x, norm_pre[l][None], w_main, w_kr, l)

        ys_p, ht = _ssm(p, ssm_d[l][None], w_glu_b, bmat, cmat, *a_rows_p, h0_p,
                        nseq=BATCH, s=SEQ, tt=128, row0=0, l=l, h0_layer=0, name="ssm_prompt")
        ht_p.append(ht)
        ys_s, ht = _ssm(p, ssm_d[l][None], w_glu_b, bmat, cmat, *a_rows_s, h0_s,
                        nseq=DEC_BATCH, s=DEC_SEQ, tt=DEC_SEQ, row0=N_PROMPT, l=l, h0_layer=l, name="ssm_sample")
        ht_s.append(ht)
        y_ssm = (ys_p.reshape(N_PROMPT, SSM_WIDTH), ys_s.reshape(N_SAMPLE, SSM_WIDTH))

        q, k, v, krp, *stacked = _mla_prep(p, kr, cs, mla_q_norm[l][None], mla_kv_norm[l][None],
                                           w_uq2, w_ukv, l, stacked)
        y_mla = (_attn_prompt(q, k, v, p),
                 _attn_sample(q, ckv_cache, krc_t, stacked[1], krp, w_ukv, p, l))

        y_mem = (_mem_attn(p, mk_all, mv_all, nb=BATCH, seq=SEQ, tq=1024, row0=0, mem_blk0=l * BATCH,
                           name="mem_attn_prompt"),
                 _mem_attn(p, mem_k_cache, mem_v_cache, nb=DEC_BATCH, seq=DEC_SEQ, tq=DEC_SEQ, row0=N_PROMPT,
                           mem_blk0=l * DEC_BATCH, name="mem_attn_sample"))

        x = _merge_out(x, y_ssm, y_mla, y_mem, p, w_ssm_o_b, w_mla_o_b, w_mem_o_b, w_out_b,
                       norm_post[l][None], l, split=l == DEPTH - 1)

    x_p, x_s = x
    ckv_p, ckv_s, kro_p, kro_s = stacked
    p_re, p_im = _ssm_state_out(jnp.stack(ht_p), BATCH)
    s_re, s_im = _ssm_state_out(jnp.stack(ht_s), DEC_BATCH)
    mem_shape = (DEPTH, BATCH, N_MEM, MEM_HEADS, MEM_HEAD_DIM)
    return (x_p.reshape(BATCH, SEQ, D_MODEL), x_s.reshape(DEC_BATCH, DEC_SEQ, D_MODEL),
            p_re, p_im,
            ckv_p.reshape(DEPTH, BATCH, SEQ, KV_LORA), kro_p.reshape(DEPTH, BATCH, SEQ, QK_ROPE),
            mk_all.reshape(mem_shape), mv_all.reshape(mem_shape),
            s_re, s_im,
            ckv_s.reshape(DEPTH, DEC_BATCH, DEC_SEQ, KV_LORA), kro_s.reshape(DEPTH, DEC_BATCH, DEC_SEQ, QK_ROPE))
```

```python
import functools
import math

import jax
import jax.numpy as jnp
import numpy as np
from jax import lax
from jax.experimental import pallas as pl
from jax.experimental.pallas import tpu as pltpu

D_MODEL = 2048
BATCH = 4
SEQ = 2048
DEPTH = 4
DEC_BATCH = 8
DEC_SEQ = 32
PAST_LEN = 2048
CHUNK = 64
N_MEM = 256
EPS = 1e-6
NEG_INF = -1e30
SSM_WIDTH = 1024
SSM_GROUP = 16
SSM_GROUPS = 64
SSM_STATE = 64
MLA_HEADS = 8
QK_NOPE = 128
QK_ROPE = 64
V_HEAD = 128
Q_LORA = 512
KV_LORA = 512
MLA_WIDTH = 1024
ROPE_THETA = 10000.0
MEM_HEADS = 4
MEM_HEAD_DIM = 128
MEM_WIDTH = 512

N_PROMPT = BATCH * SEQ
N_SAMPLE = DEC_BATCH * DEC_SEQ
N_TOK = N_PROMPT + N_SAMPLE

P_GATE = 0
P_U = 6144
P_ZS = 7168
P_CQ = 8192
P_CKV = 8704
P_ZA = 9216
P_QM = 10240
P_ZM = 10752
P_COLS = 11264
W_IN_COLS = 11328

MLA_SCALE = (QK_NOPE + QK_ROPE) ** -0.5
MEM_SCALE = MEM_HEAD_DIM ** -0.5

V7X_VMEM_LIMIT = 52 * 1024 * 1024

BF16 = jnp.bfloat16
F32 = jnp.float32


def _params(sem, vmem=V7X_VMEM_LIMIT):
    return pltpu.CompilerParams(dimension_semantics=sem, vmem_limit_bytes=vmem)


def _dot(a, b):
    return jnp.dot(a, b, preferred_element_type=F32)


def _dot_nt(a, b):
    return lax.dot_general(a, b, (((1,), (1,)), ((), ())), preferred_element_type=F32)


def _rms(x, g):
    return x * lax.rsqrt(jnp.mean(x * x, axis=-1, keepdims=True) + EPS) * g


def _silu(x):
    return x * jax.nn.sigmoid(x)


def _gelu_tanh(x):
    return 0.5 * x * (1.0 + jnp.tanh(math.sqrt(2.0 / math.pi) * (x + 0.044715 * (x * x * x))))


IN_TM = 1408
IN_TN = 512

W_IN_K_ROPE = 2 * SSM_WIDTH + Q_LORA + KV_LORA
_W_GATE0 = W_IN_K_ROPE + QK_ROPE + MLA_WIDTH + 2 * MEM_WIDTH
_W_UNIT = 64


def _w_src_row(l, j):
    u = lambda c: c // _W_UNIT
    assert all(c % _W_UNIT == 0 for c in (_W_GATE0, IN_TN, W_IN_K_ROPE + QK_ROPE, W_IN_COLS))
    n_gate, n_front = P_U // IN_TN, P_ZA // IN_TN
    start = jnp.where(j < n_gate, u(_W_GATE0) + u(IN_TN) * j,
                      jnp.where(j < n_front, u(IN_TN) * (j - n_gate),
                                u(W_IN_K_ROPE + QK_ROPE) + u(IN_TN) * (j - n_front)))
    return (l * u(W_IN_COLS) + start) * _W_UNIT


def _in_proj_kernel(x_ref, g_ref, wt_ref, wkr_ref, p_ref, kr_ref, h_ref):
    @pl.when(pl.program_id(1) == 0)
    def _():
        h = _rms(x_ref[...], g_ref[...]).astype(BF16)
        h_ref[...] = h
        kr_ref[...] = _dot(h, wkr_ref[...])

    p_ref[...] = _dot_nt(h_ref[...], wt_ref[...].astype(BF16))


def _in_proj(x, g_pre, w_in_t, wkr, l):
    return pl.pallas_call(
        _in_proj_kernel,
        grid=(N_TOK // IN_TM, P_COLS // IN_TN),
        in_specs=[
            pl.BlockSpec((IN_TM, D_MODEL), lambda i, j: (i, 0)),
            pl.BlockSpec((1, D_MODEL), lambda i, j: (0, 0)),
            pl.BlockSpec((pl.Element(IN_TN), pl.Element(D_MODEL)), lambda i, j: (_w_src_row(l, j), 0)),
            pl.BlockSpec((D_MODEL, 128), lambda i, j: (l, 0)),
        ],
        out_specs=[
            pl.BlockSpec((IN_TM, IN_TN), lambda i, j: (i, j)),
            pl.BlockSpec((IN_TM, 128), lambda i, j: (i, 0)),
        ],
        out_shape=[
            jax.ShapeDtypeStruct((N_TOK, P_COLS), F32),
            jax.ShapeDtypeStruct((N_TOK, 128), F32),
        ],
        scratch_shapes=[pltpu.VMEM((IN_TM, D_MODEL), BF16)],
        compiler_params=_params(("parallel", "arbitrary")),
        name="in_proj",
    )(x, g_pre, w_in_t, wkr)


PREP_TM = 256
PREP_PROMPT_TILES = N_PROMPT // PREP_TM


def _mla_prep_kernel(*refs, n_alias):
    (cq_ref, ckv_ref, kr_ref, cs_ref, qg_ref, kvg_ref, wuq_ref, wukv_ref,
     q_ref, k_ref, v_ref, krp_ref, ckvp_ref, ckvs_ref, krop_ref, kros_ref) = refs[n_alias:]
    i = pl.program_id(0)
    d = 128
    cs = cs_ref[...]
    hq = _rms(cq_ref[...], qg_ref[...]).astype(BF16)
    q = _dot(hq, wuq_ref[...])

    ckv = _rms(ckv_ref[...], kvg_ref[...])
    kv = _dot(ckv.astype(BF16), wukv_ref[...])
    k_ref[...] = kv[:, :MLA_WIDTH].astype(BF16)
    v_ref[...] = kv[:, MLA_WIDTH:].astype(BF16)

    tk = kr_ref[...] * cs
    rk = tk + pltpu.roll(tk, 64, axis=1)
    lane = lax.broadcasted_iota(jnp.int32, rk.shape, 1)
    krp = jnp.where(lane < QK_ROPE, rk, 0.0).astype(BF16)
    krp_ref[...] = krp

    for h in range(MLA_HEADS):
        t = q[:, MLA_WIDTH + d * h:MLA_WIDTH + d * (h + 1)] * cs
        q_ref[:, 2 * d * h:2 * d * h + d] = (q[:, d * h:d * (h + 1)] * MLA_SCALE).astype(BF16)
        q_ref[:, 2 * d * h + d:2 * d * (h + 1)] = ((t + pltpu.roll(t, 64, axis=1)) * MLA_SCALE).astype(BF16)

    @pl.when(i < PREP_PROMPT_TILES)
    def _():
        ckvp_ref[...] = ckv
        krop_ref[...] = rk[:, :QK_ROPE]

    @pl.when(i == PREP_PROMPT_TILES)
    def _():
        ckvs_ref[...] = ckv
        kros_ref[...] = rk[:, :QK_ROPE]


def _mla_prep(p, kr, cs, qg, kvg, wuq, wukv, l, stacked):
    tm = PREP_TM
    row = lambda c: (lambda i: (i, c))
    const = lambda i: (0, 0)
    layer = lambda i: (l, 0)
    prompt_blk = lambda i: (l * PREP_PROMPT_TILES + jnp.minimum(i, PREP_PROMPT_TILES - 1), 0)
    n_alias = 0 if stacked is None else 4
    in_specs = [
        pl.BlockSpec((tm, Q_LORA), row(P_CQ // Q_LORA)),
        pl.BlockSpec((tm, KV_LORA), row(P_CKV // KV_LORA)),
        pl.BlockSpec((tm, 128), row(0)),
        pl.BlockSpec((tm, 128), row(0)),
        pl.BlockSpec((1, Q_LORA), const),
        pl.BlockSpec((1, KV_LORA), const),
        pl.BlockSpec((Q_LORA, 2048), layer),
        pl.BlockSpec((KV_LORA, 2048), layer),
    ]
    args = [p, p, kr, cs, qg, kvg, wuq, wukv]
    if stacked is not None:
        in_specs = [pl.BlockSpec(memory_space=pl.ANY)] * 4 + in_specs
        args = list(stacked) + args
    return pl.pallas_call(
        functools.partial(_mla_prep_kernel, n_alias=n_alias),
        grid=(N_TOK // tm,),
        in_specs=in_specs,
        out_specs=[
            pl.BlockSpec((tm, 2 * MLA_WIDTH), row(0)),
            pl.BlockSpec((tm, MLA_WIDTH), row(0)),
            pl.BlockSpec((tm, MLA_WIDTH), row(0)),
            pl.BlockSpec((tm, 128), row(0)),
            pl.BlockSpec((tm, KV_LORA), prompt_blk),
            pl.BlockSpec((tm, KV_LORA), layer),
            pl.BlockSpec((tm, QK_ROPE), prompt_blk),
            pl.BlockSpec((tm, QK_ROPE), layer),
        ],
        out_shape=[
            jax.ShapeDtypeStruct((N_TOK, 2 * MLA_WIDTH), BF16),
            jax.ShapeDtypeStruct((N_TOK, MLA_WIDTH), BF16),
            jax.ShapeDtypeStruct((N_TOK, MLA_WIDTH), BF16),
            jax.ShapeDtypeStruct((N_TOK, 128), BF16),
            jax.ShapeDtypeStruct((DEPTH * N_PROMPT, KV_LORA), F32),
            jax.ShapeDtypeStruct((DEPTH * N_SAMPLE, KV_LORA), F32),
            jax.ShapeDtypeStruct((DEPTH * N_PROMPT, QK_ROPE), F32),
            jax.ShapeDtypeStruct((DEPTH * N_SAMPLE, QK_ROPE), F32),
        ],
        input_output_aliases={k: 4 + k for k in range(n_alias)},
        compiler_params=_params(("arbitrary",)),
        name="mla_prep",
    )(*args)


def _mm_kernel(x_ref, w_ref, *o_refs):
    y = _dot(x_ref[...].astype(BF16), w_ref[...])
    off = 0
    for o_ref in o_refs:
        n = o_ref.shape[1]
        o_ref[...] = y[:, off:off + n].astype(o_ref.dtype)
        off += n


def _mem_kv(mem_rows, w_mem_kv):
    tm = 512
    nr = BATCH * N_MEM // tm
    out = pl.BlockSpec((tm, MEM_WIDTH), lambda l, i: (l * nr + i, 0))
    return pl.pallas_call(
        _mm_kernel,
        grid=(DEPTH, nr),
        in_specs=[pl.BlockSpec((tm, D_MODEL), lambda l, i: (i, 0)),
                  pl.BlockSpec((D_MODEL, 2 * MEM_WIDTH), lambda l, i: (l, 0))],
        out_specs=[out, out],
        out_shape=[jax.ShapeDtypeStruct((DEPTH * BATCH * N_MEM, MEM_WIDTH), F32)] * 2,
        compiler_params=_params(("parallel", "parallel")),
        name="mem_kv",
    )(mem_rows, w_mem_kv)


ATT_TQ = 512


def _chunk_mask(q_pos0, k_pos0, nq, nk):
    qp = q_pos0 + lax.broadcasted_iota(jnp.int32, (nq, nk), 0)
    kp = k_pos0 + lax.broadcasted_iota(jnp.int32, (nq, nk), 1)
    sh = CHUNK.bit_length() - 1
    return lax.shift_right_logical(kp, sh) <= lax.shift_right_logical(qp, sh)


def _attn_prompt_kernel(q_ref, kn_ref, kr_ref, v_ref, z_ref, o_ref, k_scr, v_scr):
    tq, d = ATT_TQ, 128
    k_scr[:, :d] = kn_ref[...]
    k_scr[:, d:] = kr_ref[...]
    v_scr[:, :d] = v_ref[...]
    v_scr[:, d:] = jnp.ones((SEQ, d), BF16)
    for qi in range(SEQ // tq):
        lo, hi = qi * tq, (qi + 1) * tq
        q = q_ref[lo:hi, :]
        sd = jnp.where(_chunk_mask(lo, lo, tq, tq), _dot_nt(q, k_scr[lo:hi, :]), NEG_INF)
        m = jnp.max(sd, axis=1, keepdims=True)
        if qi > 0:
            so = _dot_nt(q, k_scr[0:lo, :])
            m = jnp.maximum(m, jnp.max(so, axis=1, keepdims=True))
        acc = _dot(jnp.exp(sd - m).astype(BF16), v_scr[lo:hi, :])
        if qi > 0:
            acc = acc + _dot(jnp.exp(so - m).astype(BF16), v_scr[0:lo, :])
        o_ref[lo:hi, :] = (acc[:, :d] * (1.0 / acc[:, d:]) * _silu(z_ref[lo:hi, :])).astype(BF16)


def _attn_prompt(q, kn, krp, v, p):
    blk = lambda c0: pl.BlockSpec((SEQ, 128), lambda b, h: (b, c0 + h))
    return pl.pallas_call(
        _attn_prompt_kernel,
        grid=(BATCH, MLA_HEADS),
        in_specs=[pl.BlockSpec((SEQ, 256), lambda b, h: (b, h)),
                  blk(0), pl.BlockSpec((SEQ, 128), lambda b, h: (b, 0)), blk(0), blk(P_ZA // 128)],
        out_specs=blk(0),
        out_shape=jax.ShapeDtypeStruct((N_PROMPT, MLA_WIDTH), BF16),
        scratch_shapes=[pltpu.VMEM((SEQ, 256), BF16), pltpu.VMEM((SEQ, 256), BF16)],
        compiler_params=_params(("parallel", "parallel")),
        name="attn_prompt",
    )(q, kn, krp, v, p)


def _attn_sample_kernel(q_ref, ckvc_ref, krct_ref, ckvn_ref, krn_ref, wukv_ref, z_ref, o_ref):
    nh, d = MLA_HEADS, 128
    qa = jnp.concatenate(
        [_dot_nt(q_ref[:, 2 * d * h:2 * d * h + d], wukv_ref[:, d * h:d * (h + 1)]) for h in range(nh)],
        axis=0).astype(BF16)
    qr = jnp.concatenate([q_ref[:, 2 * d * h + d:2 * d * (h + 1)] for h in range(nh)], axis=0)
    ckvc = ckvc_ref[...].astype(BF16)
    krct = krct_ref[...].astype(BF16)
    ckvn = ckvn_ref[...].astype(BF16)
    rows = nh * DEC_SEQ

    def mask(k_pos0, nk):
        qp = PAST_LEN + (lax.broadcasted_iota(jnp.int32, (rows, nk), 0) & (DEC_SEQ - 1))
        kp = k_pos0 + lax.broadcasted_iota(jnp.int32, (rows, nk), 1)
        sh = CHUNK.bit_length() - 1
        return lax.shift_right_logical(kp, sh) <= lax.shift_right_logical(qp, sh)

    sp = jnp.where(mask(0, PAST_LEN), _dot_nt(qa, ckvc) + _dot(qr[:, :QK_ROPE], krct), NEG_INF)
    sn = jnp.where(mask(PAST_LEN, DEC_SEQ), _dot_nt(qa, ckvn) + _dot_nt(qr, krn_ref[...]), NEG_INF)
    m = jnp.maximum(jnp.max(sp, axis=1, keepdims=True), jnp.max(sn, axis=1, keepdims=True))
    ep, en = jnp.exp(sp - m), jnp.exp(sn - m)
    l = jnp.sum(ep, axis=1, keepdims=True) + jnp.sum(en, axis=1, keepdims=True)
    lat = ((_dot(ep.astype(BF16), ckvc) + _dot(en.astype(BF16), ckvn)) * (1.0 / l)).astype(BF16)
    o = jnp.concatenate(
        [_dot(lat[DEC_SEQ * h:DEC_SEQ * (h + 1)], wukv_ref[:, MLA_WIDTH + d * h:MLA_WIDTH + d * (h + 1)])
         for h in range(nh)], axis=1)
    o_ref[...] = (o * _silu(z_ref[...])).astype(BF16)


def _attn_sample(q, ckv_cache, krc_t, ckv_s, krp, wukv, p, l):
    assert DEC_SEQ & (DEC_SEQ - 1) == 0
    r0 = N_PROMPT // DEC_SEQ
    new = lambda width, c: pl.BlockSpec((DEC_SEQ, width), lambda b: (r0 + b, c))
    stream = lambda rows, width: pl.BlockSpec((rows, width), lambda b: (l * DEC_BATCH + b, 0))
    return pl.pallas_call(
        _attn_sample_kernel,
        grid=(DEC_BATCH,),
        in_specs=[new(2 * MLA_WIDTH, 0),
                  stream(PAST_LEN, KV_LORA), stream(QK_ROPE, PAST_LEN),
                  stream(DEC_SEQ, KV_LORA),
                  new(128, 0),
                  pl.BlockSpec((KV_LORA, 2048), lambda b: (l, 0)),
                  new(MLA_WIDTH, P_ZA // MLA_WIDTH)],
        out_specs=pl.BlockSpec((DEC_SEQ, MLA_WIDTH), lambda b: (b, 0)),
        out_shape=jax.ShapeDtypeStruct((N_SAMPLE, MLA_WIDTH), BF16),
        compiler_params=_params(("parallel",)),
        name="attn_sample",
    )(q, ckv_cache, krc_t, ckv_s, krp, wukv, p)


def _mem_attn_kernel(q_ref, z_ref, k_ref, v_ref, o_ref):
    q = q_ref[...] * MEM_SCALE
    outs = []
    for h in range(MEM_HEADS):
        sl = slice(MEM_HEAD_DIM * h, MEM_HEAD_DIM * (h + 1))
        s = _dot_nt(q[:, sl].astype(BF16), k_ref[:, sl].astype(BF16))
        m = jnp.max(s, axis=1, keepdims=True)
        e = jnp.exp(s - m)
        l = jnp.sum(e, axis=1, keepdims=True)
        outs.append(_dot(e.astype(BF16), v_ref[:, sl].astype(BF16)) * (1.0 / l))
    o_ref[...] = (jnp.concatenate(outs, axis=1) * _silu(z_ref[...])).astype(BF16)


def _mem_attn(p, mem_k, mem_v, *, nb, seq, tq, row0, mem_blk0, name):
    seq_blocks = seq // tq
    r0 = row0 // tq
    rowmap = lambda c: (lambda b, i: (r0 + b * seq_blocks + i, c))
    mem = pl.BlockSpec((N_MEM, MEM_WIDTH), lambda b, i: (mem_blk0 + b, 0))
    return pl.pallas_call(
        _mem_attn_kernel,
        grid=(nb, seq_blocks),
        in_specs=[pl.BlockSpec((tq, MEM_WIDTH), rowmap(P_QM // MEM_WIDTH)),
                  pl.BlockSpec((tq, MEM_WIDTH), rowmap(P_ZM // MEM_WIDTH)),
                  mem, mem],
        out_specs=pl.BlockSpec((tq, MEM_WIDTH), lambda b, i: (b * seq_blocks + i, 0)),
        out_shape=jax.ShapeDtypeStruct((nb * seq, MEM_WIDTH), BF16),
        compiler_params=_params(("parallel", "parallel")),
        name=name,
    )(p, p, mem_k, mem_v)


def _s5_disc_kernel(are_ref, aim_ref, ldt_ref, bre_ref, bim_ref, abr_ref, abi_ref, bbr_ref, bbi_ref):
    dt = jnp.exp(ldt_ref[...])
    lr, li = are_ref[...], aim_ref[...]
    mag = jnp.exp(lr * dt)
    ab_re, ab_im = mag * jnp.cos(li * dt), mag * jnp.sin(li * dt)
    den = lr * lr + li * li
    nr, ni = ab_re - 1.0, ab_im
    f_re = (nr * lr + ni * li) / den
    f_im = (ni * lr - nr * li) / den
    abr_ref[...] = ab_re
    abi_ref[...] = ab_im
    br, bi = bre_ref[...], bim_ref[...]
    bbr_ref[...] = f_re * br - f_im * bi
    bbi_ref[...] = f_re * bi + f_im * br


def _s5_discretize(a_re, a_im, log_dt, b_re_t, b_im_t):
    G, P, C = a_re.shape[0], SSM_STATE, SSM_GROUP
    ab_re, ab_im, bb_re_t, bb_im_t = pl.pallas_call(
        _s5_disc_kernel,
        out_shape=[jax.ShapeDtypeStruct((G, 1, P), F32), jax.ShapeDtypeStruct((G, 1, P), F32),
                   jax.ShapeDtypeStruct((G, C, P), F32), jax.ShapeDtypeStruct((G, C, P), F32)],
        name="s5_discretize",
    )(a_re.reshape(G, 1, P), a_im.reshape(G, 1, P), log_dt.reshape(G, 1, 1), b_re_t, b_im_t)
    return ab_re.reshape(G, P), ab_im.reshape(G, P), bb_re_t, bb_im_t


SSM_PAIRS = 4
SSM_HALF_STATE = 512
SSM_SUB = 32


def _ssm_kernel(*refs, nseq, tt):
    u_refs, z_refs = refs[:nseq], refs[nseq:2 * nseq]
    (d_ref, wglu_ref, bmat_ref, cmat_ref, are_ref, aim_ref, h0_ref, pin_ref, pout_ref,
     y_ref, ht_ref, st_ref, hb_ref, g_ref) = refs[2 * nseq:]
    q = 2 * nseq
    r = nseq * SSM_SUB
    hs = SSM_HALF_STATE
    c = pl.program_id(0)

    @pl.when(c == 0)
    def _():
        st_ref[...] = h0_ref[...]

    lane = lax.broadcasted_iota(jnp.int32, (r, SSM_WIDTH), 1)
    low = (lane & 128) == 0
    rows2 = lax.broadcasted_iota(jnp.int32, (q * SSM_SUB, SSM_WIDTH), 0)
    lane2 = lax.broadcasted_iota(jnp.int32, (q * SSM_SUB, SSM_WIDTH), 1)
    keep = ((rows2 & 1) == 0) == ((lane2 & 128) == 0)

    def sub_tile(s, carry):
        t0 = pl.multiple_of(s * SSM_SUB, SSM_SUB)
        u = jnp.concatenate([u_refs[j][pl.ds(t0, SSM_SUB), :] for j in range(nseq)], axis=0)
        ub = u.astype(BF16)
        zero = jnp.zeros_like(ub)
        stacked = jnp.concatenate([jnp.where(low, ub, zero), jnp.where(low, zero, ub)], axis=0)
        lall = _dot(pin_ref[...], stacked).astype(BF16)
        for p in range(SSM_PAIRS):
            hb_ref[p] = _dot(lall[:, 256 * p:256 * (p + 1)], bmat_ref[p])
        for p in range(SSM_PAIRS):
            ar = are_ref[p]
            ai = aim_ref[p]

            def step(t, hc, p=p, ar=ar, ai=ai):
                hr, hi = hc
                r0 = pl.multiple_of(t * q, q)
                bur = hb_ref[p, pl.ds(r0, q), 0:hs]
                bui = hb_ref[p, pl.ds(r0, q), hs:2 * hs]
                nr = ar * hr - ai * hi + bur
                ni = ar * hi + ai * hr + bui
                hb_ref[p, pl.ds(r0, q), 0:hs] = nr
                hb_ref[p, pl.ds(r0, q), hs:2 * hs] = ni
                return nr, ni

            hr, hi = lax.fori_loop(0, SSM_SUB, step, (st_ref[p, :, 0:hs], st_ref[p, :, hs:2 * hs]), unroll=True)
            st_ref[p, :, 0:hs] = hr
            st_ref[p, :, hs:2 * hs] = hi
        y2 = jnp.concatenate([_dot(hb_ref[p].astype(BF16), cmat_ref[p]) for p in range(SSM_PAIRS)], axis=1)
        zf = jnp.where(keep, y2, 0.0)
        z_hi = zf.astype(BF16)
        r1 = zf - z_hi.astype(F32)
        z_mid = r1.astype(BF16)
        z_lo = (r1 - z_mid.astype(F32)).astype(BF16)
        pout = pout_ref[...]
        yn = _dot(pout, z_hi) + _dot(pout, z_mid) + _dot(pout, z_lo)
        g = _gelu_tanh(yn + d_ref[...] * u).astype(BF16)
        for j in range(nseq):
            g_ref[pl.ds(pl.multiple_of(j * tt + t0, SSM_SUB), SSM_SUB), :] = g[j * SSM_SUB:(j + 1) * SSM_SUB]
        return carry

    lax.fori_loop(0, tt // SSM_SUB, sub_tile, 0)

    g = g_ref[...]
    ga = _dot(g, wglu_ref[:, :SSM_WIDTH])
    gb = _dot(g, wglu_ref[:, SSM_WIDTH:])
    z = jnp.concatenate([z_refs[j][...] for j in range(nseq)], axis=0)
    out = (ga * jax.nn.sigmoid(gb) * _silu(z)).astype(BF16)
    for j in range(nseq):
        y_ref[j] = out[j * tt:(j + 1) * tt]

    @pl.when(c == pl.num_programs(0) - 1)
    def _():
        ht_ref[...] = st_ref[...]


def _ssm_perms(nseq):
    r, q = nseq * SSM_SUB, 2 * nseq
    pin = np.zeros((2 * r, 2 * r), np.float32)
    pout = np.zeros((r, 2 * r), np.float32)
    for t in range(SSM_SUB):
        for j in range(nseq):
            for h in range(2):
                pin[t * q + 2 * j + h, h * r + j * SSM_SUB + t] = 1.0
                pout[j * SSM_SUB + t, t * q + 2 * j + h] = 1.0
    return jnp.asarray(pin, BF16), jnp.asarray(pout, BF16)


def _ssm(p, d, w_glu, bmat, cmat, a_re, a_im, h0, *, nseq, s, tt, row0, l, h0_layer, name):
    nt = s // tt
    q = 2 * nseq
    rb0 = row0 // tt
    pin, pout = _ssm_perms(nseq)
    seq_spec = lambda j, col: pl.BlockSpec((tt, SSM_WIDTH), lambda c: (rb0 + j * nt + c, col))
    whole = lambda a: pl.BlockSpec(a.shape, lambda c: (0,) * a.ndim)
    layer = lambda a, lay: pl.BlockSpec((SSM_PAIRS,) + a.shape[1:], lambda c: (lay, 0, 0))
    in_specs = ([seq_spec(j, P_U // SSM_WIDTH) for j in range(nseq)]
                + [seq_spec(j, P_ZS // SSM_WIDTH) for j in range(nseq)]
                + [pl.BlockSpec((1, SSM_WIDTH), lambda c: (0, 0)),
                   pl.BlockSpec((SSM_WIDTH, 2 * SSM_WIDTH), lambda c: (l, 0)),
                   layer(bmat, l), layer(cmat, l), layer(a_re, l), layer(a_im, l), layer(h0, h0_layer),
                   whole(pin), whole(pout)])
    return pl.pallas_call(
        functools.partial(_ssm_kernel, nseq=nseq, tt=tt),
        grid=(nt,),
        in_specs=in_specs,
        out_specs=[pl.BlockSpec((nseq, tt, SSM_WIDTH), lambda c: (0, c, 0)),
                   pl.BlockSpec((SSM_PAIRS, q, 2 * SSM_HALF_STATE), lambda c: (0, 0, 0))],
        out_shape=[jax.ShapeDtypeStruct((nseq, s, SSM_WIDTH), BF16),
                   jax.ShapeDtypeStruct((SSM_PAIRS, q, 2 * SSM_HALF_STATE), F32)],
        scratch_shapes=[pltpu.VMEM((SSM_PAIRS, q, 2 * SSM_HALF_STATE), F32),
                        pltpu.VMEM((SSM_PAIRS, q * SSM_SUB, 2 * SSM_HALF_STATE), F32),
                        pltpu.VMEM((nseq * tt, SSM_WIDTH), BF16)],
        compiler_params=_params(("arbitrary",)),
        name=name,
    )(*([p] * (2 * nseq)), d, w_glu, bmat, cmat, a_re, a_im, h0, pin, pout)


def _ssm_state_in(h_re, h_im):
    nl, nseq = h_re.shape[:2]

    def arr(h):
        return h.reshape(nl, nseq, SSM_PAIRS, 2, SSM_HALF_STATE).transpose(0, 2, 1, 3, 4).reshape(
            nl * SSM_PAIRS, nseq * 2, SSM_HALF_STATE)
    return jnp.concatenate([arr(h_re), arr(h_im)], axis=-1)


def _ssm_state_out(ht, nseq):
    nl = ht.shape[0]

    def arr(h):
        return h.reshape(nl, SSM_PAIRS, nseq, 2, 8, SSM_STATE).transpose(0, 2, 1, 3, 4, 5).reshape(
            nl, nseq, SSM_GROUPS, SSM_STATE)
    return arr(ht[..., :SSM_HALF_STATE]), arr(ht[..., SSM_HALF_STATE:])


def _ssm_mats(bb_re_t, bb_im_t, c_re, c_im):
    eye8 = jnp.eye(8, dtype=F32)

    def bmat(bb):
        b6 = bb.reshape(-1, 2, 8, SSM_GROUP, SSM_STATE)
        return jnp.einsum('ahgcp,gk->ahgckp', b6, eye8).reshape(-1, 256, SSM_HALF_STATE)

    def cmat(cc):
        c6 = cc.reshape(-1, 2, 8, SSM_GROUP, SSM_STATE)
        return jnp.einsum('ahgcp,gk->akphgc', c6, eye8).reshape(-1, SSM_HALF_STATE, 256)

    b_all = jnp.concatenate([bmat(bb_re_t), bmat(bb_im_t)], axis=2).astype(BF16)
    c_all = jnp.concatenate([cmat(c_re), cmat(-c_im)], axis=1).astype(BF16)
    return b_all, c_all


def _ssm_a_rows(a, nseq):
    a3 = a.reshape(-1, 1, 2, SSM_HALF_STATE)
    return jnp.broadcast_to(a3, (a3.shape[0], nseq, 2, SSM_HALF_STATE)).reshape(
        a3.shape[0], nseq * 2, SSM_HALF_STATE)


OUT_TM = 256


OUT_PROMPT_TILES = N_PROMPT // OUT_TM


def _merge_out_kernel(x_ref, ysp_ref, yss_ref, yap_ref, yas_ref, ymp_ref, yms_ref, g0_ref, g1_ref, g2_ref,
                      ws_ref, wa_ref, wm_ref, wo_ref, gp_ref, *o_refs):
    i = pl.program_id(0)
    pick = lambda prompt_ref, sample_ref: jnp.where(i < OUT_PROMPT_TILES, prompt_ref[...], sample_ref[...])
    merged = (jax.nn.sigmoid(g0_ref[...]) * _dot(pick(ysp_ref, yss_ref), ws_ref[...])
              + jax.nn.sigmoid(g1_ref[...]) * _dot(pick(yap_ref, yas_ref), wa_ref[...])
              + jax.nn.sigmoid(g2_ref[...]) * _dot(pick(ymp_ref, yms_ref), wm_ref[...]))
    out = _dot(merged.astype(BF16), wo_ref[...])
    y = x_ref[...] + _rms(out, gp_ref[...])
    if len(o_refs) == 1:
        o_refs[0][...] = y
    else:
        @pl.when(i < OUT_PROMPT_TILES)
        def _():
            o_refs[0][...] = y

        @pl.when(i == OUT_PROMPT_TILES)
        def _():
            o_refs[1][...] = y


def _merge_out(x, y_ssm, y_mla, y_mem, p, w_ssm_o, w_mla_o, w_mem_o, w_out, g_post, l, split):
    tm = OUT_TM
    assert N_SAMPLE == tm
    row = lambda c: (lambda i: (i, c))
    const = lambda i: (0, 0)
    prompt_row = lambda i: (jnp.minimum(i, OUT_PROMPT_TILES - 1), 0)
    resident = lambda shape: pl.BlockSpec(shape, lambda i: (l, 0), pipeline_mode=pl.Buffered(1))
    branch = lambda width: [pl.BlockSpec((tm, width), prompt_row), pl.BlockSpec((tm, width), const)]
    if split:
        out_specs = [pl.BlockSpec((tm, D_MODEL), prompt_row),
                     pl.BlockSpec((tm, D_MODEL), const)]
        out_shape = [jax.ShapeDtypeStruct((N_PROMPT, D_MODEL), F32),
                     jax.ShapeDtypeStruct((N_SAMPLE, D_MODEL), F32)]
    else:
        out_specs = pl.BlockSpec((tm, D_MODEL), row(0))
        out_shape = jax.ShapeDtypeStruct((N_TOK, D_MODEL), F32)
    return pl.pallas_call(
        _merge_out_kernel,
        grid=(N_TOK // tm,),
        in_specs=[pl.BlockSpec((tm, D_MODEL), row(0)),
                  *branch(SSM_WIDTH), *branch(MLA_WIDTH), *branch(MEM_WIDTH),
                  pl.BlockSpec((tm, D_MODEL), row(0)),
                  pl.BlockSpec((tm, D_MODEL), row(1)),
                  pl.BlockSpec((tm, D_MODEL), row(2)),
                  resident((SSM_WIDTH, D_MODEL)),
                  resident((MLA_WIDTH, D_MODEL)),
                  resident((MEM_WIDTH, D_MODEL)),
                  resident((D_MODEL, D_MODEL)),
                  pl.BlockSpec((1, D_MODEL), const)],
        out_specs=out_specs,
        out_shape=out_shape,
        compiler_params=_params(("arbitrary",)),
        name="merge_out",
    )(x, *y_ssm, *y_mla, *y_mem, p, p, p, w_ssm_o, w_mla_o, w_mem_o, w_out, g_post)


def _rot_half_cols(w):
    half = w.shape[-1] // 2
    return jnp.concatenate([w[..., half:], w[..., :half]], axis=-1)


def _stack_rows(w):
    return w.astype(BF16).reshape(w.shape[0] * w.shape[1], w.shape[2])


def _prep_small_weights(w_in_t, w_uq, w_uk, w_uv):
    wk = jnp.swapaxes(w_in_t[:, W_IN_K_ROPE:W_IN_K_ROPE + QK_ROPE, :], 1, 2)
    w_kr = _stack_rows(jnp.concatenate([wk, _rot_half_cols(wk)], axis=-1))
    uq = w_uq.reshape(DEPTH, Q_LORA, MLA_HEADS, QK_NOPE + QK_ROPE)
    uq_nope = uq[..., :QK_NOPE].reshape(DEPTH, Q_LORA, MLA_WIDTH)
    uq_rope = uq[..., QK_NOPE:]
    uq_rr = jnp.concatenate([uq_rope, _rot_half_cols(uq_rope)], axis=-1).reshape(DEPTH, Q_LORA, MLA_WIDTH)
    w_uq2 = jnp.concatenate([uq_nope, uq_rr], axis=-1)
    w_ukv = jnp.concatenate([w_uk, w_uv], axis=-1)
    return w_kr, _stack_rows(w_uq2), _stack_rows(w_ukv)


def _rope_table():
    half = QK_ROPE // 2
    pos = jnp.concatenate([jnp.tile(jnp.arange(SEQ, dtype=jnp.int32), BATCH),
                           jnp.tile(PAST_LEN + jnp.arange(DEC_SEQ, dtype=jnp.int32), DEC_BATCH)])
    inv = ROPE_THETA ** (-jnp.arange(half, dtype=F32) / half)
    ang = pos.astype(F32)[:, None] * inv[None, :]
    cos, sin = jnp.cos(ang), jnp.sin(ang)
    return jnp.concatenate([cos, cos, -sin, sin], axis=-1)


def kernel(x_prompt, x_sample, cache_mla_ckv, cache_mla_krope, cache_mem_k, cache_mem_v, state_ssm_re, state_ssm_im, mem_prompt, norm_pre, w_in, ssm_a_re, ssm_a_im, ssm_log_dt, ssm_b_re, ssm_b_im, ssm_c_re, ssm_c_im, ssm_d, w_glu, mla_q_norm, w_uq, mla_kv_norm, w_uk, w_uv, w_mem_k, w_mem_v, w_ssm_o, w_mla_o, w_mem_o, w_out, norm_post):
    w_in_t = jnp.swapaxes(w_in, 1, 2)
    w_in_rows = w_in_t.reshape(DEPTH * W_IN_COLS, D_MODEL)
    w_kr, w_uq2, w_ukv = _prep_small_weights(w_in_t, w_uq, w_uk, w_uv)
    w_glu_b, w_ssm_o_b, w_mla_o_b, w_mem_o_b, w_out_b = (
        _stack_rows(w) for w in (w_glu, w_ssm_o, w_mla_o, w_mem_o, w_out))
    w_mem_kv = _stack_rows(jnp.concatenate([w_mem_k, w_mem_v], axis=-1))
    cs = _rope_table()
    mem_rows = mem_prompt.reshape(BATCH * N_MEM, D_MODEL)
    ckv_cache = cache_mla_ckv.reshape(DEPTH * DEC_BATCH * PAST_LEN, KV_LORA)
    krc_t = jnp.swapaxes(cache_mla_krope, 2, 3).reshape(DEPTH * DEC_BATCH * QK_ROPE, PAST_LEN)
    mem_k_cache = cache_mem_k.reshape(DEPTH * DEC_BATCH * N_MEM, MEM_WIDTH)
    mem_v_cache = cache_mem_v.reshape(DEPTH * DEC_BATCH * N_MEM, MEM_WIDTH)

    zeros_state = jnp.zeros((BATCH, SSM_GROUPS, SSM_STATE), F32)
    x = jnp.concatenate([x_prompt.reshape(N_PROMPT, D_MODEL), x_sample.reshape(N_SAMPLE, D_MODEL)], axis=0)

    mk_all, mv_all = _mem_kv(mem_rows, w_mem_kv)

    lg = DEPTH * SSM_GROUPS
    flat = lambda a: a.reshape((lg,) + a.shape[2:])
    ab_re, ab_im, bb_re_t, bb_im_t = _s5_discretize(
        flat(ssm_a_re), flat(ssm_a_im), ssm_log_dt.reshape(lg),
        flat(jnp.swapaxes(ssm_b_re, 2, 3)), flat(jnp.swapaxes(ssm_b_im, 2, 3)))
    bmat, cmat = _ssm_mats(bb_re_t, bb_im_t, flat(ssm_c_re), flat(ssm_c_im))
    a_rows_p = (_ssm_a_rows(ab_re, BATCH), _ssm_a_rows(ab_im, BATCH))
    a_rows_s = (_ssm_a_rows(ab_re, DEC_BATCH), _ssm_a_rows(ab_im, DEC_BATCH))
    h0_p = _ssm_state_in(zeros_state[None], zeros_state[None])
    h0_s = _ssm_state_in(state_ssm_re, state_ssm_im)

    ht_p, ht_s = [], []
    stacked = None
    for l in range(DEPTH):
        p, kr = _in_proj(x, norm_pre[l][None], w_in_rows, w_kr, l)

        ys_p, ht = _ssm(p, ssm_d[l][None], w_glu_b, bmat, cmat, *a_rows_p, h0_p,
                        nseq=BATCH, s=SEQ, tt=128, row0=0, l=l, h0_layer=0, name="ssm_prompt")
        ht_p.append(ht)
        ys_s, ht = _ssm(p, ssm_d[l][None], w_glu_b, bmat, cmat, *a_rows_s, h0_s,
                        nseq=DEC_BATCH, s=DEC_SEQ, tt=DEC_SEQ, row0=N_PROMPT, l=l, h0_layer=l, name="ssm_sample")
        ht_s.append(ht)
        y_ssm = (ys_p.reshape(N_PROMPT, SSM_WIDTH), ys_s.reshape(N_SAMPLE, SSM_WIDTH))

        q, k, v, krp, *stacked = _mla_prep(p, kr, cs, mla_q_norm[l][None], mla_kv_norm[l][None],
                                           w_uq2, w_ukv, l, stacked)
        y_mla = (_attn_prompt(q, k, krp, v, p),
                 _attn_sample(q, ckv_cache, krc_t, stacked[1], krp, w_ukv, p, l))

        y_mem = (_mem_attn(p, mk_all, mv_all, nb=BATCH, seq=SEQ, tq=1024, row0=0, mem_blk0=l * BATCH,
                           name="mem_attn_prompt"),
                 _mem_attn(p, mem_k_cache, mem_v_cache, nb=DEC_BATCH, seq=DEC_SEQ, tq=DEC_SEQ, row0=N_PROMPT,
                           mem_blk0=l * DEC_BATCH, name="mem_attn_sample"))

        x = _merge_out(x, y_ssm, y_mla, y_mem, p, w_ssm_o_b, w_mla_o_b, w_mem_o_b, w_out_b,
                       norm_post[l][None], l, split=l == DEPTH - 1)

    x_p, x_s = x
    ckv_p, ckv_s, kro_p, kro_s = stacked
    p_re, p_im = _ssm_state_out(jnp.stack(ht_p), BATCH)
    s_re, s_im = _ssm_state_out(jnp.stack(ht_s), DEC_BATCH)
    mem_shape = (DEPTH, BATCH, N_MEM, MEM_HEADS, MEM_HEAD_DIM)
    return (x_p.reshape(BATCH, SEQ, D_MODEL), x_s.reshape(DEC_BATCH, DEC_SEQ, D_MODEL),
            p_re, p_im,
            ckv_p.reshape(DEPTH, BATCH, SEQ, KV_LORA), kro_p.reshape(DEPTH, BATCH, SEQ, QK_ROPE),
            mk_all.reshape(mem_shape), mv_all.reshape(mem_shape),
            s_re, s_im,
            ckv_s.reshape(DEPTH, DEC_BATCH, DEC_SEQ, KV_LORA), kro_s.reshape(DEPTH, DEC_BATCH, DEC_SEQ, QK_ROPE))
```

```python
import functools
import math

import jax
import jax.numpy as jnp
import numpy as np
from jax import lax
from jax.experimental import pallas as pl
from jax.experimental.pallas import tpu as pltpu

D_MODEL = 2048
BATCH = 4
SEQ = 2048
DEPTH = 4
DEC_BATCH = 8
DEC_SEQ = 32
PAST_LEN = 2048
CHUNK = 64
N_MEM = 256
EPS = 1e-6
NEG_INF = -1e30
SSM_WIDTH = 1024
SSM_GROUP = 16
SSM_GROUPS = 64
SSM_STATE = 64
MLA_HEADS = 8
QK_NOPE = 128
QK_ROPE = 64
V_HEAD = 128
Q_LORA = 512
KV_LORA = 512
MLA_WIDTH = 1024
ROPE_THETA = 10000.0
MEM_HEADS = 4
MEM_HEAD_DIM = 128
MEM_WIDTH = 512

N_PROMPT = BATCH * SEQ
N_SAMPLE = DEC_BATCH * DEC_SEQ
N_TOK = N_PROMPT + N_SAMPLE

P_GATE = 0
P_U = 6144
P_ZS = 7168
P_CQ = 8192
P_CKV = 8704
P_ZA = 9216
P_QM = 10240
P_ZM = 10752
P_COLS = 11264
W_IN_COLS = 11328

LOG2E = math.log2(math.e)
MLA_SCALE = (QK_NOPE + QK_ROPE) ** -0.5 * LOG2E
MEM_SCALE = MEM_HEAD_DIM ** -0.5 * LOG2E

V7X_VMEM_LIMIT = 52 * 1024 * 1024

BF16 = jnp.bfloat16
F32 = jnp.float32


def _params(sem, vmem=V7X_VMEM_LIMIT):
    return pltpu.CompilerParams(dimension_semantics=sem, vmem_limit_bytes=vmem)


def _dot(a, b):
    return jnp.dot(a, b, preferred_element_type=F32)


def _dot_nt(a, b):
    return lax.dot_general(a, b, (((1,), (1,)), ((), ())), preferred_element_type=F32)


def _rms(x, g):
    return x * lax.rsqrt(jnp.mean(x * x, axis=-1, keepdims=True) + EPS) * g


def _silu(x):
    return x * jax.nn.sigmoid(x)


def _gelu_tanh(x):
    return 0.5 * x * (1.0 + jnp.tanh(math.sqrt(2.0 / math.pi) * (x + 0.044715 * (x * x * x))))


IN_TM = 1408
IN_TN = 512

W_IN_K_ROPE = 2 * SSM_WIDTH + Q_LORA + KV_LORA
_W_GATE0 = W_IN_K_ROPE + QK_ROPE + MLA_WIDTH + 2 * MEM_WIDTH
_W_UNIT = 64


def _w_src_row(l, j):
    u = lambda c: c // _W_UNIT
    assert all(c % _W_UNIT == 0 for c in (_W_GATE0, IN_TN, W_IN_K_ROPE + QK_ROPE, W_IN_COLS))
    n_gate, n_front = P_U // IN_TN, P_ZA // IN_TN
    start = jnp.where(j < n_gate, u(_W_GATE0) + u(IN_TN) * j,
                      jnp.where(j < n_front, u(IN_TN) * (j - n_gate),
                                u(W_IN_K_ROPE + QK_ROPE) + u(IN_TN) * (j - n_front)))
    return (l * u(W_IN_COLS) + start) * _W_UNIT


IN_TILES = N_TOK // IN_TM
IN_LAST_PROMPT = N_PROMPT - (IN_TILES - 1) * IN_TM


def _in_proj_kernel(*refs, split_x):
    x_refs, (g_ref, wt_ref, wkr_ref, p_ref, kr_ref, h_ref) = refs[:len(refs) - 6], refs[len(refs) - 6:]

    @pl.when(pl.program_id(1) == 0)
    def _():
        if split_x:
            xp_ref, xs_ref = x_refs
            last = pl.program_id(0) == IN_TILES - 1

            @pl.when(jnp.logical_not(last))
            def _():
                h_ref[...] = _rms(xp_ref[...], g_ref[...]).astype(BF16)

            @pl.when(last)
            def _():
                h_ref[:IN_LAST_PROMPT] = _rms(xp_ref[IN_TM - IN_LAST_PROMPT:], g_ref[...]).astype(BF16)
                h_ref[IN_LAST_PROMPT:] = _rms(xs_ref[...], g_ref[...]).astype(BF16)
        else:
            h_ref[...] = _rms(x_refs[0][...], g_ref[...]).astype(BF16)
        kr_ref[...] = _dot(h_ref[...], wkr_ref[...])

    p_ref[...] = _dot_nt(h_ref[...], wt_ref[...].astype(BF16))


def _in_proj(x, g_pre, w_in_t, wkr, l):
    split_x = isinstance(x, tuple)
    if split_x:
        assert IN_TM - IN_LAST_PROMPT == N_SAMPLE and IN_TM % 64 == 0 and (N_PROMPT - IN_TM) % 64 == 0
        clamp = lambda i: jnp.minimum(i * (IN_TM // 64), (N_PROMPT - IN_TM) // 64) * 64
        x_specs = [pl.BlockSpec((pl.Element(IN_TM), pl.Element(D_MODEL)), lambda i, j: (clamp(i), 0)),
                   pl.BlockSpec((N_SAMPLE, D_MODEL), lambda i, j: (0, 0))]
        x_args = list(x)
    else:
        x_specs = [pl.BlockSpec((IN_TM, D_MODEL), lambda i, j: (i, 0))]
        x_args = [x]
    return pl.pallas_call(
        functools.partial(_in_proj_kernel, split_x=split_x),
        grid=(IN_TILES, P_COLS // IN_TN),
        in_specs=x_specs + [
            pl.BlockSpec((1, D_MODEL), lambda i, j: (0, 0)),
            pl.BlockSpec((pl.Element(IN_TN), pl.Element(D_MODEL)), lambda i, j: (_w_src_row(l, j), 0)),
            pl.BlockSpec((D_MODEL, 128), lambda i, j: (l, 0)),
        ],
        out_specs=[
            pl.BlockSpec((IN_TM, IN_TN), lambda i, j: (i, j)),
            pl.BlockSpec((IN_TM, 128), lambda i, j: (i, 0)),
        ],
        out_shape=[
            jax.ShapeDtypeStruct((N_TOK, P_COLS), F32),
            jax.ShapeDtypeStruct((N_TOK, 128), F32),
        ],
        scratch_shapes=[pltpu.VMEM((IN_TM, D_MODEL), BF16)],
        compiler_params=_params(("parallel", "arbitrary")),
        name="in_proj",
    )(*x_args, g_pre, w_in_t, wkr)


PREP_TM = 256
PREP_PROMPT_TILES = N_PROMPT // PREP_TM


def _mla_prep_kernel(*refs, n_alias):
    (cq_ref, ckv_ref, kr_ref, cs_ref, qg_ref, kvg_ref, wuq_ref, wukv_ref,
     q_ref, k_ref, v_ref, krp_ref, ckvp_ref, ckvs_ref, krop_ref, kros_ref) = refs[n_alias:]
    i = pl.program_id(0)
    d = 128
    cs = cs_ref[...]
    hq = _rms(cq_ref[...], qg_ref[...]).astype(BF16)
    q = _dot(hq, wuq_ref[...])

    ckv = _rms(ckv_ref[...], kvg_ref[...])
    kv = _dot(ckv.astype(BF16), wukv_ref[...])
    k_ref[...] = kv[:, :MLA_WIDTH].astype(BF16)
    v_ref[...] = kv[:, MLA_WIDTH:].astype(BF16)

    tk = kr_ref[...] * cs
    rk = tk + pltpu.roll(tk, 64, axis=1)
    lane = lax.broadcasted_iota(jnp.int32, rk.shape, 1)
    krp = jnp.where(lane < QK_ROPE, rk, 0.0).astype(BF16)
    krp_ref[...] = krp

    for h in range(MLA_HEADS):
        t = q[:, MLA_WIDTH + d * h:MLA_WIDTH + d * (h + 1)] * cs
        q_ref[:, 2 * d * h:2 * d * h + d] = (q[:, d * h:d * (h + 1)] * MLA_SCALE).astype(BF16)
        q_ref[:, 2 * d * h + d:2 * d * (h + 1)] = ((t + pltpu.roll(t, 64, axis=1)) * MLA_SCALE).astype(BF16)

    @pl.when(i < PREP_PROMPT_TILES)
    def _():
        ckvp_ref[...] = ckv
        krop_ref[...] = rk[:, :QK_ROPE]

    @pl.when(i == PREP_PROMPT_TILES)
    def _():
        ckvs_ref[...] = ckv
        kros_ref[...] = rk[:, :QK_ROPE]


def _mla_prep(p, kr, cs, qg, kvg, wuq, wukv, l, stacked):
    tm = PREP_TM
    row = lambda c: (lambda i: (i, c))
    const = lambda i: (0, 0)
    layer = lambda i: (l, 0)
    prompt_blk = lambda i: (l * PREP_PROMPT_TILES + jnp.minimum(i, PREP_PROMPT_TILES - 1), 0)
    n_alias = 0 if stacked is None else 4
    in_specs = [
        pl.BlockSpec((tm, Q_LORA), row(P_CQ // Q_LORA)),
        pl.BlockSpec((tm, KV_LORA), row(P_CKV // KV_LORA)),
        pl.BlockSpec((tm, 128), row(0)),
        pl.BlockSpec((tm, 128), row(0)),
        pl.BlockSpec((1, Q_LORA), const),
        pl.BlockSpec((1, KV_LORA), const),
        pl.BlockSpec((Q_LORA, 2048), layer),
        pl.BlockSpec((KV_LORA, 2048), layer),
    ]
    args = [p, p, kr, cs, qg, kvg, wuq, wukv]
    if stacked is not None:
        in_specs = [pl.BlockSpec(memory_space=pl.ANY)] * 4 + in_specs
        args = list(stacked) + args
    return pl.pallas_call(
        functools.partial(_mla_prep_kernel, n_alias=n_alias),
        grid=(N_TOK // tm,),
        in_specs=in_specs,
        out_specs=[
            pl.BlockSpec((tm, 2 * MLA_WIDTH), row(0)),
            pl.BlockSpec((tm, MLA_WIDTH), row(0)),
            pl.BlockSpec((tm, MLA_WIDTH), row(0)),
            pl.BlockSpec((tm, 128), row(0)),
            pl.BlockSpec((tm, KV_LORA), prompt_blk),
            pl.BlockSpec((tm, KV_LORA), layer),
            pl.BlockSpec((tm, QK_ROPE), prompt_blk),
            pl.BlockSpec((tm, QK_ROPE), layer),
        ],
        out_shape=[
            jax.ShapeDtypeStruct((N_TOK, 2 * MLA_WIDTH), BF16),
            jax.ShapeDtypeStruct((N_TOK, MLA_WIDTH), BF16),
            jax.ShapeDtypeStruct((N_TOK, MLA_WIDTH), BF16),
            jax.ShapeDtypeStruct((N_TOK, 128), BF16),
            jax.ShapeDtypeStruct((DEPTH * N_PROMPT, KV_LORA), F32),
            jax.ShapeDtypeStruct((DEPTH * N_SAMPLE, KV_LORA), F32),
            jax.ShapeDtypeStruct((DEPTH * N_PROMPT, QK_ROPE), F32),
            jax.ShapeDtypeStruct((DEPTH * N_SAMPLE, QK_ROPE), F32),
        ],
        input_output_aliases={k: 4 + k for k in range(n_alias)},
        compiler_params=_params(("arbitrary",)),
        name="mla_prep",
    )(*args)


def _mm_kernel(x_ref, w_ref, *o_refs):
    y = _dot(x_ref[...].astype(BF16), w_ref[...])
    off = 0
    for o_ref in o_refs:
        n = o_ref.shape[1]
        o_ref[...] = y[:, off:off + n].astype(o_ref.dtype)
        off += n


def _mem_kv(mem_rows, w_mem_kv):
    tm = 512
    nr = BATCH * N_MEM // tm
    out = pl.BlockSpec((tm, MEM_WIDTH), lambda l, i: (l * nr + i, 0))
    return pl.pallas_call(
        _mm_kernel,
        grid=(DEPTH, nr),
        in_specs=[pl.BlockSpec((tm, D_MODEL), lambda l, i: (i, 0)),
                  pl.BlockSpec((D_MODEL, 2 * MEM_WIDTH), lambda l, i: (l, 0))],
        out_specs=[out, out],
        out_shape=[jax.ShapeDtypeStruct((DEPTH * BATCH * N_MEM, MEM_WIDTH), F32)] * 2,
        compiler_params=_params(("parallel", "parallel")),
        name="mem_kv",
    )(mem_rows, w_mem_kv)


ATT_TQ = 512


def _chunk_mask(q_pos0, k_pos0, nq, nk):
    qp = q_pos0 + lax.broadcasted_iota(jnp.int32, (nq, nk), 0)
    kp = k_pos0 + lax.broadcasted_iota(jnp.int32, (nq, nk), 1)
    sh = CHUNK.bit_length() - 1
    return lax.shift_right_logical(kp, sh) <= lax.shift_right_logical(qp, sh)


ATT_HEADS = 4


def _attn_prompt_kernel(q_ref, kn_ref, kr_ref, v_ref, z_ref, o_ref, k_scr, v_scr):
    tq, d = ATT_TQ, 128
    for g in range(ATT_HEADS):
        k_scr[g, :, :d] = kn_ref[:, d * g:d * (g + 1)]
        k_scr[g, :, d:] = kr_ref[...]
        v_scr[g, :, :d] = v_ref[:, d * g:d * (g + 1)]
        v_scr[g, :, d:] = jnp.ones((SEQ, d), BF16)
    for qi in range(SEQ // tq):
        lo, hi = qi * tq, (qi + 1) * tq
        for g in range(ATT_HEADS):
            q = q_ref[lo:hi, 2 * d * g:2 * d * (g + 1)]
            sd = jnp.where(_chunk_mask(lo, lo, tq, tq), _dot_nt(q, k_scr[g, lo:hi, :]), NEG_INF)
            m = jnp.max(sd, axis=1, keepdims=True)
            if qi > 0:
                so = _dot_nt(q, k_scr[g, 0:lo, :])
                m = jnp.maximum(m, jnp.max(so, axis=1, keepdims=True))
            acc = _dot(jnp.exp2(sd - m).astype(BF16), v_scr[g, lo:hi, :])
            if qi > 0:
                acc = acc + _dot(jnp.exp2(so - m).astype(BF16), v_scr[g, 0:lo, :])
            z = z_ref[lo:hi, d * g:d * (g + 1)]
            o_ref[lo:hi, d * g:d * (g + 1)] = (acc[:, :d] * (1.0 / acc[:, d:]) * _silu(z)).astype(BF16)


def _attn_prompt(q, kn, krp, v, p):
    w = 128 * ATT_HEADS
    blk = lambda c0: pl.BlockSpec((SEQ, w), lambda b, h: (b, c0 + h))
    return pl.pallas_call(
        _attn_prompt_kernel,
        grid=(BATCH, MLA_HEADS // ATT_HEADS),
        in_specs=[pl.BlockSpec((SEQ, 2 * w), lambda b, h: (b, h)),
                  blk(0), pl.BlockSpec((SEQ, 128), lambda b, h: (b, 0)), blk(0), blk(P_ZA // w)],
        out_specs=blk(0),
        out_shape=jax.ShapeDtypeStruct((N_PROMPT, MLA_WIDTH), BF16),
        scratch_shapes=[pltpu.VMEM((ATT_HEADS, SEQ, 256), BF16), pltpu.VMEM((ATT_HEADS, SEQ, 256), BF16)],
        compiler_params=_params(("parallel", "parallel")),
        name="attn_prompt",
    )(q, kn, krp, v, p)


def _attn_sample_kernel(q_ref, ckvc_ref, krct_ref, ckvn_ref, krn_ref, wukv_ref, z_ref, o_ref):
    nh, d = MLA_HEADS, 128
    qa = jnp.concatenate(
        [_dot_nt(q_ref[:, 2 * d * h:2 * d * h + d], wukv_ref[:, d * h:d * (h + 1)]) for h in range(nh)],
        axis=0).astype(BF16)
    qr = jnp.concatenate([q_ref[:, 2 * d * h + d:2 * d * (h + 1)] for h in range(nh)], axis=0)
    ckvc = ckvc_ref[...].astype(BF16)
    krct = krct_ref[...].astype(BF16)
    ckvn = ckvn_ref[...].astype(BF16)
    rows = nh * DEC_SEQ

    def mask(k_pos0, nk):
        qp = PAST_LEN + (lax.broadcasted_iota(jnp.int32, (rows, nk), 0) & (DEC_SEQ - 1))
        kp = k_pos0 + lax.broadcasted_iota(jnp.int32, (rows, nk), 1)
        sh = CHUNK.bit_length() - 1
        return lax.shift_right_logical(kp, sh) <= lax.shift_right_logical(qp, sh)

    sp = jnp.where(mask(0, PAST_LEN), _dot_nt(qa, ckvc) + _dot(qr[:, :QK_ROPE], krct), NEG_INF)
    sn = jnp.where(mask(PAST_LEN, DEC_SEQ), _dot_nt(qa, ckvn) + _dot_nt(qr, krn_ref[...]), NEG_INF)
    m = jnp.maximum(jnp.max(sp, axis=1, keepdims=True), jnp.max(sn, axis=1, keepdims=True))
    ep, en = jnp.exp2(sp - m), jnp.exp2(sn - m)
    l = jnp.sum(ep, axis=1, keepdims=True) + jnp.sum(en, axis=1, keepdims=True)
    lat = ((_dot(ep.astype(BF16), ckvc) + _dot(en.astype(BF16), ckvn)) * (1.0 / l)).astype(BF16)
    o = jnp.concatenate(
        [_dot(lat[DEC_SEQ * h:DEC_SEQ * (h + 1)], wukv_ref[:, MLA_WIDTH + d * h:MLA_WIDTH + d * (h + 1)])
         for h in range(nh)], axis=1)
    o_ref[...] = (o * _silu(z_ref[...])).astype(BF16)


def _attn_sample(q, ckv_cache, krc_t, ckv_s, krp, wukv, p, l):
    assert DEC_SEQ & (DEC_SEQ - 1) == 0
    r0 = N_PROMPT // DEC_SEQ
    new = lambda width, c: pl.BlockSpec((DEC_SEQ, width), lambda b: (r0 + b, c))
    stream = lambda rows, width: pl.BlockSpec((rows, width), lambda b: (l * DEC_BATCH + b, 0))
    return pl.pallas_call(
        _attn_sample_kernel,
        grid=(DEC_BATCH,),
        in_specs=[new(2 * MLA_WIDTH, 0),
                  stream(PAST_LEN, KV_LORA), stream(QK_ROPE, PAST_LEN),
                  stream(DEC_SEQ, KV_LORA),
                  new(128, 0),
                  pl.BlockSpec((KV_LORA, 2048), lambda b: (l, 0)),
                  new(MLA_WIDTH, P_ZA // MLA_WIDTH)],
        out_specs=pl.BlockSpec((DEC_SEQ, MLA_WIDTH), lambda b: (b, 0)),
        out_shape=jax.ShapeDtypeStruct((N_SAMPLE, MLA_WIDTH), BF16),
        compiler_params=_params(("parallel",)),
        name="attn_sample",
    )(q, ckv_cache, krc_t, ckv_s, krp, wukv, p)


def _mem_attn_kernel(q_ref, z_ref, k_ref, v_ref, o_ref):
    q = q_ref[...] * MEM_SCALE
    outs = []
    for h in range(MEM_HEADS):
        sl = slice(MEM_HEAD_DIM * h, MEM_HEAD_DIM * (h + 1))
        s = _dot_nt(q[:, sl].astype(BF16), k_ref[:, sl].astype(BF16))
        m = jnp.max(s, axis=1, keepdims=True)
        e = jnp.exp2(s - m)
        l = jnp.sum(e, axis=1, keepdims=True)
        outs.append(_dot(e.astype(BF16), v_ref[:, sl].astype(BF16)) * (1.0 / l))
    o_ref[...] = (jnp.concatenate(outs, axis=1) * _silu(z_ref[...])).astype(BF16)


def _mem_attn(p, mem_k, mem_v, *, nb, seq, tq, row0, mem_blk0, name):
    seq_blocks = seq // tq
    r0 = row0 // tq
    rowmap = lambda c: (lambda b, i: (r0 + b * seq_blocks + i, c))
    mem = pl.BlockSpec((N_MEM, MEM_WIDTH), lambda b, i: (mem_blk0 + b, 0))
    return pl.pallas_call(
        _mem_attn_kernel,
        grid=(nb, seq_blocks),
        in_specs=[pl.BlockSpec((tq, MEM_WIDTH), rowmap(P_QM // MEM_WIDTH)),
                  pl.BlockSpec((tq, MEM_WIDTH), rowmap(P_ZM // MEM_WIDTH)),
                  mem, mem],
        out_specs=pl.BlockSpec((tq, MEM_WIDTH), lambda b, i: (b * seq_blocks + i, 0)),
        out_shape=jax.ShapeDtypeStruct((nb * seq, MEM_WIDTH), BF16),
        compiler_params=_params(("parallel", "parallel")),
        name=name,
    )(p, p, mem_k, mem_v)


def _s5_disc_kernel(are_ref, aim_ref, ldt_ref, bre_ref, bim_ref, abr_ref, abi_ref, bbr_ref, bbi_ref):
    dt = jnp.exp(ldt_ref[...])
    lr, li = are_ref[...], aim_ref[...]
    mag = jnp.exp(lr * dt)
    ab_re, ab_im = mag * jnp.cos(li * dt), mag * jnp.sin(li * dt)
    den = lr * lr + li * li
    nr, ni = ab_re - 1.0, ab_im
    f_re = (nr * lr + ni * li) / den
    f_im = (ni * lr - nr * li) / den
    abr_ref[...] = ab_re
    abi_ref[...] = ab_im
    br, bi = bre_ref[...], bim_ref[...]
    bbr_ref[...] = f_re * br - f_im * bi
    bbi_ref[...] = f_re * bi + f_im * br


def _s5_discretize(a_re, a_im, log_dt, b_re_t, b_im_t):
    G, P, C = a_re.shape[0], SSM_STATE, SSM_GROUP
    ab_re, ab_im, bb_re_t, bb_im_t = pl.pallas_call(
        _s5_disc_kernel,
        out_shape=[jax.ShapeDtypeStruct((G, 1, P), F32), jax.ShapeDtypeStruct((G, 1, P), F32),
                   jax.ShapeDtypeStruct((G, C, P), F32), jax.ShapeDtypeStruct((G, C, P), F32)],
        name="s5_discretize",
    )(a_re.reshape(G, 1, P), a_im.reshape(G, 1, P), log_dt.reshape(G, 1, 1), b_re_t, b_im_t)
    return ab_re.reshape(G, P), ab_im.reshape(G, P), bb_re_t, bb_im_t


SSM_PAIRS = 4
SSM_HALF_STATE = 512
SSM_SUB = 32


def _ssm_kernel(*refs, nseq, tt):
    u_refs, z_refs = refs[:nseq], refs[nseq:2 * nseq]
    (d_ref, wglu_ref, bmat_ref, cmat_ref, are_ref, aim_ref, h0_ref, pin_ref, pout_ref,
     y_ref, ht_ref, st_ref, hb_ref, g_ref) = refs[2 * nseq:]
    q = 2 * nseq
    r = nseq * SSM_SUB
    hs = SSM_HALF_STATE
    c = pl.program_id(0)

    @pl.when(c == 0)
    def _():
        st_ref[...] = h0_ref[...]

    lane = lax.broadcasted_iota(jnp.int32, (r, SSM_WIDTH), 1)
    low = (lane & 128) == 0
    rows2 = lax.broadcasted_iota(jnp.int32, (q * SSM_SUB, SSM_WIDTH), 0)
    lane2 = lax.broadcasted_iota(jnp.int32, (q * SSM_SUB, SSM_WIDTH), 1)
    keep = ((rows2 & 1) == 0) == ((lane2 & 128) == 0)

    def sub_tile(s, carry):
        t0 = pl.multiple_of(s * SSM_SUB, SSM_SUB)
        u = jnp.concatenate([u_refs[j][pl.ds(t0, SSM_SUB), :] for j in range(nseq)], axis=0)
        ub = u.astype(BF16)
        zero = jnp.zeros_like(ub)
        stacked = jnp.concatenate([jnp.where(low, ub, zero), jnp.where(low, zero, ub)], axis=0)
        lall = _dot(pin_ref[...], stacked).astype(BF16)
        for p in range(SSM_PAIRS):
            hb_ref[p] = _dot(lall[:, 256 * p:256 * (p + 1)], bmat_ref[p])
        for p in range(SSM_PAIRS):
            ar = are_ref[p]
            ai = aim_ref[p]

            def step(t, hc, p=p, ar=ar, ai=ai):
                hr, hi = hc
                r0 = pl.multiple_of(t * q, q)
                bur = hb_ref[p, pl.ds(r0, q), 0:hs]
                bui = hb_ref[p, pl.ds(r0, q), hs:2 * hs]
                nr = ar * hr - ai * hi + bur
                ni = ar * hi + ai * hr + bui
                hb_ref[p, pl.ds(r0, q), 0:hs] = nr
                hb_ref[p, pl.ds(r0, q), hs:2 * hs] = ni
                return nr, ni

            hr, hi = lax.fori_loop(0, SSM_SUB, step, (st_ref[p, :, 0:hs], st_ref[p, :, hs:2 * hs]), unroll=True)
            st_ref[p, :, 0:hs] = hr
            st_ref[p, :, hs:2 * hs] = hi
        y2 = jnp.concatenate([_dot(hb_ref[p].astype(BF16), cmat_ref[p]) for p in range(SSM_PAIRS)], axis=1)
        zf = jnp.where(keep, y2, 0.0)
        z_hi = zf.astype(BF16)
        r1 = zf - z_hi.astype(F32)
        z_mid = r1.astype(BF16)
        z_lo = (r1 - z_mid.astype(F32)).astype(BF16)
        pout = pout_ref[...]
        yn = _dot(pout, z_hi) + _dot(pout, z_mid) + _dot(pout, z_lo)
        g = _gelu_tanh(yn + d_ref[...] * u).astype(BF16)
        for j in range(nseq):
            g_ref[pl.ds(pl.multiple_of(j * tt + t0, SSM_SUB), SSM_SUB), :] = g[j * SSM_SUB:(j + 1) * SSM_SUB]
        return carry

    lax.fori_loop(0, tt // SSM_SUB, sub_tile, 0)

    g = g_ref[...]
    ga = _dot(g, wglu_ref[:, :SSM_WIDTH].astype(BF16))
    gb = _dot(g, wglu_ref[:, SSM_WIDTH:].astype(BF16))
    z = jnp.concatenate([z_refs[j][...] for j in range(nseq)], axis=0)
    out = (ga * jax.nn.sigmoid(gb) * _silu(z)).astype(BF16)
    for j in range(nseq):
        y_ref[j] = out[j * tt:(j + 1) * tt]

    @pl.when(c == pl.num_programs(0) - 1)
    def _():
        ht_ref[...] = st_ref[...]


def _ssm_perms(nseq):
    r, q = nseq * SSM_SUB, 2 * nseq
    pin = np.zeros((2 * r, 2 * r), np.float32)
    pout = np.zeros((r, 2 * r), np.float32)
    for t in range(SSM_SUB):
        for j in range(nseq):
            for h in range(2):
                pin[t * q + 2 * j + h, h * r + j * SSM_SUB + t] = 1.0
                pout[j * SSM_SUB + t, t * q + 2 * j + h] = 1.0
    return jnp.asarray(pin, BF16), jnp.asarray(pout, BF16)


def _ssm(p, d, w_glu, bmat, cmat, a_re, a_im, h0, *, nseq, s, tt, row0, l, h0_layer, name):
    nt = s // tt
    q = 2 * nseq
    rb0 = row0 // tt
    pin, pout = _ssm_perms(nseq)
    seq_spec = lambda j, col: pl.BlockSpec((tt, SSM_WIDTH), lambda c: (rb0 + j * nt + c, col))
    whole = lambda a: pl.BlockSpec(a.shape, lambda c: (0,) * a.ndim)
    layer = lambda a, lay: pl.BlockSpec((SSM_PAIRS,) + a.shape[1:], lambda c: (lay, 0, 0))
    in_specs = ([seq_spec(j, P_U // SSM_WIDTH) for j in range(nseq)]
                + [seq_spec(j, P_ZS // SSM_WIDTH) for j in range(nseq)]
                + [pl.BlockSpec((1, SSM_WIDTH), lambda c: (0, 0)),
                   pl.BlockSpec((SSM_WIDTH, 2 * SSM_WIDTH), lambda c: (l, 0), pipeline_mode=pl.Buffered(1)),
                   layer(bmat, l), layer(cmat, l), layer(a_re, l), layer(a_im, l), layer(h0, h0_layer),
                   whole(pin), whole(pout)])
    return pl.pallas_call(
        functools.partial(_ssm_kernel, nseq=nseq, tt=tt),
        grid=(nt,),
        in_specs=in_specs,
        out_specs=[pl.BlockSpec((nseq, tt, SSM_WIDTH), lambda c: (0, c, 0)),
                   pl.BlockSpec((SSM_PAIRS, q, 2 * SSM_HALF_STATE), lambda c: (0, 0, 0))],
        out_shape=[jax.ShapeDtypeStruct((nseq, s, SSM_WIDTH), BF16),
                   jax.ShapeDtypeStruct((SSM_PAIRS, q, 2 * SSM_HALF_STATE), F32)],
        scratch_shapes=[pltpu.VMEM((SSM_PAIRS, q, 2 * SSM_HALF_STATE), F32),
                        pltpu.VMEM((SSM_PAIRS, q * SSM_SUB, 2 * SSM_HALF_STATE), F32),
                        pltpu.VMEM((nseq * tt, SSM_WIDTH), BF16)],
        compiler_params=_params(("arbitrary",)),
        name=name,
    )(*([p] * (2 * nseq)), d, w_glu, bmat, cmat, a_re, a_im, h0, pin, pout)


def _ssm_state_in(h_re, h_im):
    nl, nseq = h_re.shape[:2]

    def arr(h):
        return h.reshape(nl, nseq, SSM_PAIRS, 2, SSM_HALF_STATE).transpose(0, 2, 1, 3, 4).reshape(
            nl * SSM_PAIRS, nseq * 2, SSM_HALF_STATE)
    return jnp.concatenate([arr(h_re), arr(h_im)], axis=-1)


def _ssm_state_out(ht, nseq):
    nl = ht.shape[0]

    def arr(h):
        return h.reshape(nl, SSM_PAIRS, nseq, 2, 8, SSM_STATE).transpose(0, 2, 1, 3, 4, 5).reshape(
            nl, nseq, SSM_GROUPS, SSM_STATE)
    return arr(ht[..., :SSM_HALF_STATE]), arr(ht[..., SSM_HALF_STATE:])


def _ssm_mats(bb_re_t, bb_im_t, c_re, c_im):
    eye8 = jnp.eye(8, dtype=F32)

    def bmat(bb):
        b6 = bb.reshape(-1, 2, 8, SSM_GROUP, SSM_STATE)
        return jnp.einsum('ahgcp,gk->ahgckp', b6, eye8).reshape(-1, 256, SSM_HALF_STATE)

    def cmat(cc):
        c6 = cc.reshape(-1, 2, 8, SSM_GROUP, SSM_STATE)
        return jnp.einsum('ahgcp,gk->akphgc', c6, eye8).reshape(-1, SSM_HALF_STATE, 256)

    b_all = jnp.concatenate([bmat(bb_re_t), bmat(bb_im_t)], axis=2).astype(BF16)
    c_all = jnp.concatenate([cmat(c_re), cmat(-c_im)], axis=1).astype(BF16)
    return b_all, c_all


def _ssm_a_rows(a, nseq):
    a3 = a.reshape(-1, 1, 2, SSM_HALF_STATE)
    return jnp.broadcast_to(a3, (a3.shape[0], nseq, 2, SSM_HALF_STATE)).reshape(
        a3.shape[0], nseq * 2, SSM_HALF_STATE)


OUT_TM = 256


OUT_PROMPT_TILES = N_PROMPT // OUT_TM


def _merge_out_kernel(*refs, split_x):
    nx = 2 if split_x else 1
    x_refs = refs[:nx]
    (ysp_ref, yss_ref, yap_ref, yas_ref, ymp_ref, yms_ref, g0_ref, g1_ref, g2_ref,
     ws_ref, wa_ref, wm_ref, wo_ref, gp_ref, *o_refs) = refs[nx:]
    i = pl.program_id(0)
    pick = lambda prompt_ref, sample_ref: jnp.where(i < OUT_PROMPT_TILES, prompt_ref[...], sample_ref[...])
    x = pick(*x_refs) if split_x else x_refs[0][...]
    merged = (jax.nn.sigmoid(g0_ref[...]) * _dot(pick(ysp_ref, yss_ref), ws_ref[...])
              + jax.nn.sigmoid(g1_ref[...]) * _dot(pick(yap_ref, yas_ref), wa_ref[...])
              + jax.nn.sigmoid(g2_ref[...]) * _dot(pick(ymp_ref, yms_ref), wm_ref[...]))
    out = _dot(merged.astype(BF16), wo_ref[...])
    y = x + _rms(out, gp_ref[...])
    if len(o_refs) == 1:
        o_refs[0][...] = y
    else:
        @pl.when(i < OUT_PROMPT_TILES)
        def _():
            o_refs[0][...] = y

        @pl.when(i == OUT_PROMPT_TILES)
        def _():
            o_refs[1][...] = y


def _merge_out(x, y_ssm, y_mla, y_mem, p, w_ssm_o, w_mla_o, w_mem_o, w_out, g_post, l, split):
    tm = OUT_TM
    assert N_SAMPLE == tm
    split_x = isinstance(x, tuple)
    row = lambda c: (lambda i: (i, c))
    const = lambda i: (0, 0)
    prompt_row = lambda i: (jnp.minimum(i, OUT_PROMPT_TILES - 1), 0)
    resident = lambda shape: pl.BlockSpec(shape, lambda i: (l, 0), pipeline_mode=pl.Buffered(1))
    branch = lambda width: [pl.BlockSpec((tm, width), prompt_row), pl.BlockSpec((tm, width), const)]
    if split:
        out_specs = [pl.BlockSpec((tm, D_MODEL), prompt_row),
                     pl.BlockSpec((tm, D_MODEL), const)]
        out_shape = [jax.ShapeDtypeStruct((N_PROMPT, D_MODEL), F32),
                     jax.ShapeDtypeStruct((N_SAMPLE, D_MODEL), F32)]
    else:
        out_specs = pl.BlockSpec((tm, D_MODEL), row(0))
        out_shape = jax.ShapeDtypeStruct((N_TOK, D_MODEL), F32)
    x_specs = branch(D_MODEL) if split_x else [pl.BlockSpec((tm, D_MODEL), row(0))]
    x_args = list(x) if split_x else [x]
    return pl.pallas_call(
        functools.partial(_merge_out_kernel, split_x=split_x),
        grid=(N_TOK // tm,),
        in_specs=[*x_specs,
                  *branch(SSM_WIDTH), *branch(MLA_WIDTH), *branch(MEM_WIDTH),
                  pl.BlockSpec((tm, D_MODEL), row(0)),
                  pl.BlockSpec((tm, D_MODEL), row(1)),
                  pl.BlockSpec((tm, D_MODEL), row(2)),
                  resident((SSM_WIDTH, D_MODEL)),
                  resident((MLA_WIDTH, D_MODEL)),
                  resident((MEM_WIDTH, D_MODEL)),
                  resident((D_MODEL, D_MODEL)),
                  pl.BlockSpec((1, D_MODEL), const)],
        out_specs=out_specs,
        out_shape=out_shape,
        compiler_params=_params(("arbitrary",)),
        name="merge_out",
    )(*x_args, *y_ssm, *y_mla, *y_mem, p, p, p, w_ssm_o, w_mla_o, w_mem_o, w_out, g_post)


def _rot_half_cols(w):
    half = w.shape[-1] // 2
    return jnp.concatenate([w[..., half:], w[..., :half]], axis=-1)


def _stack_rows(w):
    return w.astype(BF16).reshape(w.shape[0] * w.shape[1], w.shape[2])


def _prep_small_weights(w_in_t, w_uq, w_uk, w_uv):
    wk = jnp.swapaxes(w_in_t[:, W_IN_K_ROPE:W_IN_K_ROPE + QK_ROPE, :], 1, 2)
    w_kr = _stack_rows(jnp.concatenate([wk, _rot_half_cols(wk)], axis=-1))
    uq = w_uq.reshape(DEPTH, Q_LORA, MLA_HEADS, QK_NOPE + QK_ROPE)
    uq_nope = uq[..., :QK_NOPE].reshape(DEPTH, Q_LORA, MLA_WIDTH)
    uq_rope = uq[..., QK_NOPE:]
    uq_rr = jnp.concatenate([uq_rope, _rot_half_cols(uq_rope)], axis=-1).reshape(DEPTH, Q_LORA, MLA_WIDTH)
    w_uq2 = jnp.concatenate([uq_nope, uq_rr], axis=-1)
    w_ukv = jnp.concatenate([w_uk, w_uv], axis=-1)
    return w_kr, _stack_rows(w_uq2), _stack_rows(w_ukv)


def _rope_table():
    half = QK_ROPE // 2
    pos = jnp.concatenate([jnp.tile(jnp.arange(SEQ, dtype=jnp.int32), BATCH),
                           jnp.tile(PAST_LEN + jnp.arange(DEC_SEQ, dtype=jnp.int32), DEC_BATCH)])
    inv = ROPE_THETA ** (-jnp.arange(half, dtype=F32) / half)
    ang = pos.astype(F32)[:, None] * inv[None, :]
    cos, sin = jnp.cos(ang), jnp.sin(ang)
    return jnp.concatenate([cos, cos, -sin, sin], axis=-1)


def kernel(x_prompt, x_sample, cache_mla_ckv, cache_mla_krope, cache_mem_k, cache_mem_v, state_ssm_re, state_ssm_im, mem_prompt, norm_pre, w_in, ssm_a_re, ssm_a_im, ssm_log_dt, ssm_b_re, ssm_b_im, ssm_c_re, ssm_c_im, ssm_d, w_glu, mla_q_norm, w_uq, mla_kv_norm, w_uk, w_uv, w_mem_k, w_mem_v, w_ssm_o, w_mla_o, w_mem_o, w_out, norm_post):
    w_in_t = jnp.swapaxes(w_in, 1, 2)
    w_in_rows = w_in_t.reshape(DEPTH * W_IN_COLS, D_MODEL)
    w_kr, w_uq2, w_ukv = _prep_small_weights(w_in_t, w_uq, w_uk, w_uv)
    w_glu_rows = w_glu.reshape(DEPTH * SSM_WIDTH, 2 * SSM_WIDTH)
    w_ssm_o_b, w_mla_o_b, w_mem_o_b, w_out_b = (_stack_rows(w) for w in (w_ssm_o, w_mla_o, w_mem_o, w_out))
    w_mem_kv = _stack_rows(jnp.concatenate([w_mem_k, w_mem_v], axis=-1))
    cs = _rope_table()
    mem_rows = mem_prompt.reshape(BATCH * N_MEM, D_MODEL)
    ckv_cache = cache_mla_ckv.reshape(DEPTH * DEC_BATCH * PAST_LEN, KV_LORA)
    krc_t = jnp.swapaxes(cache_mla_krope, 2, 3).reshape(DEPTH * DEC_BATCH * QK_ROPE, PAST_LEN)
    mem_k_cache = cache_mem_k.reshape(DEPTH * DEC_BATCH * N_MEM, MEM_WIDTH)
    mem_v_cache = cache_mem_v.reshape(DEPTH * DEC_BATCH * N_MEM, MEM_WIDTH)

    zeros_state = jnp.zeros((BATCH, SSM_GROUPS, SSM_STATE), F32)
    x = (x_prompt.reshape(N_PROMPT, D_MODEL), x_sample.reshape(N_SAMPLE, D_MODEL))

    mk_all, mv_all = _mem_kv(mem_rows, w_mem_kv)

    lg = DEPTH * SSM_GROUPS
    flat = lambda a: a.reshape((lg,) + a.shape[2:])
    ab_re, ab_im, bb_re_t, bb_im_t = _s5_discretize(
        flat(ssm_a_re), flat(ssm_a_im), ssm_log_dt.reshape(lg),
        flat(jnp.swapaxes(ssm_b_re, 2, 3)), flat(jnp.swapaxes(ssm_b_im, 2, 3)))
    bmat, cmat = _ssm_mats(bb_re_t, bb_im_t, flat(ssm_c_re), flat(ssm_c_im))
    a_rows_p = (_ssm_a_rows(ab_re, BATCH), _ssm_a_rows(ab_im, BATCH))
    a_rows_s = (_ssm_a_rows(ab_re, DEC_BATCH), _ssm_a_rows(ab_im, DEC_BATCH))
    h0_p = _ssm_state_in(zeros_state[None], zeros_state[None])
    h0_s = _ssm_state_in(state_ssm_re, state_ssm_im)

    ht_p, ht_s = [], []
    stacked = None
    for l in range(DEPTH):
        p, kr = _in_proj(x, norm_pre[l][None], w_in_rows, w_kr, l)

        ys_p, ht = _ssm(p, ssm_d[l][None], w_glu_rows, bmat, cmat, *a_rows_p, h0_p,
                        nseq=BATCH, s=SEQ, tt=128, row0=0, l=l, h0_layer=0, name="ssm_prompt")
        ht_p.append(ht)
        ys_s, ht = _ssm(p, ssm_d[l][None], w_glu_rows, bmat, cmat, *a_rows_s, h0_s,
                        nseq=DEC_BATCH, s=DEC_SEQ, tt=DEC_SEQ, row0=N_PROMPT, l=l, h0_layer=l, name="ssm_sample")
        ht_s.append(ht)
        y_ssm = (ys_p.reshape(N_PROMPT, SSM_WIDTH), ys_s.reshape(N_SAMPLE, SSM_WIDTH))

        q, k, v, krp, *stacked = _mla_prep(p, kr, cs, mla_q_norm[l][None], mla_kv_norm[l][None],
                                           w_uq2, w_ukv, l, stacked)
        y_mla = (_attn_prompt(q, k, krp, v, p),
                 _attn_sample(q, ckv_cache, krc_t, stacked[1], krp, w_ukv, p, l))

        y_mem = (_mem_attn(p, mk_all, mv_all, nb=BATCH, seq=SEQ, tq=1024, row0=0, mem_blk0=l * BATCH,
                           name="mem_attn_prompt"),
                 _mem_attn(p, mem_k_cache, mem_v_cache, nb=DEC_BATCH, seq=DEC_SEQ, tq=DEC_SEQ, row0=N_PROMPT,
                           mem_blk0=l * DEC_BATCH, name="mem_attn_sample"))

        x = _merge_out(x, y_ssm, y_mla, y_mem, p, w_ssm_o_b, w_mla_o_b, w_mem_o_b, w_out_b,
                       norm_post[l][None], l, split=l == DEPTH - 1)

    x_p, x_s = x
    ckv_p, ckv_s, kro_p, kro_s = stacked
    p_re, p_im = _ssm_state_out(jnp.stack(ht_p), BATCH)
    s_re, s_im = _ssm_state_out(jnp.stack(ht_s), DEC_BATCH)
    mem_shape = (DEPTH, BATCH, N_MEM, MEM_HEADS, MEM_HEAD_DIM)
    return (x_p.reshape(BATCH, SEQ, D_MODEL), x_s.reshape(DEC_BATCH, DEC_SEQ, D_MODEL),
            p_re, p_im,
            ckv_p.reshape(DEPTH, BATCH, SEQ, KV_LORA), kro_p.reshape(DEPTH, BATCH, SEQ, QK_ROPE),
            mk_all.reshape(mem_shape), mv_all.reshape(mem_shape),
            s_re, s_im,
            ckv_s.reshape(DEPTH, DEC_BATCH, DEC_SEQ, KV_LORA), kro_s.reshape(DEPTH, DEC_BATCH, DEC_SEQ, QK_ROPE))
```

```python
import functools
import math

import jax
import jax.numpy as jnp
import numpy as np
from jax import lax
from jax.experimental import pallas as pl
from jax.experimental.pallas import tpu as pltpu

D_MODEL = 2048
BATCH = 4
SEQ = 2048
DEPTH = 4
DEC_BATCH = 8
DEC_SEQ = 32
PAST_LEN = 2048
CHUNK = 64
N_MEM = 256
EPS = 1e-6
NEG_INF = -1e30
SSM_WIDTH = 1024
SSM_GROUP = 16
SSM_GROUPS = 64
SSM_STATE = 64
MLA_HEADS = 8
QK_NOPE = 128
QK_ROPE = 64
V_HEAD = 128
Q_LORA = 512
KV_LORA = 512
MLA_WIDTH = MLA_HEADS * V_HEAD
ROPE_THETA = 10000.0
MEM_HEADS = 4
MEM_HEAD_DIM = 128
MEM_WIDTH = 512

N_PROMPT = BATCH * SEQ
N_SAMPLE = DEC_BATCH * DEC_SEQ
N_TOK = N_PROMPT + N_SAMPLE

P_GATE = 0
P_U = 6144
P_ZS = 7168
P_CQ = 8192
P_CKV = 8704
P_ZA = 9216
P_QM = 10240
P_ZM = 10752
P_COLS = 11264
W_IN_COLS = 11328

LOG2E = math.log2(math.e)
MLA_SCALE = (QK_NOPE + QK_ROPE) ** -0.5 * LOG2E
MEM_SCALE = MEM_HEAD_DIM ** -0.5 * LOG2E

V7X_VMEM_LIMIT = 52 * 1024 * 1024
LANES = 128

BF16 = jnp.bfloat16
F32 = jnp.float32


def _params(sem, vmem=V7X_VMEM_LIMIT):
    return pltpu.CompilerParams(dimension_semantics=sem, vmem_limit_bytes=vmem)


def _dot(a, b):
    return jnp.dot(a, b, preferred_element_type=F32)


def _dot_nt(a, b):
    return lax.dot_general(a, b, (((1,), (1,)), ((), ())), preferred_element_type=F32)


def _rms(x, g):
    return x * lax.rsqrt(jnp.mean(x * x, axis=-1, keepdims=True) + EPS) * g


def _silu(x):
    return x * jax.nn.sigmoid(x)


def _gelu_tanh(x):
    return 0.5 * x * (1.0 + jnp.tanh(math.sqrt(2.0 / math.pi) * (x + 0.044715 * (x * x * x))))


IN_TM = 1408
IN_TN = 512

W_IN_K_ROPE = 2 * SSM_WIDTH + Q_LORA + KV_LORA
_W_GATE0 = W_IN_K_ROPE + QK_ROPE + MLA_WIDTH + 2 * MEM_WIDTH
_W_UNIT = 64


def _w_src_row(l, j):
    u = lambda c: c // _W_UNIT
    assert all(c % _W_UNIT == 0 for c in (_W_GATE0, IN_TN, W_IN_K_ROPE + QK_ROPE, W_IN_COLS))
    n_gate, n_front = P_U // IN_TN, P_ZA // IN_TN
    start = jnp.where(j < n_gate, u(_W_GATE0) + u(IN_TN) * j,
                      jnp.where(j < n_front, u(IN_TN) * (j - n_gate),
                                u(W_IN_K_ROPE + QK_ROPE) + u(IN_TN) * (j - n_front)))
    return (l * u(W_IN_COLS) + start) * _W_UNIT


IN_TILES = N_TOK // IN_TM
IN_LAST_PROMPT = N_PROMPT - (IN_TILES - 1) * IN_TM


def _in_proj_kernel(*refs, split_x):
    x_refs, (g_ref, wt_ref, wkr_ref, p_ref, kr_ref, h_ref) = refs[:len(refs) - 6], refs[len(refs) - 6:]

    @pl.when(pl.program_id(1) == 0)
    def _():
        if split_x:
            xp_ref, xs_ref = x_refs
            last = pl.program_id(0) == IN_TILES - 1

            @pl.when(jnp.logical_not(last))
            def _():
                h_ref[...] = _rms(xp_ref[...], g_ref[...]).astype(BF16)

            @pl.when(last)
            def _():
                h_ref[:IN_LAST_PROMPT] = _rms(xp_ref[IN_TM - IN_LAST_PROMPT:], g_ref[...]).astype(BF16)
                h_ref[IN_LAST_PROMPT:] = _rms(xs_ref[...], g_ref[...]).astype(BF16)
        else:
            h_ref[...] = _rms(x_refs[0][...], g_ref[...]).astype(BF16)
        kr_ref[...] = _dot(h_ref[...], wkr_ref[...])

    p_ref[...] = _dot_nt(h_ref[...], wt_ref[...].astype(BF16))


def _in_proj(x, g_pre, w_in_t, wkr, l):
    split_x = isinstance(x, tuple)
    if split_x:
        assert IN_TM - IN_LAST_PROMPT == N_SAMPLE and IN_TM % 64 == 0 and (N_PROMPT - IN_TM) % 64 == 0
        clamp = lambda i: jnp.minimum(i * (IN_TM // 64), (N_PROMPT - IN_TM) // 64) * 64
        x_specs = [pl.BlockSpec((pl.Element(IN_TM), pl.Element(D_MODEL)), lambda i, j: (clamp(i), 0)),
                   pl.BlockSpec((N_SAMPLE, D_MODEL), lambda i, j: (0, 0))]
        x_args = list(x)
    else:
        x_specs = [pl.BlockSpec((IN_TM, D_MODEL), lambda i, j: (i, 0))]
        x_args = [x]
    return pl.pallas_call(
        functools.partial(_in_proj_kernel, split_x=split_x),
        grid=(IN_TILES, P_COLS // IN_TN),
        in_specs=x_specs + [
            pl.BlockSpec((1, D_MODEL), lambda i, j: (0, 0)),
            pl.BlockSpec((pl.Element(IN_TN), pl.Element(D_MODEL)), lambda i, j: (_w_src_row(l, j), 0)),
            pl.BlockSpec((D_MODEL, LANES), lambda i, j: (l, 0)),
        ],
        out_specs=[
            pl.BlockSpec((IN_TM, IN_TN), lambda i, j: (i, j)),
            pl.BlockSpec((IN_TM, LANES), lambda i, j: (i, 0)),
        ],
        out_shape=[
            jax.ShapeDtypeStruct((N_TOK, P_COLS), F32),
            jax.ShapeDtypeStruct((N_TOK, LANES), F32),
        ],
        scratch_shapes=[pltpu.VMEM((IN_TM, D_MODEL), BF16)],
        compiler_params=_params(("parallel", "arbitrary")),
        name="in_proj",
    )(*x_args, g_pre, w_in_t, wkr)


PREP_TM = 256
PREP_PROMPT_TILES = N_PROMPT // PREP_TM


def _mla_prep_kernel(*refs, n_alias):
    (cq_ref, ckv_ref, kr_ref, cs_ref, qg_ref, kvg_ref, wuq_ref, wukv_ref,
     q_ref, k_ref, v_ref, krp_ref, ckvp_ref, ckvs_ref, krop_ref, kros_ref) = refs[n_alias:]
    i = pl.program_id(0)
    d = LANES
    cs = cs_ref[...]
    hq = _rms(cq_ref[...], qg_ref[...]).astype(BF16)
    q = _dot(hq, wuq_ref[...])

    ckv = _rms(ckv_ref[...], kvg_ref[...])
    kv = _dot(ckv.astype(BF16), wukv_ref[...])
    k_ref[...] = kv[:, :MLA_WIDTH].astype(BF16)
    v_ref[...] = kv[:, MLA_WIDTH:].astype(BF16)

    tk = kr_ref[...] * cs
    rk = tk + pltpu.roll(tk, QK_ROPE, axis=1)
    lane = lax.broadcasted_iota(jnp.int32, rk.shape, 1)
    krp = jnp.where(lane < QK_ROPE, rk, 0.0).astype(BF16)
    krp_ref[...] = krp

    for h in range(MLA_HEADS):
        t = q[:, MLA_WIDTH + d * h:MLA_WIDTH + d * (h + 1)] * cs
        q_ref[:, 2 * d * h:2 * d * h + d] = (q[:, d * h:d * (h + 1)] * MLA_SCALE).astype(BF16)
        q_ref[:, 2 * d * h + d:2 * d * (h + 1)] = ((t + pltpu.roll(t, QK_ROPE, axis=1)) * MLA_SCALE).astype(BF16)

    @pl.when(i < PREP_PROMPT_TILES)
    def _():
        ckvp_ref[...] = ckv
        krop_ref[...] = rk[:, :QK_ROPE]

    @pl.when(i == PREP_PROMPT_TILES)
    def _():
        ckvs_ref[...] = ckv
        kros_ref[...] = rk[:, :QK_ROPE]


def _mla_prep(p, kr, cs, qg, kvg, wuq, wukv, l, stacked):
    tm = PREP_TM
    row = lambda c: (lambda i: (i, c))
    const = lambda i: (0, 0)
    layer = lambda i: (l, 0)
    prompt_blk = lambda i: (l * PREP_PROMPT_TILES + jnp.minimum(i, PREP_PROMPT_TILES - 1), 0)
    n_alias = 0 if stacked is None else 4
    seq_tiles = SEQ // tm
    cs_blk = lambda i: (jnp.where(i < PREP_PROMPT_TILES, i % seq_tiles, seq_tiles), 0)
    in_specs = [
        pl.BlockSpec((tm, Q_LORA), row(P_CQ // Q_LORA)),
        pl.BlockSpec((tm, KV_LORA), row(P_CKV // KV_LORA)),
        pl.BlockSpec((tm, LANES), row(0)),
        pl.BlockSpec((tm, LANES), cs_blk),
        pl.BlockSpec((1, Q_LORA), const),
        pl.BlockSpec((1, KV_LORA), const),
        pl.BlockSpec((Q_LORA, 2048), layer),
        pl.BlockSpec((KV_LORA, 2048), layer),
    ]
    args = [p, p, kr, cs, qg, kvg, wuq, wukv]
    if stacked is not None:
        in_specs = [pl.BlockSpec(memory_space=pl.ANY)] * 4 + in_specs
        args = list(stacked) + args
    return pl.pallas_call(
        functools.partial(_mla_prep_kernel, n_alias=n_alias),
        grid=(N_TOK // tm,),
        in_specs=in_specs,
        out_specs=[
            pl.BlockSpec((tm, 2 * MLA_WIDTH), row(0)),
            pl.BlockSpec((tm, MLA_WIDTH), row(0)),
            pl.BlockSpec((tm, MLA_WIDTH), row(0)),
            pl.BlockSpec((tm, LANES), row(0)),
            pl.BlockSpec((tm, KV_LORA), prompt_blk),
            pl.BlockSpec((tm, KV_LORA), layer),
            pl.BlockSpec((tm, QK_ROPE), prompt_blk),
            pl.BlockSpec((tm, QK_ROPE), layer),
        ],
        out_shape=[
            jax.ShapeDtypeStruct((N_TOK, 2 * MLA_WIDTH), BF16),
            jax.ShapeDtypeStruct((N_TOK, MLA_WIDTH), BF16),
            jax.ShapeDtypeStruct((N_TOK, MLA_WIDTH), BF16),
            jax.ShapeDtypeStruct((N_TOK, LANES), BF16),
            jax.ShapeDtypeStruct((DEPTH * N_PROMPT, KV_LORA), F32),
            jax.ShapeDtypeStruct((DEPTH * N_SAMPLE, KV_LORA), F32),
            jax.ShapeDtypeStruct((DEPTH * N_PROMPT, QK_ROPE), F32),
            jax.ShapeDtypeStruct((DEPTH * N_SAMPLE, QK_ROPE), F32),
        ],
        input_output_aliases={k: 4 + k for k in range(n_alias)},
        compiler_params=_params(("arbitrary",)),
        name="mla_prep",
    )(*args)


MEMKV_TM = 512


def _mem_kv_kernel(x_ref, w_ref, k_ref, v_ref):
    y = _dot(x_ref[...].astype(BF16), w_ref[...])
    for h in range(MEM_HEADS):
        rows = pl.ds(h, MEMKV_TM, stride=MEM_HEADS)
        k_ref[rows, :] = y[:, MEM_HEAD_DIM * h:MEM_HEAD_DIM * (h + 1)]
        v_ref[rows, :] = y[:, MEM_WIDTH + MEM_HEAD_DIM * h:MEM_WIDTH + MEM_HEAD_DIM * (h + 1)]


def _mem_kv(mem_rows, w_mem_kv):
    tm = MEMKV_TM
    nr = BATCH * N_MEM // tm
    out = pl.BlockSpec((tm * MEM_HEADS, MEM_HEAD_DIM), lambda l, i: (l * nr + i, 0))
    return pl.pallas_call(
        _mem_kv_kernel,
        grid=(DEPTH, nr),
        in_specs=[pl.BlockSpec((tm, D_MODEL), lambda l, i: (i, 0)),
                  pl.BlockSpec((D_MODEL, 2 * MEM_WIDTH), lambda l, i: (l, 0))],
        out_specs=[out, out],
        out_shape=[jax.ShapeDtypeStruct((DEPTH * BATCH * N_MEM * MEM_HEADS, MEM_HEAD_DIM), F32)] * 2,
        compiler_params=_params(("parallel", "parallel")),
        name="mem_kv",
    )(mem_rows, w_mem_kv)


ATT_TQ = 512


def _chunk_mask(q_pos0, k_pos0, nq, nk):
    qp = q_pos0 + lax.broadcasted_iota(jnp.int32, (nq, nk), 0)
    kp = k_pos0 + lax.broadcasted_iota(jnp.int32, (nq, nk), 1)
    sh = CHUNK.bit_length() - 1
    return lax.shift_right_logical(kp, sh) <= lax.shift_right_logical(qp, sh)


ATT_HEADS = 4


def _attn_prompt_kernel(q_ref, kn_ref, kr_ref, v_ref, z_ref, o_ref, k_scr, v_scr):
    tq, d = ATT_TQ, LANES
    for g in range(ATT_HEADS):
        k_scr[g, :, :d] = kn_ref[:, d * g:d * (g + 1)]
        k_scr[g, :, d:] = kr_ref[...]
        v_scr[g, :, :d] = v_ref[:, d * g:d * (g + 1)]
        v_scr[g, :, d:] = jnp.ones((SEQ, d), BF16)
    for qi in range(SEQ // tq):
        lo, hi = qi * tq, (qi + 1) * tq
        for g in range(ATT_HEADS):
            q = q_ref[lo:hi, 2 * d * g:2 * d * (g + 1)]
            sd = jnp.where(_chunk_mask(lo, lo, tq, tq), _dot_nt(q, k_scr[g, lo:hi, :]), NEG_INF)
            m = jnp.max(sd, axis=1, keepdims=True)
            if qi > 0:
                so = _dot_nt(q, k_scr[g, 0:lo, :])
                m = jnp.maximum(m, jnp.max(so, axis=1, keepdims=True))
            acc = _dot(jnp.exp2(sd - m).astype(BF16), v_scr[g, lo:hi, :])
            if qi > 0:
                acc = acc + _dot(jnp.exp2(so - m).astype(BF16), v_scr[g, 0:lo, :])
            z = z_ref[lo:hi, d * g:d * (g + 1)]
            o_ref[lo:hi, d * g:d * (g + 1)] = (acc[:, :d] * (1.0 / acc[:, d:]) * _silu(z)).astype(BF16)


def _attn_prompt(q, kn, krp, v, p):
    w = LANES * ATT_HEADS
    blk = lambda c0: pl.BlockSpec((SEQ, w), lambda b, h: (b, c0 + h))
    return pl.pallas_call(
        _attn_prompt_kernel,
        grid=(BATCH, MLA_HEADS // ATT_HEADS),
        in_specs=[pl.BlockSpec((SEQ, 2 * w), lambda b, h: (b, h)),
                  blk(0), pl.BlockSpec((SEQ, LANES), lambda b, h: (b, 0)), blk(0), blk(P_ZA // w)],
        out_specs=blk(0),
        out_shape=jax.ShapeDtypeStruct((N_PROMPT, MLA_WIDTH), BF16),
        scratch_shapes=[pltpu.VMEM((ATT_HEADS, SEQ, 2 * LANES), BF16), pltpu.VMEM((ATT_HEADS, SEQ, 2 * LANES), BF16)],
        compiler_params=_params(("parallel", "parallel")),
        name="attn_prompt",
    )(q, kn, krp, v, p)


def _attn_sample_kernel(q_ref, ckvc_ref, krct_ref, ckvn_ref, krn_ref, wukv_ref, z_ref, o_ref):
    nh, d = MLA_HEADS, LANES
    qa = jnp.concatenate(
        [_dot_nt(q_ref[:, 2 * d * h:2 * d * h + d], wukv_ref[:, d * h:d * (h + 1)]) for h in range(nh)],
        axis=0).astype(BF16)
    qr = jnp.concatenate([q_ref[:, 2 * d * h + d:2 * d * (h + 1)] for h in range(nh)], axis=0)
    ckvc = ckvc_ref[...].astype(BF16)
    krct = krct_ref[...].astype(BF16)
    ckvn = ckvn_ref[...].astype(BF16)
    rows = nh * DEC_SEQ

    def mask(k_pos0, nk):
        qp = PAST_LEN + (lax.broadcasted_iota(jnp.int32, (rows, nk), 0) & (DEC_SEQ - 1))
        kp = k_pos0 + lax.broadcasted_iota(jnp.int32, (rows, nk), 1)
        sh = CHUNK.bit_length() - 1
        return lax.shift_right_logical(kp, sh) <= lax.shift_right_logical(qp, sh)

    sp = jnp.where(mask(0, PAST_LEN), _dot_nt(qa, ckvc) + _dot(qr[:, :QK_ROPE], krct), NEG_INF)
    sn = jnp.where(mask(PAST_LEN, DEC_SEQ), _dot_nt(qa, ckvn) + _dot_nt(qr, krn_ref[...]), NEG_INF)
    m = jnp.maximum(jnp.max(sp, axis=1, keepdims=True), jnp.max(sn, axis=1, keepdims=True))
    ep, en = jnp.exp2(sp - m), jnp.exp2(sn - m)
    l = jnp.sum(ep, axis=1, keepdims=True) + jnp.sum(en, axis=1, keepdims=True)
    lat = ((_dot(ep.astype(BF16), ckvc) + _dot(en.astype(BF16), ckvn)) * (1.0 / l)).astype(BF16)
    o = jnp.concatenate(
        [_dot(lat[DEC_SEQ * h:DEC_SEQ * (h + 1)], wukv_ref[:, MLA_WIDTH + d * h:MLA_WIDTH + d * (h + 1)])
         for h in range(nh)], axis=1)
    o_ref[...] = (o * _silu(z_ref[...])).astype(BF16)


def _attn_sample(q, ckv_cache, krc_t, ckv_s, krp, wukv, p, l):
    assert DEC_SEQ & (DEC_SEQ - 1) == 0
    r0 = N_PROMPT // DEC_SEQ
    new = lambda width, c: pl.BlockSpec((DEC_SEQ, width), lambda b: (r0 + b, c))
    stream = lambda rows, width: pl.BlockSpec((rows, width), lambda b: (l * DEC_BATCH + b, 0))
    return pl.pallas_call(
        _attn_sample_kernel,
        grid=(DEC_BATCH,),
        in_specs=[new(2 * MLA_WIDTH, 0),
                  stream(PAST_LEN, KV_LORA), stream(QK_ROPE, PAST_LEN),
                  stream(DEC_SEQ, KV_LORA),
                  new(LANES, 0),
                  pl.BlockSpec((KV_LORA, 2048), lambda b: (l, 0)),
                  new(MLA_WIDTH, P_ZA // MLA_WIDTH)],
        out_specs=pl.BlockSpec((DEC_SEQ, MLA_WIDTH), lambda b: (b, 0)),
        out_shape=jax.ShapeDtypeStruct((N_SAMPLE, MLA_WIDTH), BF16),
        compiler_params=_params(("parallel",)),
        name="attn_sample",
    )(q, ckv_cache, krc_t, ckv_s, krp, wukv, p)


def _mem_attn_kernel(q_ref, z_ref, k_ref, v_ref, o_ref):
    q = q_ref[...] * MEM_SCALE
    outs = []
    for h in range(MEM_HEADS):
        sl = slice(MEM_HEAD_DIM * h, MEM_HEAD_DIM * (h + 1))
        rows = pl.ds(h, N_MEM, stride=MEM_HEADS)
        s = _dot_nt(q[:, sl].astype(BF16), k_ref[rows, :].astype(BF16))
        m = jnp.max(s, axis=1, keepdims=True)
        e = jnp.exp2(s - m)
        l = jnp.sum(e, axis=1, keepdims=True)
        outs.append(_dot(e.astype(BF16), v_ref[rows, :].astype(BF16)) * (1.0 / l))
    o_ref[...] = (jnp.concatenate(outs, axis=1) * _silu(z_ref[...])).astype(BF16)


def _mem_attn(p, mem_k, mem_v, *, nb, seq, tq, row0, mem_blk0, name):
    seq_blocks = seq // tq
    r0 = row0 // tq
    rowmap = lambda c: (lambda b, i: (r0 + b * seq_blocks + i, c))
    mem = pl.BlockSpec((N_MEM * MEM_HEADS, MEM_HEAD_DIM), lambda b, i: (mem_blk0 + b, 0))
    return pl.pallas_call(
        _mem_attn_kernel,
        grid=(nb, seq_blocks),
        in_specs=[pl.BlockSpec((tq, MEM_WIDTH), rowmap(P_QM // MEM_WIDTH)),
                  pl.BlockSpec((tq, MEM_WIDTH), rowmap(P_ZM // MEM_WIDTH)),
                  mem, mem],
        out_specs=pl.BlockSpec((tq, MEM_WIDTH), lambda b, i: (b * seq_blocks + i, 0)),
        out_shape=jax.ShapeDtypeStruct((nb * seq, MEM_WIDTH), BF16),
        compiler_params=_params(("parallel", "parallel")),
        name=name,
    )(p, p, mem_k, mem_v)


def _s5_disc_kernel(are_ref, aim_ref, ldt_ref, bre_ref, bim_ref, abr_ref, abi_ref, bbr_ref, bbi_ref):
    dt = jnp.exp(ldt_ref[...])
    lr, li = are_ref[...], aim_ref[...]
    mag = jnp.exp(lr * dt)
    ab_re, ab_im = mag * jnp.cos(li * dt), mag * jnp.sin(li * dt)
    den = lr * lr + li * li
    nr, ni = ab_re - 1.0, ab_im
    f_re = (nr * lr + ni * li) / den
    f_im = (ni * lr - nr * li) / den
    abr_ref[...] = ab_re
    abi_ref[...] = ab_im
    br, bi = bre_ref[...], bim_ref[...]
    bbr_ref[...] = f_re * br - f_im * bi
    bbi_ref[...] = f_re * bi + f_im * br


def _s5_discretize(a_re, a_im, log_dt, b_re_t, b_im_t):
    G, P, C = a_re.shape[0], SSM_STATE, SSM_GROUP
    ab_re, ab_im, bb_re_t, bb_im_t = pl.pallas_call(
        _s5_disc_kernel,
        out_shape=[jax.ShapeDtypeStruct((G, 1, P), F32), jax.ShapeDtypeStruct((G, 1, P), F32),
                   jax.ShapeDtypeStruct((G, C, P), F32), jax.ShapeDtypeStruct((G, C, P), F32)],
        name="s5_discretize",
    )(a_re.reshape(G, 1, P), a_im.reshape(G, 1, P), log_dt.reshape(G, 1, 1), b_re_t, b_im_t)
    return ab_re.reshape(G, P), ab_im.reshape(G, P), bb_re_t, bb_im_t


SSM_PAIRS = 4
SSM_PAIR_W = SSM_WIDTH // SSM_PAIRS
SSM_HALF_STATE = 512
SSM_SUB = 32
SSM_TT = 128


def _ssm_kernel(*refs, nseq, tt):
    u_refs, z_refs = refs[:nseq], refs[nseq:2 * nseq]
    (d_ref, wglu_ref, bmat_ref, cmat_ref, are_ref, aim_ref, h0_ref, pin_ref, pout_ref,
     y_ref, ht_ref, st_ref, hb_ref, g_ref) = refs[2 * nseq:]
    q = 2 * nseq
    r = nseq * SSM_SUB
    hs = SSM_HALF_STATE
    c = pl.program_id(0)

    @pl.when(c == 0)
    def _():
        st_ref[...] = h0_ref[...]

    lane = lax.broadcasted_iota(jnp.int32, (r, SSM_WIDTH), 1)
    low = (lane & LANES) == 0
    rows2 = lax.broadcasted_iota(jnp.int32, (q * SSM_SUB, SSM_WIDTH), 0)
    lane2 = lax.broadcasted_iota(jnp.int32, (q * SSM_SUB, SSM_WIDTH), 1)
    keep = ((rows2 & 1) == 0) == ((lane2 & LANES) == 0)

    def sub_tile(s, carry):
        t0 = pl.multiple_of(s * SSM_SUB, SSM_SUB)
        u = jnp.concatenate([u_refs[j][pl.ds(t0, SSM_SUB), :] for j in range(nseq)], axis=0)
        ub = u.astype(BF16)
        zero = jnp.zeros_like(ub)
        stacked = jnp.concatenate([jnp.where(low, ub, zero), jnp.where(low, zero, ub)], axis=0)
        lall = _dot(pin_ref[...], stacked).astype(BF16)
        for p in range(SSM_PAIRS):
            hb_ref[p] = _dot(lall[:, SSM_PAIR_W * p:SSM_PAIR_W * (p + 1)], bmat_ref[p])
        for p in range(SSM_PAIRS):
            ar = are_ref[p]
            ai = aim_ref[p]

            def step(t, hc, p=p, ar=ar, ai=ai):
                hr, hi = hc
                r0 = pl.multiple_of(t * q, q)
                bur = hb_ref[p, pl.ds(r0, q), 0:hs]
                bui = hb_ref[p, pl.ds(r0, q), hs:2 * hs]
                nr = ar * hr - ai * hi + bur
                ni = ar * hi + ai * hr + bui
                hb_ref[p, pl.ds(r0, q), 0:hs] = nr
                hb_ref[p, pl.ds(r0, q), hs:2 * hs] = ni
                return nr, ni

            hr, hi = lax.fori_loop(0, SSM_SUB, step, (st_ref[p, :, 0:hs], st_ref[p, :, hs:2 * hs]), unroll=True)
            st_ref[p, :, 0:hs] = hr
            st_ref[p, :, hs:2 * hs] = hi
        y2 = jnp.concatenate([_dot(hb_ref[p].astype(BF16), cmat_ref[p]) for p in range(SSM_PAIRS)], axis=1)
        zf = jnp.where(keep, y2, 0.0)
        z_hi = zf.astype(BF16)
        r1 = zf - z_hi.astype(F32)
        z_mid = r1.astype(BF16)
        z_lo = (r1 - z_mid.astype(F32)).astype(BF16)
        pout = pout_ref[...]
        yn = _dot(pout, z_hi) + _dot(pout, z_mid) + _dot(pout, z_lo)
        g = _gelu_tanh(yn + d_ref[...] * u).astype(BF16)
        for j in range(nseq):
            g_ref[pl.ds(pl.multiple_of(j * tt + t0, SSM_SUB), SSM_SUB), :] = g[j * SSM_SUB:(j + 1) * SSM_SUB]
        return carry

    lax.fori_loop(0, tt // SSM_SUB, sub_tile, 0)

    g = g_ref[...]
    ga = _dot(g, wglu_ref[:, :SSM_WIDTH].astype(BF16))
    gb = _dot(g, wglu_ref[:, SSM_WIDTH:].astype(BF16))
    z = jnp.concatenate([z_refs[j][...] for j in range(nseq)], axis=0)
    out = (ga * jax.nn.sigmoid(gb) * _silu(z)).astype(BF16)
    for j in range(nseq):
        y_ref[j] = out[j * tt:(j + 1) * tt]

    @pl.when(c == pl.num_programs(0) - 1)
    def _():
        ht_ref[...] = st_ref[...]


def _ssm_perms(nseq):
    r, q = nseq * SSM_SUB, 2 * nseq
    pin = np.zeros((2 * r, 2 * r), np.float32)
    pout = np.zeros((r, 2 * r), np.float32)
    for t in range(SSM_SUB):
        for j in range(nseq):
            for h in range(2):
                pin[t * q + 2 * j + h, h * r + j * SSM_SUB + t] = 1.0
                pout[j * SSM_SUB + t, t * q + 2 * j + h] = 1.0
    return jnp.asarray(pin, BF16), jnp.asarray(pout, BF16)


def _ssm(p, d, w_glu, bmat, cmat, a_re, a_im, h0, *, nseq, s, tt, row0, l, h0_layer, name):
    nt = s // tt
    q = 2 * nseq
    rb0 = row0 // tt
    pin, pout = _ssm_perms(nseq)
    seq_spec = lambda j, col: pl.BlockSpec((tt, SSM_WIDTH), lambda c: (rb0 + j * nt + c, col))
    whole = lambda a: pl.BlockSpec(a.shape, lambda c: (0,) * a.ndim)
    layer = lambda a, lay: pl.BlockSpec((SSM_PAIRS,) + a.shape[1:], lambda c: (lay, 0, 0))
    in_specs = ([seq_spec(j, P_U // SSM_WIDTH) for j in range(nseq)]
                + [seq_spec(j, P_ZS // SSM_WIDTH) for j in range(nseq)]
                + [pl.BlockSpec((1, SSM_WIDTH), lambda c: (0, 0)),
                   pl.BlockSpec((SSM_WIDTH, 2 * SSM_WIDTH), lambda c: (l, 0), pipeline_mode=pl.Buffered(1)),
                   layer(bmat, l), layer(cmat, l), layer(a_re, l), layer(a_im, l), layer(h0, h0_layer),
                   whole(pin), whole(pout)])
    return pl.pallas_call(
        functools.partial(_ssm_kernel, nseq=nseq, tt=tt),
        grid=(nt,),
        in_specs=in_specs,
        out_specs=[pl.BlockSpec((nseq, tt, SSM_WIDTH), lambda c: (0, c, 0)),
                   pl.BlockSpec((SSM_PAIRS, q, 2 * SSM_HALF_STATE), lambda c: (0, 0, 0))],
        out_shape=[jax.ShapeDtypeStruct((nseq, s, SSM_WIDTH), BF16),
                   jax.ShapeDtypeStruct((SSM_PAIRS, q, 2 * SSM_HALF_STATE), F32)],
        scratch_shapes=[pltpu.VMEM((SSM_PAIRS, q, 2 * SSM_HALF_STATE), F32),
                        pltpu.VMEM((SSM_PAIRS, q * SSM_SUB, 2 * SSM_HALF_STATE), F32),
                        pltpu.VMEM((nseq * tt, SSM_WIDTH), BF16)],
        compiler_params=_params(("arbitrary",)),
        name=name,
    )(*([p] * (2 * nseq)), d, w_glu, bmat, cmat, a_re, a_im, h0, pin, pout)


def _ssm_state_in(h_re, h_im):
    nl, nseq = h_re.shape[:2]

    def arr(h):
        return h.reshape(nl, nseq, SSM_PAIRS, 2, SSM_HALF_STATE).transpose(0, 2, 1, 3, 4).reshape(
            nl * SSM_PAIRS, nseq * 2, SSM_HALF_STATE)
    return jnp.concatenate([arr(h_re), arr(h_im)], axis=-1)


def _ssm_state_out(ht, nseq):
    nl = ht.shape[0]

    def arr(h):
        return h.reshape(nl, SSM_PAIRS, nseq, 2, 8, SSM_STATE).transpose(0, 2, 1, 3, 4, 5).reshape(
            nl, nseq, SSM_GROUPS, SSM_STATE)
    return arr(ht[..., :SSM_HALF_STATE]), arr(ht[..., SSM_HALF_STATE:])


def _ssm_mats(bb_re_t, bb_im_t, c_re, c_im):
    eye8 = jnp.eye(8, dtype=F32)

    def bmat(bb):
        b6 = bb.reshape(-1, 2, 8, SSM_GROUP, SSM_STATE)
        return jnp.einsum('ahgcp,gk->ahgckp', b6, eye8).reshape(-1, SSM_PAIR_W, SSM_HALF_STATE)

    def cmat(cc):
        c6 = cc.reshape(-1, 2, 8, SSM_GROUP, SSM_STATE)
        return jnp.einsum('ahgcp,gk->akphgc', c6, eye8).reshape(-1, SSM_HALF_STATE, SSM_PAIR_W)

    b_all = jnp.concatenate([bmat(bb_re_t), bmat(bb_im_t)], axis=2).astype(BF16)
    c_all = jnp.concatenate([cmat(c_re), cmat(-c_im)], axis=1).astype(BF16)
    return b_all, c_all


def _ssm_a_rows(a, nseq):
    a3 = a.reshape(-1, 1, 2, SSM_HALF_STATE)
    return jnp.broadcast_to(a3, (a3.shape[0], nseq, 2, SSM_HALF_STATE)).reshape(
        a3.shape[0], nseq * 2, SSM_HALF_STATE)


OUT_TM = 256


OUT_PROMPT_TILES = N_PROMPT // OUT_TM


def _merge_out_kernel(*refs, split_x):
    nx = 2 if split_x else 1
    x_refs = refs[:nx]
    (ysp_ref, yss_ref, yap_ref, yas_ref, ymp_ref, yms_ref, g0_ref, g1_ref, g2_ref,
     ws_ref, wa_ref, wm_ref, wo_ref, gp_ref, *o_refs) = refs[nx:]
    i = pl.program_id(0)
    pick = lambda prompt_ref, sample_ref: jnp.where(i < OUT_PROMPT_TILES, prompt_ref[...], sample_ref[...])
    x = pick(*x_refs) if split_x else x_refs[0][...]
    merged = (jax.nn.sigmoid(g0_ref[...]) * _dot(pick(ysp_ref, yss_ref), ws_ref[...])
              + jax.nn.sigmoid(g1_ref[...]) * _dot(pick(yap_ref, yas_ref), wa_ref[...])
              + jax.nn.sigmoid(g2_ref[...]) * _dot(pick(ymp_ref, yms_ref), wm_ref[...]))
    out = _dot(merged.astype(BF16), wo_ref[...])
    y = x + _rms(out, gp_ref[...])
    if len(o_refs) == 1:
        o_refs[0][...] = y
    else:
        @pl.when(i < OUT_PROMPT_TILES)
        def _():
            o_refs[0][...] = y

        @pl.when(i == OUT_PROMPT_TILES)
        def _():
            o_refs[1][...] = y


def _merge_out(x, y_ssm, y_mla, y_mem, p, w_ssm_o, w_mla_o, w_mem_o, w_out, g_post, l, split):
    tm = OUT_TM
    assert N_SAMPLE == tm
    split_x = isinstance(x, tuple)
    row = lambda c: (lambda i: (i, c))
    const = lambda i: (0, 0)
    prompt_row = lambda i: (jnp.minimum(i, OUT_PROMPT_TILES - 1), 0)
    resident = lambda shape: pl.BlockSpec(shape, lambda i: (l, 0), pipeline_mode=pl.Buffered(1))
    branch = lambda width: [pl.BlockSpec((tm, width), prompt_row), pl.BlockSpec((tm, width), const)]
    if split:
        out_specs = [pl.BlockSpec((tm, D_MODEL), prompt_row),
                     pl.BlockSpec((tm, D_MODEL), const)]
        out_shape = [jax.ShapeDtypeStruct((N_PROMPT, D_MODEL), F32),
                     jax.ShapeDtypeStruct((N_SAMPLE, D_MODEL), F32)]
    else:
        out_specs = pl.BlockSpec((tm, D_MODEL), row(0))
        out_shape = jax.ShapeDtypeStruct((N_TOK, D_MODEL), F32)
    x_specs = branch(D_MODEL) if split_x else [pl.BlockSpec((tm, D_MODEL), row(0))]
    x_args = list(x) if split_x else [x]
    return pl.pallas_call(
        functools.partial(_merge_out_kernel, split_x=split_x),
        grid=(N_TOK // tm,),
        in_specs=[*x_specs,
                  *branch(SSM_WIDTH), *branch(MLA_WIDTH), *branch(MEM_WIDTH),
                  *[pl.BlockSpec((tm, D_MODEL), row(P_GATE // D_MODEL + k)) for k in range(3)],
                  resident((SSM_WIDTH, D_MODEL)),
                  resident((MLA_WIDTH, D_MODEL)),
                  resident((MEM_WIDTH, D_MODEL)),
                  resident((D_MODEL, D_MODEL)),
                  pl.BlockSpec((1, D_MODEL), const)],
        out_specs=out_specs,
        out_shape=out_shape,
        compiler_params=_params(("arbitrary",)),
        name="merge_out",
    )(*x_args, *y_ssm, *y_mla, *y_mem, p, p, p, w_ssm_o, w_mla_o, w_mem_o, w_out, g_post)


def _rot_half_cols(w):
    half = w.shape[-1] // 2
    return jnp.concatenate([w[..., half:], w[..., :half]], axis=-1)


def _stack_rows(w):
    return w.astype(BF16).reshape(w.shape[0] * w.shape[1], w.shape[2])


def _prep_small_weights(w_in_t, w_uq, w_uk, w_uv):
    wk = jnp.swapaxes(w_in_t[:, W_IN_K_ROPE:W_IN_K_ROPE + QK_ROPE, :], 1, 2)
    w_kr = _stack_rows(jnp.concatenate([wk, _rot_half_cols(wk)], axis=-1))
    uq = w_uq.reshape(DEPTH, Q_LORA, MLA_HEADS, QK_NOPE + QK_ROPE)
    uq_nope = uq[..., :QK_NOPE].reshape(DEPTH, Q_LORA, MLA_WIDTH)
    uq_rope = uq[..., QK_NOPE:]
    uq_rr = jnp.concatenate([uq_rope, _rot_half_cols(uq_rope)], axis=-1).reshape(DEPTH, Q_LORA, MLA_WIDTH)
    w_uq2 = jnp.concatenate([uq_nope, uq_rr], axis=-1)
    w_ukv = jnp.concatenate([w_uk, w_uv], axis=-1)
    return w_kr, _stack_rows(w_uq2), _stack_rows(w_ukv)


def _rope_table():
    half = QK_ROPE // 2
    pos = jnp.concatenate([jnp.arange(SEQ, dtype=jnp.int32),
                           jnp.tile(PAST_LEN + jnp.arange(DEC_SEQ, dtype=jnp.int32), DEC_BATCH)])
    inv = ROPE_THETA ** (-jnp.arange(half, dtype=F32) / half)
    ang = pos.astype(F32)[:, None] * inv[None, :]
    cos, sin = jnp.cos(ang), jnp.sin(ang)
    return jnp.concatenate([cos, cos, -sin, sin], axis=-1)


def kernel(x_prompt, x_sample, cache_mla_ckv, cache_mla_krope, cache_mem_k, cache_mem_v, state_ssm_re, state_ssm_im, mem_prompt, norm_pre, w_in, ssm_a_re, ssm_a_im, ssm_log_dt, ssm_b_re, ssm_b_im, ssm_c_re, ssm_c_im, ssm_d, w_glu, mla_q_norm, w_uq, mla_kv_norm, w_uk, w_uv, w_mem_k, w_mem_v, w_ssm_o, w_mla_o, w_mem_o, w_out, norm_post):
    w_in_t = jnp.swapaxes(w_in, 1, 2)
    w_in_rows = w_in_t.reshape(DEPTH * W_IN_COLS, D_MODEL)
    w_kr, w_uq2, w_ukv = _prep_small_weights(w_in_t, w_uq, w_uk, w_uv)
    w_glu_rows = w_glu.reshape(DEPTH * SSM_WIDTH, 2 * SSM_WIDTH)
    w_ssm_o_b, w_mla_o_b, w_mem_o_b, w_out_b = (_stack_rows(w) for w in (w_ssm_o, w_mla_o, w_mem_o, w_out))
    w_mem_kv = _stack_rows(jnp.concatenate([w_mem_k, w_mem_v], axis=-1))
    cs = _rope_table()
    mem_rows = mem_prompt.reshape(BATCH * N_MEM, D_MODEL)
    ckv_cache = cache_mla_ckv.reshape(DEPTH * DEC_BATCH * PAST_LEN, KV_LORA)
    krc_t = jnp.swapaxes(cache_mla_krope, 2, 3).reshape(DEPTH * DEC_BATCH * QK_ROPE, PAST_LEN)
    mem_k_cache = cache_mem_k.reshape(DEPTH * DEC_BATCH * N_MEM * MEM_HEADS, MEM_HEAD_DIM)
    mem_v_cache = cache_mem_v.reshape(DEPTH * DEC_BATCH * N_MEM * MEM_HEADS, MEM_HEAD_DIM)

    zeros_state = jnp.zeros((BATCH, SSM_GROUPS, SSM_STATE), F32)
    x = (x_prompt.reshape(N_PROMPT, D_MODEL), x_sample.reshape(N_SAMPLE, D_MODEL))

    mk_all, mv_all = _mem_kv(mem_rows, w_mem_kv)

    lg = DEPTH * SSM_GROUPS
    flat = lambda a: a.reshape((lg,) + a.shape[2:])
    ab_re, ab_im, bb_re_t, bb_im_t = _s5_discretize(
        flat(ssm_a_re), flat(ssm_a_im), ssm_log_dt.reshape(lg),
        flat(jnp.swapaxes(ssm_b_re, 2, 3)), flat(jnp.swapaxes(ssm_b_im, 2, 3)))
    bmat, cmat = _ssm_mats(bb_re_t, bb_im_t, flat(ssm_c_re), flat(ssm_c_im))
    a_rows_p = (_ssm_a_rows(ab_re, BATCH), _ssm_a_rows(ab_im, BATCH))
    a_rows_s = (_ssm_a_rows(ab_re, DEC_BATCH), _ssm_a_rows(ab_im, DEC_BATCH))
    h0_p = _ssm_state_in(zeros_state[None], zeros_state[None])
    h0_s = _ssm_state_in(state_ssm_re, state_ssm_im)

    ht_p, ht_s = [], []
    stacked = None
    for l in range(DEPTH):
        p, kr = _in_proj(x, norm_pre[l][None], w_in_rows, w_kr, l)

        ys_p, ht = _ssm(p, ssm_d[l][None], w_glu_rows, bmat, cmat, *a_rows_p, h0_p,
                        nseq=BATCH, s=SEQ, tt=SSM_TT, row0=0, l=l, h0_layer=0, name="ssm_prompt")
        ht_p.append(ht)
        ys_s, ht = _ssm(p, ssm_d[l][None], w_glu_rows, bmat, cmat, *a_rows_s, h0_s,
                        nseq=DEC_BATCH, s=DEC_SEQ, tt=DEC_SEQ, row0=N_PROMPT, l=l, h0_layer=l, name="ssm_sample")
        ht_s.append(ht)
        y_ssm = (ys_p.reshape(N_PROMPT, SSM_WIDTH), ys_s.reshape(N_SAMPLE, SSM_WIDTH))

        q, k, v, krp, *stacked = _mla_prep(p, kr, cs, mla_q_norm[l][None], mla_kv_norm[l][None],
                                           w_uq2, w_ukv, l, stacked)
        y_mla = (_attn_prompt(q, k, krp, v, p),
                 _attn_sample(q, ckv_cache, krc_t, stacked[1], krp, w_ukv, p, l))

        y_mem = (_mem_attn(p, mk_all, mv_all, nb=BATCH, seq=SEQ, tq=1024, row0=0, mem_blk0=l * BATCH,
                           name="mem_attn_prompt"),
                 _mem_attn(p, mem_k_cache, mem_v_cache, nb=DEC_BATCH, seq=DEC_SEQ, tq=DEC_SEQ, row0=N_PROMPT,
                           mem_blk0=l * DEC_BATCH, name="mem_attn_sample"))

        x = _merge_out(x, y_ssm, y_mla, y_mem, p, w_ssm_o_b, w_mla_o_b, w_mem_o_b, w_out_b,
                       norm_post[l][None], l, split=l == DEPTH - 1)

    x_p, x_s = x
    ckv_p, ckv_s, kro_p, kro_s = stacked
    p_re, p_im = _ssm_state_out(jnp.stack(ht_p), BATCH)
    s_re, s_im = _ssm_state_out(jnp.stack(ht_s), DEC_BATCH)
    mem_shape = (DEPTH, BATCH, N_MEM, MEM_HEADS, MEM_HEAD_DIM)
    return (x_p.reshape(BATCH, SEQ, D_MODEL), x_s.reshape(DEC_BATCH, DEC_SEQ, D_MODEL),
            p_re, p_im,
            ckv_p.reshape(DEPTH, BATCH, SEQ, KV_LORA), kro_p.reshape(DEPTH, BATCH, SEQ, QK_ROPE),
            mk_all.reshape(mem_shape), mv_all.reshape(mem_shape),
            s_re, s_im,
            ckv_s.reshape(DEPTH, DEC_BATCH, DEC_SEQ, KV_LORA), kro_s.reshape(DEPTH, DEC_BATCH, DEC_SEQ, QK_ROPE))
```

```python
import functools
import math

import jax
import jax.numpy as jnp
import numpy as np
from jax import lax
from jax.experimental import pallas as pl
from jax.experimental.pallas import tpu as pltpu

D_MODEL = 2048
BATCH = 4
SEQ = 2048
DEPTH = 4
DEC_BATCH = 8
DEC_SEQ = 32
PAST_LEN = 2048
CHUNK = 64
N_MEM = 256
EPS = 1e-6
NEG_INF = -1e30
SSM_WIDTH = 1024
SSM_GROUP = 16
SSM_GROUPS = 64
SSM_STATE = 64
MLA_HEADS = 8
QK_NOPE = 128
QK_ROPE = 64
V_HEAD = 128
Q_LORA = 512
KV_LORA = 512
MLA_WIDTH = MLA_HEADS * V_HEAD
ROPE_THETA = 10000.0
MEM_HEADS = 4
MEM_HEAD_DIM = 128
MEM_WIDTH = 512

N_PROMPT = BATCH * SEQ
N_SAMPLE = DEC_BATCH * DEC_SEQ
N_TOK = N_PROMPT + N_SAMPLE

P_GATE = 0
P_U = 6144
P_ZS = 7168
P_CQ = 8192
P_CKV = 8704
P_ZA = 9216
P_QM = 10240
P_ZM = 10752
P_COLS = 11264
W_IN_COLS = 11328

LOG2E = math.log2(math.e)
MLA_SCALE = (QK_NOPE + QK_ROPE) ** -0.5 * LOG2E
MEM_SCALE = MEM_HEAD_DIM ** -0.5 * LOG2E

V7X_VMEM_LIMIT = 52 * 1024 * 1024
LANES = 128

BF16 = jnp.bfloat16
F32 = jnp.float32


def _params(sem, vmem=V7X_VMEM_LIMIT):
    return pltpu.CompilerParams(dimension_semantics=sem, vmem_limit_bytes=vmem)


def _dot(a, b):
    return jnp.dot(a, b, preferred_element_type=F32)


def _dot_nt(a, b):
    return lax.dot_general(a, b, (((1,), (1,)), ((), ())), preferred_element_type=F32)


def _rms(x, g):
    return x * lax.rsqrt(jnp.mean(x * x, axis=-1, keepdims=True) + EPS) * g


def _sigmoid(x):
    return 0.5 + 0.5 * jnp.tanh(0.5 * x)


def _silu(x):
    return x * _sigmoid(x)


def _gelu_tanh(x):
    return 0.5 * x * (1.0 + jnp.tanh(math.sqrt(2.0 / math.pi) * (x + 0.044715 * (x * x * x))))


IN_TM = 1408
IN_TN = 512

W_IN_K_ROPE = 2 * SSM_WIDTH + Q_LORA + KV_LORA
_W_GATE0 = W_IN_K_ROPE + QK_ROPE + MLA_WIDTH + 2 * MEM_WIDTH
_W_UNIT = 64


def _w_src_row(l, j):
    u = lambda c: c // _W_UNIT
    assert all(c % _W_UNIT == 0 for c in (_W_GATE0, IN_TN, W_IN_K_ROPE + QK_ROPE, W_IN_COLS))
    n_gate, n_front = P_U // IN_TN, P_ZA // IN_TN
    start = jnp.where(j < n_gate, u(_W_GATE0) + u(IN_TN) * j,
                      jnp.where(j < n_front, u(IN_TN) * (j - n_gate),
                                u(W_IN_K_ROPE + QK_ROPE) + u(IN_TN) * (j - n_front)))
    return (l * u(W_IN_COLS) + start) * _W_UNIT


IN_TILES = N_TOK // IN_TM
IN_LAST_PROMPT = N_PROMPT - (IN_TILES - 1) * IN_TM


def _in_proj_kernel(*refs, split_x):
    x_refs, (g_ref, wt_ref, wkr_ref, p_ref, kr_ref, h_ref) = refs[:len(refs) - 6], refs[len(refs) - 6:]

    @pl.when(pl.program_id(1) == 0)
    def _():
        if split_x:
            xp_ref, xs_ref = x_refs
            last = pl.program_id(0) == IN_TILES - 1

            @pl.when(jnp.logical_not(last))
            def _():
                h_ref[...] = _rms(xp_ref[...], g_ref[...]).astype(BF16)

            @pl.when(last)
            def _():
                h_ref[:IN_LAST_PROMPT] = _rms(xp_ref[IN_TM - IN_LAST_PROMPT:], g_ref[...]).astype(BF16)
                h_ref[IN_LAST_PROMPT:] = _rms(xs_ref[...], g_ref[...]).astype(BF16)
        else:
            h_ref[...] = _rms(x_refs[0][...], g_ref[...]).astype(BF16)
        kr_ref[...] = _dot(h_ref[...], wkr_ref[...])

    p_ref[...] = _dot_nt(h_ref[...], wt_ref[...].astype(BF16))


def _in_proj(x, g_pre, w_in_t, wkr, l):
    split_x = isinstance(x, tuple)
    if split_x:
        assert IN_TM - IN_LAST_PROMPT == N_SAMPLE and IN_TM % 64 == 0 and (N_PROMPT - IN_TM) % 64 == 0
        clamp = lambda i: jnp.minimum(i * (IN_TM // 64), (N_PROMPT - IN_TM) // 64) * 64
        x_specs = [pl.BlockSpec((pl.Element(IN_TM), pl.Element(D_MODEL)), lambda i, j: (clamp(i), 0)),
                   pl.BlockSpec((N_SAMPLE, D_MODEL), lambda i, j: (0, 0))]
        x_args = list(x)
    else:
        x_specs = [pl.BlockSpec((IN_TM, D_MODEL), lambda i, j: (i, 0))]
        x_args = [x]
    return pl.pallas_call(
        functools.partial(_in_proj_kernel, split_x=split_x),
        grid=(IN_TILES, P_COLS // IN_TN),
        in_specs=x_specs + [
            pl.BlockSpec((1, D_MODEL), lambda i, j: (0, 0)),
            pl.BlockSpec((pl.Element(IN_TN), pl.Element(D_MODEL)), lambda i, j: (_w_src_row(l, j), 0)),
            pl.BlockSpec((D_MODEL, LANES), lambda i, j: (l, 0)),
        ],
        out_specs=[
            pl.BlockSpec((IN_TM, IN_TN), lambda i, j: (i, j)),
            pl.BlockSpec((IN_TM, LANES), lambda i, j: (i, 0)),
        ],
        out_shape=[
            jax.ShapeDtypeStruct((N_TOK, P_COLS), F32),
            jax.ShapeDtypeStruct((N_TOK, LANES), F32),
        ],
        scratch_shapes=[pltpu.VMEM((IN_TM, D_MODEL), BF16)],
        compiler_params=_params(("parallel", "arbitrary")),
        name="in_proj",
    )(*x_args, g_pre, w_in_t, wkr)


PREP_TM = 256
PREP_PROMPT_TILES = N_PROMPT // PREP_TM


def _mla_prep_kernel(*refs, n_alias):
    (cq_ref, ckv_ref, kr_ref, cs_ref, qg_ref, kvg_ref, wuq_ref, wukv_ref,
     q_ref, k_ref, v_ref, krp_ref, ckvp_ref, ckvs_ref, krop_ref, kros_ref) = refs[n_alias:]
    i = pl.program_id(0)
    d = LANES
    cs = cs_ref[...]
    hq = _rms(cq_ref[...], qg_ref[...]).astype(BF16)
    q = _dot(hq, wuq_ref[...])

    ckv = _rms(ckv_ref[...], kvg_ref[...])
    kv = _dot(ckv.astype(BF16), wukv_ref[...])
    k_ref[...] = kv[:, :MLA_WIDTH].astype(BF16)
    v_ref[...] = kv[:, MLA_WIDTH:].astype(BF16)

    tk = kr_ref[...] * cs
    rk = tk + pltpu.roll(tk, QK_ROPE, axis=1)
    lane = lax.broadcasted_iota(jnp.int32, rk.shape, 1)
    krp = jnp.where(lane < QK_ROPE, rk, 0.0).astype(BF16)
    krp_ref[...] = krp

    for h in range(MLA_HEADS):
        t = q[:, MLA_WIDTH + d * h:MLA_WIDTH + d * (h + 1)] * cs
        q_ref[:, 2 * d * h:2 * d * h + d] = (q[:, d * h:d * (h + 1)] * MLA_SCALE).astype(BF16)
        q_ref[:, 2 * d * h + d:2 * d * (h + 1)] = ((t + pltpu.roll(t, QK_ROPE, axis=1)) * MLA_SCALE).astype(BF16)

    @pl.when(i < PREP_PROMPT_TILES)
    def _():
        ckvp_ref[...] = ckv
        krop_ref[...] = rk[:, :QK_ROPE]

    @pl.when(i == PREP_PROMPT_TILES)
    def _():
        ckvs_ref[...] = ckv
        kros_ref[...] = rk[:, :QK_ROPE]


def _mla_prep(p, kr, cs, qg, kvg, wuq, wukv, l, stacked):
    tm = PREP_TM
    row = lambda c: (lambda i: (i, c))
    const = lambda i: (0, 0)
    layer = lambda i: (l, 0)
    prompt_blk = lambda i: (l * PREP_PROMPT_TILES + jnp.minimum(i, PREP_PROMPT_TILES - 1), 0)
    n_alias = 0 if stacked is None else 4
    seq_tiles = SEQ // tm
    cs_blk = lambda i: (jnp.where(i < PREP_PROMPT_TILES, i % seq_tiles, seq_tiles), 0)
    in_specs = [
        pl.BlockSpec((tm, Q_LORA), row(P_CQ // Q_LORA)),
        pl.BlockSpec((tm, KV_LORA), row(P_CKV // KV_LORA)),
        pl.BlockSpec((tm, LANES), row(0)),
        pl.BlockSpec((tm, LANES), cs_blk),
        pl.BlockSpec((1, Q_LORA), const),
        pl.BlockSpec((1, KV_LORA), const),
        pl.BlockSpec((Q_LORA, 2048), layer),
        pl.BlockSpec((KV_LORA, 2048), layer),
    ]
    args = [p, p, kr, cs, qg, kvg, wuq, wukv]
    if stacked is not None:
        in_specs = [pl.BlockSpec(memory_space=pl.ANY)] * 4 + in_specs
        args = list(stacked) + args
    return pl.pallas_call(
        functools.partial(_mla_prep_kernel, n_alias=n_alias),
        grid=(N_TOK // tm,),
        in_specs=in_specs,
        out_specs=[
            pl.BlockSpec((tm, 2 * MLA_WIDTH), row(0)),
            pl.BlockSpec((tm, MLA_WIDTH), row(0)),
            pl.BlockSpec((tm, MLA_WIDTH), row(0)),
            pl.BlockSpec((tm, LANES), row(0)),
            pl.BlockSpec((tm, KV_LORA), prompt_blk),
            pl.BlockSpec((tm, KV_LORA), layer),
            pl.BlockSpec((tm, QK_ROPE), prompt_blk),
            pl.BlockSpec((tm, QK_ROPE), layer),
        ],
        out_shape=[
            jax.ShapeDtypeStruct((N_TOK, 2 * MLA_WIDTH), BF16),
            jax.ShapeDtypeStruct((N_TOK, MLA_WIDTH), BF16),
            jax.ShapeDtypeStruct((N_TOK, MLA_WIDTH), BF16),
            jax.ShapeDtypeStruct((N_TOK, LANES), BF16),
            jax.ShapeDtypeStruct((DEPTH * N_PROMPT, KV_LORA), F32),
            jax.ShapeDtypeStruct((DEPTH * N_SAMPLE, KV_LORA), F32),
            jax.ShapeDtypeStruct((DEPTH * N_PROMPT, QK_ROPE), F32),
            jax.ShapeDtypeStruct((DEPTH * N_SAMPLE, QK_ROPE), F32),
        ],
        input_output_aliases={k: 4 + k for k in range(n_alias)},
        compiler_params=_params(("arbitrary",)),
        name="mla_prep",
    )(*args)


MEMKV_TM = 512


def _mem_kv_kernel(x_ref, w_ref, k_ref, v_ref):
    y = _dot(x_ref[...].astype(BF16), w_ref[...])
    for h in range(MEM_HEADS):
        rows = pl.ds(h, MEMKV_TM, stride=MEM_HEADS)
        k_ref[rows, :] = y[:, MEM_HEAD_DIM * h:MEM_HEAD_DIM * (h + 1)]
        v_ref[rows, :] = y[:, MEM_WIDTH + MEM_HEAD_DIM * h:MEM_WIDTH + MEM_HEAD_DIM * (h + 1)]


def _mem_kv(mem_rows, w_mem_kv):
    tm = MEMKV_TM
    nr = BATCH * N_MEM // tm
    out = pl.BlockSpec((tm * MEM_HEADS, MEM_HEAD_DIM), lambda l, i: (l * nr + i, 0))
    return pl.pallas_call(
        _mem_kv_kernel,
        grid=(DEPTH, nr),
        in_specs=[pl.BlockSpec((tm, D_MODEL), lambda l, i: (i, 0)),
                  pl.BlockSpec((D_MODEL, 2 * MEM_WIDTH), lambda l, i: (l, 0))],
        out_specs=[out, out],
        out_shape=[jax.ShapeDtypeStruct((DEPTH * BATCH * N_MEM * MEM_HEADS, MEM_HEAD_DIM), F32)] * 2,
        compiler_params=_params(("parallel", "parallel")),
        name="mem_kv",
    )(mem_rows, w_mem_kv)


ATT_TQ = 512


def _chunk_mask(q_pos0, k_pos0, nq, nk):
    qp = q_pos0 + lax.broadcasted_iota(jnp.int32, (nq, nk), 0)
    kp = k_pos0 + lax.broadcasted_iota(jnp.int32, (nq, nk), 1)
    sh = CHUNK.bit_length() - 1
    return lax.shift_right_logical(kp, sh) <= lax.shift_right_logical(qp, sh)


ATT_HEADS = 4


def _attn_prompt_kernel(q_ref, kn_ref, kr_ref, v_ref, z_ref, o_ref, k_scr, v_scr):
    tq, d = ATT_TQ, LANES
    for g in range(ATT_HEADS):
        k_scr[g, :, :d] = kn_ref[:, d * g:d * (g + 1)]
        k_scr[g, :, d:] = kr_ref[...]
        v_scr[g, :, :d] = v_ref[:, d * g:d * (g + 1)]
        v_scr[g, :, d:] = jnp.ones((SEQ, d), BF16)
    for qi in range(SEQ // tq):
        lo, hi = qi * tq, (qi + 1) * tq
        for g in range(ATT_HEADS):
            q = q_ref[lo:hi, 2 * d * g:2 * d * (g + 1)]
            sd = jnp.where(_chunk_mask(lo, lo, tq, tq), _dot_nt(q, k_scr[g, lo:hi, :]), NEG_INF)
            m = jnp.max(sd, axis=1, keepdims=True)
            if qi > 0:
                so = _dot_nt(q, k_scr[g, 0:lo, :])
                m = jnp.maximum(m, jnp.max(so, axis=1, keepdims=True))
            acc = _dot(jnp.exp2(sd - m).astype(BF16), v_scr[g, lo:hi, :])
            if qi > 0:
                acc = acc + _dot(jnp.exp2(so - m).astype(BF16), v_scr[g, 0:lo, :])
            z = z_ref[lo:hi, d * g:d * (g + 1)]
            o_ref[lo:hi, d * g:d * (g + 1)] = (acc[:, :d] * (1.0 / acc[:, d:]) * _silu(z)).astype(BF16)


def _attn_prompt(q, kn, krp, v, p):
    w = LANES * ATT_HEADS
    blk = lambda c0: pl.BlockSpec((SEQ, w), lambda b, h: (b, c0 + h))
    return pl.pallas_call(
        _attn_prompt_kernel,
        grid=(BATCH, MLA_HEADS // ATT_HEADS),
        in_specs=[pl.BlockSpec((SEQ, 2 * w), lambda b, h: (b, h)),
                  blk(0), pl.BlockSpec((SEQ, LANES), lambda b, h: (b, 0)), blk(0), blk(P_ZA // w)],
        out_specs=blk(0),
        out_shape=jax.ShapeDtypeStruct((N_PROMPT, MLA_WIDTH), BF16),
        scratch_shapes=[pltpu.VMEM((ATT_HEADS, SEQ, 2 * LANES), BF16), pltpu.VMEM((ATT_HEADS, SEQ, 2 * LANES), BF16)],
        compiler_params=_params(("parallel", "parallel")),
        name="attn_prompt",
    )(q, kn, krp, v, p)


def _attn_sample_kernel(q_ref, ckvc_ref, krct_ref, ckvn_ref, krn_ref, wukv_ref, z_ref, o_ref):
    nh, d = MLA_HEADS, LANES
    qa = jnp.concatenate(
        [_dot_nt(q_ref[:, 2 * d * h:2 * d * h + d], wukv_ref[:, d * h:d * (h + 1)]) for h in range(nh)],
        axis=0).astype(BF16)
    qr = jnp.concatenate([q_ref[:, 2 * d * h + d:2 * d * (h + 1)] for h in range(nh)], axis=0)
    ckvc = ckvc_ref[...].astype(BF16)
    krct = krct_ref[...].astype(BF16)
    ckvn = ckvn_ref[...].astype(BF16)
    rows = nh * DEC_SEQ

    def mask(k_pos0, nk):
        qp = PAST_LEN + (lax.broadcasted_iota(jnp.int32, (rows, nk), 0) & (DEC_SEQ - 1))
        kp = k_pos0 + lax.broadcasted_iota(jnp.int32, (rows, nk), 1)
        sh = CHUNK.bit_length() - 1
        return lax.shift_right_logical(kp, sh) <= lax.shift_right_logical(qp, sh)

    sp = jnp.where(mask(0, PAST_LEN), _dot_nt(qa, ckvc) + _dot(qr[:, :QK_ROPE], krct), NEG_INF)
    sn = jnp.where(mask(PAST_LEN, DEC_SEQ), _dot_nt(qa, ckvn) + _dot_nt(qr, krn_ref[...]), NEG_INF)
    m = jnp.maximum(jnp.max(sp, axis=1, keepdims=True), jnp.max(sn, axis=1, keepdims=True))
    ep, en = jnp.exp2(sp - m), jnp.exp2(sn - m)
    l = jnp.sum(ep, axis=1, keepdims=True) + jnp.sum(en, axis=1, keepdims=True)
    lat = ((_dot(ep.astype(BF16), ckvc) + _dot(en.astype(BF16), ckvn)) * (1.0 / l)).astype(BF16)
    o = jnp.concatenate(
        [_dot(lat[DEC_SEQ * h:DEC_SEQ * (h + 1)], wukv_ref[:, MLA_WIDTH + d * h:MLA_WIDTH + d * (h + 1)])
         for h in range(nh)], axis=1)
    o_ref[...] = (o * _silu(z_ref[...])).astype(BF16)


def _attn_sample(q, ckv_cache, krc_t, ckv_s, krp, wukv, p, l):
    assert DEC_SEQ & (DEC_SEQ - 1) == 0
    r0 = N_PROMPT // DEC_SEQ
    new = lambda width, c: pl.BlockSpec((DEC_SEQ, width), lambda b: (r0 + b, c))
    stream = lambda rows, width: pl.BlockSpec((rows, width), lambda b: (l * DEC_BATCH + b, 0))
    return pl.pallas_call(
        _attn_sample_kernel,
        grid=(DEC_BATCH,),
        in_specs=[new(2 * MLA_WIDTH, 0),
                  stream(PAST_LEN, KV_LORA), stream(QK_ROPE, PAST_LEN),
                  stream(DEC_SEQ, KV_LORA),
                  new(LANES, 0),
                  pl.BlockSpec((KV_LORA, 2048), lambda b: (l, 0)),
                  new(MLA_WIDTH, P_ZA // MLA_WIDTH)],
        out_specs=pl.BlockSpec((DEC_SEQ, MLA_WIDTH), lambda b: (b, 0)),
        out_shape=jax.ShapeDtypeStruct((N_SAMPLE, MLA_WIDTH), BF16),
        compiler_params=_params(("parallel",)),
        name="attn_sample",
    )(q, ckv_cache, krc_t, ckv_s, krp, wukv, p)


def _mem_attn_kernel(q_ref, z_ref, k_ref, v_ref, o_ref):
    q = q_ref[...] * MEM_SCALE
    outs = []
    for h in range(MEM_HEADS):
        sl = slice(MEM_HEAD_DIM * h, MEM_HEAD_DIM * (h + 1))
        rows = pl.ds(h, N_MEM, stride=MEM_HEADS)
        s = _dot_nt(q[:, sl].astype(BF16), k_ref[rows, :].astype(BF16))
        m = jnp.max(s, axis=1, keepdims=True)
        e = jnp.exp2(s - m)
        l = jnp.sum(e, axis=1, keepdims=True)
        outs.append(_dot(e.astype(BF16), v_ref[rows, :].astype(BF16)) * (1.0 / l))
    o_ref[...] = (jnp.concatenate(outs, axis=1) * _silu(z_ref[...])).astype(BF16)


def _mem_attn(p, mem_k, mem_v, *, nb, seq, tq, row0, mem_blk0, name):
    seq_blocks = seq // tq
    r0 = row0 // tq
    rowmap = lambda c: (lambda b, i: (r0 + b * seq_blocks + i, c))
    mem = pl.BlockSpec((N_MEM * MEM_HEADS, MEM_HEAD_DIM), lambda b, i: (mem_blk0 + b, 0))
    return pl.pallas_call(
        _mem_attn_kernel,
        grid=(nb, seq_blocks),
        in_specs=[pl.BlockSpec((tq, MEM_WIDTH), rowmap(P_QM // MEM_WIDTH)),
                  pl.BlockSpec((tq, MEM_WIDTH), rowmap(P_ZM // MEM_WIDTH)),
                  mem, mem],
        out_specs=pl.BlockSpec((tq, MEM_WIDTH), lambda b, i: (b * seq_blocks + i, 0)),
        out_shape=jax.ShapeDtypeStruct((nb * seq, MEM_WIDTH), BF16),
        compiler_params=_params(("parallel", "parallel")),
        name=name,
    )(p, p, mem_k, mem_v)


def _s5_disc_kernel(are_ref, aim_ref, ldt_ref, bre_ref, bim_ref, abr_ref, abi_ref, bbr_ref, bbi_ref):
    dt = jnp.exp(ldt_ref[...])
    lr, li = are_ref[...], aim_ref[...]
    mag = jnp.exp(lr * dt)
    ab_re, ab_im = mag * jnp.cos(li * dt), mag * jnp.sin(li * dt)
    den = lr * lr + li * li
    nr, ni = ab_re - 1.0, ab_im
    f_re = (nr * lr + ni * li) / den
    f_im = (ni * lr - nr * li) / den
    abr_ref[...] = ab_re
    abi_ref[...] = ab_im
    br, bi = bre_ref[...], bim_ref[...]
    bbr_ref[...] = f_re * br - f_im * bi
    bbi_ref[...] = f_re * bi + f_im * br


def _s5_discretize(a_re, a_im, log_dt, b_re_t, b_im_t):
    G, P, C = a_re.shape[0], SSM_STATE, SSM_GROUP
    ab_re, ab_im, bb_re_t, bb_im_t = pl.pallas_call(
        _s5_disc_kernel,
        out_shape=[jax.ShapeDtypeStruct((G, 1, P), F32), jax.ShapeDtypeStruct((G, 1, P), F32),
                   jax.ShapeDtypeStruct((G, C, P), F32), jax.ShapeDtypeStruct((G, C, P), F32)],
        name="s5_discretize",
    )(a_re.reshape(G, 1, P), a_im.reshape(G, 1, P), log_dt.reshape(G, 1, 1), b_re_t, b_im_t)
    return ab_re.reshape(G, P), ab_im.reshape(G, P), bb_re_t, bb_im_t


SSM_PAIRS = 4
SSM_PAIR_W = SSM_WIDTH // SSM_PAIRS
SSM_HALF_STATE = 512
SSM_SUB = 32
SSM_TT = 128


def _ssm_kernel(*refs, nseq, tt):
    u_refs, z_refs = refs[:nseq], refs[nseq:2 * nseq]
    (d_ref, wglu_ref, bmat_ref, cmat_ref, are_ref, aim_ref, h0_ref, pin_ref, pout_ref,
     y_ref, ht_ref, st_ref, hb_ref, g_ref) = refs[2 * nseq:]
    q = 2 * nseq
    r = nseq * SSM_SUB
    hs = SSM_HALF_STATE
    c = pl.program_id(0)

    @pl.when(c == 0)
    def _():
        st_ref[...] = h0_ref[...]

    lane = lax.broadcasted_iota(jnp.int32, (r, SSM_WIDTH), 1)
    low = (lane & LANES) == 0
    rows2 = lax.broadcasted_iota(jnp.int32, (q * SSM_SUB, SSM_WIDTH), 0)
    lane2 = lax.broadcasted_iota(jnp.int32, (q * SSM_SUB, SSM_WIDTH), 1)
    keep = ((rows2 & 1) == 0) == ((lane2 & LANES) == 0)

    def sub_tile(s, carry):
        t0 = pl.multiple_of(s * SSM_SUB, SSM_SUB)
        u = jnp.concatenate([u_refs[j][pl.ds(t0, SSM_SUB), :] for j in range(nseq)], axis=0)
        ub = u.astype(BF16)
        zero = jnp.zeros_like(ub)
        stacked = jnp.concatenate([jnp.where(low, ub, zero), jnp.where(low, zero, ub)], axis=0)
        lall = _dot(pin_ref[...], stacked).astype(BF16)
        for p in range(SSM_PAIRS):
            hb_ref[p] = _dot(lall[:, SSM_PAIR_W * p:SSM_PAIR_W * (p + 1)], bmat_ref[p])
        for p in range(SSM_PAIRS):
            ar = are_ref[p]
            ai = aim_ref[p]

            def step(t, hc, p=p, ar=ar, ai=ai):
                hr, hi = hc
                r0 = pl.multiple_of(t * q, q)
                bur = hb_ref[p, pl.ds(r0, q), 0:hs]
                bui = hb_ref[p, pl.ds(r0, q), hs:2 * hs]
                nr = ar * hr - ai * hi + bur
                ni = ar * hi + ai * hr + bui
                hb_ref[p, pl.ds(r0, q), 0:hs] = nr
                hb_ref[p, pl.ds(r0, q), hs:2 * hs] = ni
                return nr, ni

            hr, hi = lax.fori_loop(0, SSM_SUB, step, (st_ref[p, :, 0:hs], st_ref[p, :, hs:2 * hs]), unroll=True)
            st_ref[p, :, 0:hs] = hr
            st_ref[p, :, hs:2 * hs] = hi
        y2 = jnp.concatenate([_dot(hb_ref[p].astype(BF16), cmat_ref[p]) for p in range(SSM_PAIRS)], axis=1)
        zf = jnp.where(keep, y2, 0.0)
        z_hi = zf.astype(BF16)
        r1 = zf - z_hi.astype(F32)
        z_mid = r1.astype(BF16)
        z_lo = (r1 - z_mid.astype(F32)).astype(BF16)
        pout = pout_ref[...]
        yn = _dot(pout, z_hi) + _dot(pout, z_mid) + _dot(pout, z_lo)
        g = _gelu_tanh(yn + d_ref[...] * u).astype(BF16)
        for j in range(nseq):
            g_ref[pl.ds(pl.multiple_of(j * tt + t0, SSM_SUB), SSM_SUB), :] = g[j * SSM_SUB:(j + 1) * SSM_SUB]
        return carry

    lax.fori_loop(0, tt // SSM_SUB, sub_tile, 0)

    g = g_ref[...]
    ga = _dot(g, wglu_ref[:, :SSM_WIDTH].astype(BF16))
    gb = _dot(g, wglu_ref[:, SSM_WIDTH:].astype(BF16))
    z = jnp.concatenate([z_refs[j][...] for j in range(nseq)], axis=0)
    out = (ga * _sigmoid(gb) * _silu(z)).astype(BF16)
    for j in range(nseq):
        y_ref[j] = out[j * tt:(j + 1) * tt]

    @pl.when(c == pl.num_programs(0) - 1)
    def _():
        ht_ref[...] = st_ref[...]


def _ssm_perms(nseq):
    r, q = nseq * SSM_SUB, 2 * nseq
    pin = np.zeros((2 * r, 2 * r), np.float32)
    pout = np.zeros((r, 2 * r), np.float32)
    for t in range(SSM_SUB):
        for j in range(nseq):
            for h in range(2):
                pin[t * q + 2 * j + h, h * r + j * SSM_SUB + t] = 1.0
                pout[j * SSM_SUB + t, t * q + 2 * j + h] = 1.0
    return jnp.asarray(pin, BF16), jnp.asarray(pout, BF16)


def _ssm(p, d, w_glu, bmat, cmat, a_re, a_im, h0, *, nseq, s, tt, row0, l, h0_layer, name):
    nt = s // tt
    q = 2 * nseq
    rb0 = row0 // tt
    pin, pout = _ssm_perms(nseq)
    seq_spec = lambda j, col: pl.BlockSpec((tt, SSM_WIDTH), lambda c: (rb0 + j * nt + c, col))
    whole = lambda a: pl.BlockSpec(a.shape, lambda c: (0,) * a.ndim)
    layer = lambda a, lay: pl.BlockSpec((SSM_PAIRS,) + a.shape[1:], lambda c: (lay, 0, 0))
    in_specs = ([seq_spec(j, P_U // SSM_WIDTH) for j in range(nseq)]
                + [seq_spec(j, P_ZS // SSM_WIDTH) for j in range(nseq)]
                + [pl.BlockSpec((1, SSM_WIDTH), lambda c: (0, 0)),
                   pl.BlockSpec((SSM_WIDTH, 2 * SSM_WIDTH), lambda c: (l, 0), pipeline_mode=pl.Buffered(1)),
                   layer(bmat, l), layer(cmat, l), layer(a_re, l), layer(a_im, l), layer(h0, h0_layer),
                   whole(pin), whole(pout)])
    return pl.pallas_call(
        functools.partial(_ssm_kernel, nseq=nseq, tt=tt),
        grid=(nt,),
        in_specs=in_specs,
        out_specs=[pl.BlockSpec((nseq, tt, SSM_WIDTH), lambda c: (0, c, 0)),
                   pl.BlockSpec((SSM_PAIRS, q, 2 * SSM_HALF_STATE), lambda c: (0, 0, 0))],
        out_shape=[jax.ShapeDtypeStruct((nseq, s, SSM_WIDTH), BF16),
                   jax.ShapeDtypeStruct((SSM_PAIRS, q, 2 * SSM_HALF_STATE), F32)],
        scratch_shapes=[pltpu.VMEM((SSM_PAIRS, q, 2 * SSM_HALF_STATE), F32),
                        pltpu.VMEM((SSM_PAIRS, q * SSM_SUB, 2 * SSM_HALF_STATE), F32),
                        pltpu.VMEM((nseq * tt, SSM_WIDTH), BF16)],
        compiler_params=_params(("arbitrary",)),
        name=name,
    )(*([p] * (2 * nseq)), d, w_glu, bmat, cmat, a_re, a_im, h0, pin, pout)


def _ssm_state_in(h_re, h_im):
    nl, nseq = h_re.shape[:2]

    def arr(h):
        return h.reshape(nl, nseq, SSM_PAIRS, 2, SSM_HALF_STATE).transpose(0, 2, 1, 3, 4).reshape(
            nl * SSM_PAIRS, nseq * 2, SSM_HALF_STATE)
    return jnp.concatenate([arr(h_re), arr(h_im)], axis=-1)


def _ssm_state_out(ht, nseq):
    nl = ht.shape[0]

    def arr(h):
        return h.reshape(nl, SSM_PAIRS, nseq, 2, 8, SSM_STATE).transpose(0, 2, 1, 3, 4, 5).reshape(
            nl, nseq, SSM_GROUPS, SSM_STATE)
    return arr(ht[..., :SSM_HALF_STATE]), arr(ht[..., SSM_HALF_STATE:])


def _ssm_mats(bb_re_t, bb_im_t, c_re, c_im):
    eye8 = jnp.eye(8, dtype=F32)

    def bmat(bb):
        b6 = bb.reshape(-1, 2, 8, SSM_GROUP, SSM_STATE)
        return jnp.einsum('ahgcp,gk->ahgckp', b6, eye8).reshape(-1, SSM_PAIR_W, SSM_HALF_STATE)

    def cmat(cc):
        c6 = cc.reshape(-1, 2, 8, SSM_GROUP, SSM_STATE)
        return jnp.einsum('ahgcp,gk->akphgc', c6, eye8).reshape(-1, SSM_HALF_STATE, SSM_PAIR_W)

    b_all = jnp.concatenate([bmat(bb_re_t), bmat(bb_im_t)], axis=2).astype(BF16)
    c_all = jnp.concatenate([cmat(c_re), cmat(-c_im)], axis=1).astype(BF16)
    return b_all, c_all


def _ssm_a_rows(a, nseq):
    a3 = a.reshape(-1, 1, 2, SSM_HALF_STATE)
    return jnp.broadcast_to(a3, (a3.shape[0], nseq, 2, SSM_HALF_STATE)).reshape(
        a3.shape[0], nseq * 2, SSM_HALF_STATE)


OUT_TM = 256


OUT_PROMPT_TILES = N_PROMPT // OUT_TM


def _merge_out_kernel(*refs, split_x):
    nx = 2 if split_x else 1
    x_refs = refs[:nx]
    (ysp_ref, yss_ref, yap_ref, yas_ref, ymp_ref, yms_ref, g0_ref, g1_ref, g2_ref,
     ws_ref, wa_ref, wm_ref, wo_ref, gp_ref, *o_refs) = refs[nx:]
    i = pl.program_id(0)
    pick = lambda prompt_ref, sample_ref: jnp.where(i < OUT_PROMPT_TILES, prompt_ref[...], sample_ref[...])
    x = pick(*x_refs) if split_x else x_refs[0][...]
    merged = (jax.nn.sigmoid(g0_ref[...]) * _dot(pick(ysp_ref, yss_ref), ws_ref[...])
              + jax.nn.sigmoid(g1_ref[...]) * _dot(pick(yap_ref, yas_ref), wa_ref[...])
              + jax.nn.sigmoid(g2_ref[...]) * _dot(pick(ymp_ref, yms_ref), wm_ref[...]))
    out = _dot(merged.astype(BF16), wo_ref[...])
    y = x + _rms(out, gp_ref[...])
    if len(o_refs) == 1:
        o_refs[0][...] = y
    else:
        @pl.when(i < OUT_PROMPT_TILES)
        def _():
            o_refs[0][...] = y

        @pl.when(i == OUT_PROMPT_TILES)
        def _():
            o_refs[1][...] = y


def _merge_out(x, y_ssm, y_mla, y_mem, p, w_ssm_o, w_mla_o, w_mem_o, w_out, g_post, l, split):
    tm = OUT_TM
    assert N_SAMPLE == tm
    split_x = isinstance(x, tuple)
    row = lambda c: (lambda i: (i, c))
    const = lambda i: (0, 0)
    prompt_row = lambda i: (jnp.minimum(i, OUT_PROMPT_TILES - 1), 0)
    resident = lambda shape: pl.BlockSpec(shape, lambda i: (l, 0), pipeline_mode=pl.Buffered(1))
    branch = lambda width: [pl.BlockSpec((tm, width), prompt_row), pl.BlockSpec((tm, width), const)]
    if split:
        out_specs = [pl.BlockSpec((tm, D_MODEL), prompt_row),
                     pl.BlockSpec((tm, D_MODEL), const)]
        out_shape = [jax.ShapeDtypeStruct((N_PROMPT, D_MODEL), F32),
                     jax.ShapeDtypeStruct((N_SAMPLE, D_MODEL), F32)]
    else:
        out_specs = pl.BlockSpec((tm, D_MODEL), row(0))
        out_shape = jax.ShapeDtypeStruct((N_TOK, D_MODEL), F32)
    x_specs = branch(D_MODEL) if split_x else [pl.BlockSpec((tm, D_MODEL), row(0))]
    x_args = list(x) if split_x else [x]
    return pl.pallas_call(
        functools.partial(_merge_out_kernel, split_x=split_x),
        grid=(N_TOK // tm,),
        in_specs=[*x_specs,
                  *branch(SSM_WIDTH), *branch(MLA_WIDTH), *branch(MEM_WIDTH),
                  *[pl.BlockSpec((tm, D_MODEL), row(P_GATE // D_MODEL + k)) for k in range(3)],
                  resident((SSM_WIDTH, D_MODEL)),
                  resident((MLA_WIDTH, D_MODEL)),
                  resident((MEM_WIDTH, D_MODEL)),
                  resident((D_MODEL, D_MODEL)),
                  pl.BlockSpec((1, D_MODEL), const)],
        out_specs=out_specs,
        out_shape=out_shape,
        compiler_params=_params(("arbitrary",)),
        name="merge_out",
    )(*x_args, *y_ssm, *y_mla, *y_mem, p, p, p, w_ssm_o, w_mla_o, w_mem_o, w_out, g_post)


def _rot_half_cols(w):
    half = w.shape[-1] // 2
    return jnp.concatenate([w[..., half:], w[..., :half]], axis=-1)


def _stack_rows(w):
    return w.astype(BF16).reshape(w.shape[0] * w.shape[1], w.shape[2])


def _prep_small_weights(w_in_t, w_uq, w_uk, w_uv):
    wk = jnp.swapaxes(w_in_t[:, W_IN_K_ROPE:W_IN_K_ROPE + QK_ROPE, :], 1, 2)
    w_kr = _stack_rows(jnp.concatenate([wk, _rot_half_cols(wk)], axis=-1))
    uq = w_uq.reshape(DEPTH, Q_LORA, MLA_HEADS, QK_NOPE + QK_ROPE)
    uq_nope = uq[..., :QK_NOPE].reshape(DEPTH, Q_LORA, MLA_WIDTH)
    uq_rope = uq[..., QK_NOPE:]
    uq_rr = jnp.concatenate([uq_rope, _rot_half_cols(uq_rope)], axis=-1).reshape(DEPTH, Q_LORA, MLA_WIDTH)
    w_uq2 = jnp.concatenate([uq_nope, uq_rr], axis=-1)
    w_ukv = jnp.concatenate([w_uk, w_uv], axis=-1)
    return w_kr, _stack_rows(w_uq2), _stack_rows(w_ukv)


def _rope_table():
    half = QK_ROPE // 2
    pos = jnp.concatenate([jnp.arange(SEQ, dtype=jnp.int32),
                           jnp.tile(PAST_LEN + jnp.arange(DEC_SEQ, dtype=jnp.int32), DEC_BATCH)])
    inv = ROPE_THETA ** (-jnp.arange(half, dtype=F32) / half)
    ang = pos.astype(F32)[:, None] * inv[None, :]
    cos, sin = jnp.cos(ang), jnp.sin(ang)
    return jnp.concatenate([cos, cos, -sin, sin], axis=-1)


def kernel(x_prompt, x_sample, cache_mla_ckv, cache_mla_krope, cache_mem_k, cache_mem_v, state_ssm_re, state_ssm_im, mem_prompt, norm_pre, w_in, ssm_a_re, ssm_a_im, ssm_log_dt, ssm_b_re, ssm_b_im, ssm_c_re, ssm_c_im, ssm_d, w_glu, mla_q_norm, w_uq, mla_kv_norm, w_uk, w_uv, w_mem_k, w_mem_v, w_ssm_o, w_mla_o, w_mem_o, w_out, norm_post):
    w_in_t = jnp.swapaxes(w_in, 1, 2)
    w_in_rows = w_in_t.reshape(DEPTH * W_IN_COLS, D_MODEL)
    w_kr, w_uq2, w_ukv = _prep_small_weights(w_in_t, w_uq, w_uk, w_uv)
    w_glu_rows = w_glu.reshape(DEPTH * SSM_WIDTH, 2 * SSM_WIDTH)
    w_ssm_o_b, w_mla_o_b, w_mem_o_b, w_out_b = (_stack_rows(w) for w in (w_ssm_o, w_mla_o, w_mem_o, w_out))
    w_mem_kv = _stack_rows(jnp.concatenate([w_mem_k, w_mem_v], axis=-1))
    cs = _rope_table()
    mem_rows = mem_prompt.reshape(BATCH * N_MEM, D_MODEL)
    ckv_cache = cache_mla_ckv.reshape(DEPTH * DEC_BATCH * PAST_LEN, KV_LORA)
    krc_t = jnp.swapaxes(cache_mla_krope, 2, 3).reshape(DEPTH * DEC_BATCH * QK_ROPE, PAST_LEN)
    mem_k_cache = cache_mem_k.reshape(DEPTH * DEC_BATCH * N_MEM * MEM_HEADS, MEM_HEAD_DIM)
    mem_v_cache = cache_mem_v.reshape(DEPTH * DEC_BATCH * N_MEM * MEM_HEADS, MEM_HEAD_DIM)

    zeros_state = jnp.zeros((BATCH, SSM_GROUPS, SSM_STATE), F32)
    x = (x_prompt.reshape(N_PROMPT, D_MODEL), x_sample.reshape(N_SAMPLE, D_MODEL))

    mk_all, mv_all = _mem_kv(mem_rows, w_mem_kv)

    lg = DEPTH * SSM_GROUPS
    flat = lambda a: a.reshape((lg,) + a.shape[2:])
    ab_re, ab_im, bb_re_t, bb_im_t = _s5_discretize(
        flat(ssm_a_re), flat(ssm_a_im), ssm_log_dt.reshape(lg),
        flat(jnp.swapaxes(ssm_b_re, 2, 3)), flat(jnp.swapaxes(ssm_b_im, 2, 3)))
    bmat, cmat = _ssm_mats(bb_re_t, bb_im_t, flat(ssm_c_re), flat(ssm_c_im))
    a_rows_p = (_ssm_a_rows(ab_re, BATCH), _ssm_a_rows(ab_im, BATCH))
    a_rows_s = (_ssm_a_rows(ab_re, DEC_BATCH), _ssm_a_rows(ab_im, DEC_BATCH))
    h0_p = _ssm_state_in(zeros_state[None], zeros_state[None])
    h0_s = _ssm_state_in(state_ssm_re, state_ssm_im)

    ht_p, ht_s = [], []
    stacked = None
    for l in range(DEPTH):
        p, kr = _in_proj(x, norm_pre[l][None], w_in_rows, w_kr, l)

        ys_p, ht = _ssm(p, ssm_d[l][None], w_glu_rows, bmat, cmat, *a_rows_p, h0_p,
                        nseq=BATCH, s=SEQ, tt=SSM_TT, row0=0, l=l, h0_layer=0, name="ssm_prompt")
        ht_p.append(ht)
        ys_s, ht = _ssm(p, ssm_d[l][None], w_glu_rows, bmat, cmat, *a_rows_s, h0_s,
                        nseq=DEC_BATCH, s=DEC_SEQ, tt=DEC_SEQ, row0=N_PROMPT, l=l, h0_layer=l, name="ssm_sample")
        ht_s.append(ht)
        y_ssm = (ys_p.reshape(N_PROMPT, SSM_WIDTH), ys_s.reshape(N_SAMPLE, SSM_WIDTH))

        q, k, v, krp, *stacked = _mla_prep(p, kr, cs, mla_q_norm[l][None], mla_kv_norm[l][None],
                                           w_uq2, w_ukv, l, stacked)
        y_mla = (_attn_prompt(q, k, krp, v, p),
                 _attn_sample(q, ckv_cache, krc_t, stacked[1], krp, w_ukv, p, l))

        y_mem = (_mem_attn(p, mk_all, mv_all, nb=BATCH, seq=SEQ, tq=1024, row0=0, mem_blk0=l * BATCH,
                           name="mem_attn_prompt"),
                 _mem_attn(p, mem_k_cache, mem_v_cache, nb=DEC_BATCH, seq=DEC_SEQ, tq=DEC_SEQ, row0=N_PROMPT,
                           mem_blk0=l * DEC_BATCH, name="mem_attn_sample"))

        x = _merge_out(x, y_ssm, y_mla, y_mem, p, w_ssm_o_b, w_mla_o_b, w_mem_o_b, w_out_b,
                       norm_post[l][None], l, split=l == DEPTH - 1)

    x_p, x_s = x
    ckv_p, ckv_s, kro_p, kro_s = stacked
    p_re, p_im = _ssm_state_out(jnp.stack(ht_p), BATCH)
    s_re, s_im = _ssm_state_out(jnp.stack(ht_s), DEC_BATCH)
    mem_shape = (DEPTH, BATCH, N_MEM, MEM_HEADS, MEM_HEAD_DIM)
    return (x_p.reshape(BATCH, SEQ, D_MODEL), x_s.reshape(DEC_BATCH, DEC_SEQ, D_MODEL),
            p_re, p_im,
            ckv_p.reshape(DEPTH, BATCH, SEQ, KV_LORA), kro_p.reshape(DEPTH, BATCH, SEQ, QK_ROPE),
            mk_all.reshape(mem_shape), mv_all.reshape(mem_shape),
            s_re, s_im,
            ckv_s.reshape(DEPTH, DEC_BATCH, DEC_SEQ, KV_LORA), kro_s.reshape(DEPTH, DEC_BATCH, DEC_SEQ, QK_ROPE))
```

```python
import functools
import math

import jax
import jax.numpy as jnp
import numpy as np
from jax import lax
from jax.experimental import pallas as pl
from jax.experimental.pallas import tpu as pltpu

D_MODEL = 2048
BATCH = 4
SEQ = 2048
DEPTH = 4
DEC_BATCH = 8
DEC_SEQ = 32
PAST_LEN = 2048
CHUNK = 64
N_MEM = 256
EPS = 1e-6
NEG_INF = -1e30
SSM_WIDTH = 1024
SSM_GROUP = 16
SSM_GROUPS = 64
SSM_STATE = 64
MLA_HEADS = 8
QK_NOPE = 128
QK_ROPE = 64
V_HEAD = 128
Q_LORA = 512
KV_LORA = 512
MLA_WIDTH = MLA_HEADS * V_HEAD
ROPE_THETA = 10000.0
MEM_HEADS = 4
MEM_HEAD_DIM = 128
MEM_WIDTH = 512

N_PROMPT = BATCH * SEQ
N_SAMPLE = DEC_BATCH * DEC_SEQ
N_TOK = N_PROMPT + N_SAMPLE

P_GATE = 0
P_U = 6144
P_ZS = 7168
P_CQ = 8192
P_CKV = 8704
P_ZA = 9216
P_QM = 10240
P_ZM = 10752
P_COLS = 11264
W_IN_COLS = 11328

LOG2E = math.log2(math.e)
MLA_SCALE = (QK_NOPE + QK_ROPE) ** -0.5 * LOG2E
MEM_SCALE = MEM_HEAD_DIM ** -0.5 * LOG2E

V7X_VMEM_LIMIT = 52 * 1024 * 1024
LANES = 128

BF16 = jnp.bfloat16
F32 = jnp.float32


def _params(sem, vmem=V7X_VMEM_LIMIT):
    return pltpu.CompilerParams(dimension_semantics=sem, vmem_limit_bytes=vmem)


def _dot(a, b):
    return jnp.dot(a, b, preferred_element_type=F32)


def _dot_nt(a, b):
    return lax.dot_general(a, b, (((1,), (1,)), ((), ())), preferred_element_type=F32)


def _rms(x, g):
    return x * lax.rsqrt(jnp.mean(x * x, axis=-1, keepdims=True) + EPS) * g


def _sigmoid(x):
    return 0.5 + 0.5 * jnp.tanh(0.5 * x)


def _silu(x):
    return x * _sigmoid(x)


def _gelu_tanh(x):
    return 0.5 * x * (1.0 + jnp.tanh(math.sqrt(2.0 / math.pi) * (x + 0.044715 * (x * x * x))))


IN_TM = 1408
IN_TN = 512

W_IN_K_ROPE = 2 * SSM_WIDTH + Q_LORA + KV_LORA
_W_GATE0 = W_IN_K_ROPE + QK_ROPE + MLA_WIDTH + 2 * MEM_WIDTH
_W_UNIT = 64


def _w_src_row(l, j):
    u = lambda c: c // _W_UNIT
    assert all(c % _W_UNIT == 0 for c in (_W_GATE0, IN_TN, W_IN_K_ROPE + QK_ROPE, W_IN_COLS))
    n_gate, n_front = P_U // IN_TN, P_ZA // IN_TN
    start = jnp.where(j < n_gate, u(_W_GATE0) + u(IN_TN) * j,
                      jnp.where(j < n_front, u(IN_TN) * (j - n_gate),
                                u(W_IN_K_ROPE + QK_ROPE) + u(IN_TN) * (j - n_front)))
    return (l * u(W_IN_COLS) + start) * _W_UNIT


IN_TILES = N_TOK // IN_TM
IN_LAST_PROMPT = N_PROMPT - (IN_TILES - 1) * IN_TM


def _in_proj_kernel(*refs, split_x):
    x_refs, (g_ref, wt_ref, wkr_ref, p_ref, kr_ref, h_ref) = refs[:len(refs) - 6], refs[len(refs) - 6:]

    @pl.when(pl.program_id(1) == 0)
    def _():
        if split_x:
            xp_ref, xs_ref = x_refs
            last = pl.program_id(0) == IN_TILES - 1

            @pl.when(jnp.logical_not(last))
            def _():
                h_ref[...] = _rms(xp_ref[...], g_ref[...]).astype(BF16)

            @pl.when(last)
            def _():
                h_ref[:IN_LAST_PROMPT] = _rms(xp_ref[IN_TM - IN_LAST_PROMPT:], g_ref[...]).astype(BF16)
                h_ref[IN_LAST_PROMPT:] = _rms(xs_ref[...], g_ref[...]).astype(BF16)
        else:
            h_ref[...] = _rms(x_refs[0][...], g_ref[...]).astype(BF16)
        kr_ref[...] = _dot(h_ref[...], wkr_ref[...])

    p_ref[...] = _dot_nt(h_ref[...], wt_ref[...].astype(BF16))


def _in_proj(x, g_pre, w_in_t, wkr, l):
    split_x = isinstance(x, tuple)
    if split_x:
        assert IN_TM - IN_LAST_PROMPT == N_SAMPLE and IN_TM % 64 == 0 and (N_PROMPT - IN_TM) % 64 == 0
        clamp = lambda i: jnp.minimum(i * (IN_TM // 64), (N_PROMPT - IN_TM) // 64) * 64
        x_specs = [pl.BlockSpec((pl.Element(IN_TM), pl.Element(D_MODEL)), lambda i, j: (clamp(i), 0)),
                   pl.BlockSpec((N_SAMPLE, D_MODEL), lambda i, j: (0, 0))]
        x_args = list(x)
    else:
        x_specs = [pl.BlockSpec((IN_TM, D_MODEL), lambda i, j: (i, 0))]
        x_args = [x]
    return pl.pallas_call(
        functools.partial(_in_proj_kernel, split_x=split_x),
        grid=(IN_TILES, P_COLS // IN_TN),
        in_specs=x_specs + [
            pl.BlockSpec((1, D_MODEL), lambda i, j: (0, 0)),
            pl.BlockSpec((pl.Element(IN_TN), pl.Element(D_MODEL)), lambda i, j: (_w_src_row(l, j), 0)),
            pl.BlockSpec((D_MODEL, LANES), lambda i, j: (l, 0)),
        ],
        out_specs=[
            pl.BlockSpec((IN_TM, IN_TN), lambda i, j: (i, j)),
            pl.BlockSpec((IN_TM, LANES), lambda i, j: (i, 0)),
        ],
        out_shape=[
            jax.ShapeDtypeStruct((N_TOK, P_COLS), F32),
            jax.ShapeDtypeStruct((N_TOK, LANES), F32),
        ],
        scratch_shapes=[pltpu.VMEM((IN_TM, D_MODEL), BF16)],
        compiler_params=_params(("parallel", "arbitrary")),
        name="in_proj",
    )(*x_args, g_pre, w_in_t, wkr)


PREP_TM = 256
PREP_PROMPT_TILES = N_PROMPT // PREP_TM


def _mla_prep_kernel(*refs, n_alias):
    (cq_ref, ckv_ref, kr_ref, cs_ref, qg_ref, kvg_ref, wuq_ref, wukv_ref,
     q_ref, k_ref, v_ref, krp_ref, ckvp_ref, ckvs_ref, krop_ref, kros_ref) = refs[n_alias:]
    i = pl.program_id(0)
    d = LANES
    cs = cs_ref[...]
    hq = _rms(cq_ref[...], qg_ref[...]).astype(BF16)
    q = _dot(hq, wuq_ref[...])

    ckv = _rms(ckv_ref[...], kvg_ref[...])
    kv = _dot(ckv.astype(BF16), wukv_ref[...])
    k_ref[...] = kv[:, :MLA_WIDTH].astype(BF16)
    v_ref[...] = kv[:, MLA_WIDTH:].astype(BF16)

    tk = kr_ref[...] * cs
    rk = tk + pltpu.roll(tk, QK_ROPE, axis=1)
    lane = lax.broadcasted_iota(jnp.int32, rk.shape, 1)
    krp = jnp.where(lane < QK_ROPE, rk, 0.0).astype(BF16)
    krp_ref[...] = krp

    for h in range(MLA_HEADS):
        t = q[:, MLA_WIDTH + d * h:MLA_WIDTH + d * (h + 1)] * cs
        q_ref[:, 2 * d * h:2 * d * h + d] = (q[:, d * h:d * (h + 1)] * MLA_SCALE).astype(BF16)
        q_ref[:, 2 * d * h + d:2 * d * (h + 1)] = ((t + pltpu.roll(t, QK_ROPE, axis=1)) * MLA_SCALE).astype(BF16)

    @pl.when(i < PREP_PROMPT_TILES)
    def _():
        ckvp_ref[...] = ckv
        krop_ref[...] = rk[:, :QK_ROPE]

    @pl.when(i == PREP_PROMPT_TILES)
    def _():
        ckvs_ref[...] = ckv
        kros_ref[...] = rk[:, :QK_ROPE]


def _mla_prep(p, kr, cs, qg, kvg, wuq, wukv, l, stacked):
    tm = PREP_TM
    row = lambda c: (lambda i: (i, c))
    const = lambda i: (0, 0)
    layer = lambda i: (l, 0)
    prompt_blk = lambda i: (l * PREP_PROMPT_TILES + jnp.minimum(i, PREP_PROMPT_TILES - 1), 0)
    n_alias = 0 if stacked is None else 4
    seq_tiles = SEQ // tm
    cs_blk = lambda i: (jnp.where(i < PREP_PROMPT_TILES, i % seq_tiles, seq_tiles), 0)
    in_specs = [
        pl.BlockSpec((tm, Q_LORA), row(P_CQ // Q_LORA)),
        pl.BlockSpec((tm, KV_LORA), row(P_CKV // KV_LORA)),
        pl.BlockSpec((tm, LANES), row(0)),
        pl.BlockSpec((tm, LANES), cs_blk),
        pl.BlockSpec((1, Q_LORA), const),
        pl.BlockSpec((1, KV_LORA), const),
        pl.BlockSpec((Q_LORA, 2048), layer),
        pl.BlockSpec((KV_LORA, 2048), layer),
    ]
    args = [p, p, kr, cs, qg, kvg, wuq, wukv]
    if stacked is not None:
        in_specs = [pl.BlockSpec(memory_space=pl.ANY)] * 4 + in_specs
        args = list(stacked) + args
    return pl.pallas_call(
        functools.partial(_mla_prep_kernel, n_alias=n_alias),
        grid=(N_TOK // tm,),
        in_specs=in_specs,
        out_specs=[
            pl.BlockSpec((tm, 2 * MLA_WIDTH), row(0)),
            pl.BlockSpec((tm, MLA_WIDTH), row(0)),
            pl.BlockSpec((tm, MLA_WIDTH), row(0)),
            pl.BlockSpec((tm, LANES), row(0)),
            pl.BlockSpec((tm, KV_LORA), prompt_blk),
            pl.BlockSpec((tm, KV_LORA), layer),
            pl.BlockSpec((tm, QK_ROPE), prompt_blk),
            pl.BlockSpec((tm, QK_ROPE), layer),
        ],
        out_shape=[
            jax.ShapeDtypeStruct((N_TOK, 2 * MLA_WIDTH), BF16),
            jax.ShapeDtypeStruct((N_TOK, MLA_WIDTH), BF16),
            jax.ShapeDtypeStruct((N_TOK, MLA_WIDTH), BF16),
            jax.ShapeDtypeStruct((N_TOK, LANES), BF16),
            jax.ShapeDtypeStruct((DEPTH * N_PROMPT, KV_LORA), F32),
            jax.ShapeDtypeStruct((DEPTH * N_SAMPLE, KV_LORA), F32),
            jax.ShapeDtypeStruct((DEPTH * N_PROMPT, QK_ROPE), F32),
            jax.ShapeDtypeStruct((DEPTH * N_SAMPLE, QK_ROPE), F32),
        ],
        input_output_aliases={k: 4 + k for k in range(n_alias)},
        compiler_params=_params(("arbitrary",)),
        name="mla_prep",
    )(*args)


MEMKV_TM = 512


def _mem_kv_kernel(x_ref, w_ref, k_ref, v_ref):
    y = _dot(x_ref[...].astype(BF16), w_ref[...])
    for h in range(MEM_HEADS):
        rows = pl.ds(h, MEMKV_TM, stride=MEM_HEADS)
        k_ref[rows, :] = y[:, MEM_HEAD_DIM * h:MEM_HEAD_DIM * (h + 1)]
        v_ref[rows, :] = y[:, MEM_WIDTH + MEM_HEAD_DIM * h:MEM_WIDTH + MEM_HEAD_DIM * (h + 1)]


def _mem_kv(mem_rows, w_mem_kv):
    tm = MEMKV_TM
    nr = BATCH * N_MEM // tm
    out = pl.BlockSpec((tm * MEM_HEADS, MEM_HEAD_DIM), lambda l, i: (l * nr + i, 0))
    return pl.pallas_call(
        _mem_kv_kernel,
        grid=(DEPTH, nr),
        in_specs=[pl.BlockSpec((tm, D_MODEL), lambda l, i: (i, 0)),
                  pl.BlockSpec((D_MODEL, 2 * MEM_WIDTH), lambda l, i: (l, 0))],
        out_specs=[out, out],
        out_shape=[jax.ShapeDtypeStruct((DEPTH * BATCH * N_MEM * MEM_HEADS, MEM_HEAD_DIM), F32)] * 2,
        compiler_params=_params(("parallel", "parallel")),
        name="mem_kv",
    )(mem_rows, w_mem_kv)


ATT_TQ = 512


def _chunk_mask(q_pos0, k_pos0, nq, nk):
    qp = q_pos0 + lax.broadcasted_iota(jnp.int32, (nq, nk), 0)
    kp = k_pos0 + lax.broadcasted_iota(jnp.int32, (nq, nk), 1)
    sh = CHUNK.bit_length() - 1
    return lax.shift_right_logical(kp, sh) <= lax.shift_right_logical(qp, sh)


ATT_HEADS = 4


def _attn_prompt_kernel(q_ref, kn_ref, kr_ref, v_ref, z_ref, o_ref, k_scr, v_scr):
    tq, d = ATT_TQ, LANES
    for g in range(ATT_HEADS):
        k_scr[g, :, :d] = kn_ref[:, d * g:d * (g + 1)]
        k_scr[g, :, d:] = kr_ref[...]
        v_scr[g, :, :d] = v_ref[:, d * g:d * (g + 1)]
        v_scr[g, :, d:] = jnp.ones((SEQ, d), BF16)
    for qi in range(SEQ // tq):
        lo, hi = qi * tq, (qi + 1) * tq
        for g in range(ATT_HEADS):
            q = q_ref[lo:hi, 2 * d * g:2 * d * (g + 1)]
            sd = jnp.where(_chunk_mask(lo, lo, tq, tq), _dot_nt(q, k_scr[g, lo:hi, :]), NEG_INF)
            m = jnp.max(sd, axis=1, keepdims=True)
            if qi > 0:
                so = _dot_nt(q, k_scr[g, 0:lo, :])
                m = jnp.maximum(m, jnp.max(so, axis=1, keepdims=True))
            acc = _dot(jnp.exp2(sd - m).astype(BF16), v_scr[g, lo:hi, :])
            if qi > 0:
                acc = acc + _dot(jnp.exp2(so - m).astype(BF16), v_scr[g, 0:lo, :])
            z = z_ref[lo:hi, d * g:d * (g + 1)]
            o_ref[lo:hi, d * g:d * (g + 1)] = (acc[:, :d] * (1.0 / acc[:, d:]) * _silu(z)).astype(BF16)


def _attn_prompt(q, kn, krp, v, p):
    w = LANES * ATT_HEADS
    blk = lambda c0: pl.BlockSpec((SEQ, w), lambda b, h: (b, c0 + h))
    return pl.pallas_call(
        _attn_prompt_kernel,
        grid=(BATCH, MLA_HEADS // ATT_HEADS),
        in_specs=[pl.BlockSpec((SEQ, 2 * w), lambda b, h: (b, h)),
                  blk(0), pl.BlockSpec((SEQ, LANES), lambda b, h: (b, 0)), blk(0), blk(P_ZA // w)],
        out_specs=blk(0),
        out_shape=jax.ShapeDtypeStruct((N_PROMPT, MLA_WIDTH), BF16),
        scratch_shapes=[pltpu.VMEM((ATT_HEADS, SEQ, 2 * LANES), BF16), pltpu.VMEM((ATT_HEADS, SEQ, 2 * LANES), BF16)],
        compiler_params=_params(("parallel", "parallel")),
        name="attn_prompt",
    )(q, kn, krp, v, p)


def _attn_sample_kernel(q_ref, ckvc_ref, krct_ref, ckvn_ref, krn_ref, wukv_ref, z_ref, o_ref):
    nh, d = MLA_HEADS, LANES
    qa = jnp.concatenate(
        [_dot_nt(q_ref[:, 2 * d * h:2 * d * h + d], wukv_ref[:, d * h:d * (h + 1)]) for h in range(nh)],
        axis=0).astype(BF16)
    qr = jnp.concatenate([q_ref[:, 2 * d * h + d:2 * d * (h + 1)] for h in range(nh)], axis=0)
    ckvc = ckvc_ref[...].astype(BF16)
    krct = krct_ref[...].astype(BF16)
    ckvn = ckvn_ref[...].astype(BF16)
    rows = nh * DEC_SEQ

    def mask(k_pos0, nk):
        qp = PAST_LEN + (lax.broadcasted_iota(jnp.int32, (rows, nk), 0) & (DEC_SEQ - 1))
        kp = k_pos0 + lax.broadcasted_iota(jnp.int32, (rows, nk), 1)
        sh = CHUNK.bit_length() - 1
        return lax.shift_right_logical(kp, sh) <= lax.shift_right_logical(qp, sh)

    sp = jnp.where(mask(0, PAST_LEN), _dot_nt(qa, ckvc) + _dot(qr[:, :QK_ROPE], krct), NEG_INF)
    sn = jnp.where(mask(PAST_LEN, DEC_SEQ), _dot_nt(qa, ckvn) + _dot_nt(qr, krn_ref[...]), NEG_INF)
    m = jnp.maximum(jnp.max(sp, axis=1, keepdims=True), jnp.max(sn, axis=1, keepdims=True))
    ep, en = jnp.exp2(sp - m), jnp.exp2(sn - m)
    l = jnp.sum(ep, axis=1, keepdims=True) + jnp.sum(en, axis=1, keepdims=True)
    lat = ((_dot(ep.astype(BF16), ckvc) + _dot(en.astype(BF16), ckvn)) * (1.0 / l)).astype(BF16)
    o = jnp.concatenate(
        [_dot(lat[DEC_SEQ * h:DEC_SEQ * (h + 1)], wukv_ref[:, MLA_WIDTH + d * h:MLA_WIDTH + d * (h + 1)])
         for h in range(nh)], axis=1)
    o_ref[...] = (o * _silu(z_ref[...])).astype(BF16)


def _attn_sample(q, ckv_cache, krc_t, ckv_s, krp, wukv, p, l):
    assert DEC_SEQ & (DEC_SEQ - 1) == 0
    r0 = N_PROMPT // DEC_SEQ
    new = lambda width, c: pl.BlockSpec((DEC_SEQ, width), lambda b: (r0 + b, c))
    stream = lambda rows, width: pl.BlockSpec((rows, width), lambda b: (l * DEC_BATCH + b, 0))
    return pl.pallas_call(
        _attn_sample_kernel,
        grid=(DEC_BATCH,),
        in_specs=[new(2 * MLA_WIDTH, 0),
                  stream(PAST_LEN, KV_LORA), stream(QK_ROPE, PAST_LEN),
                  stream(DEC_SEQ, KV_LORA),
                  new(LANES, 0),
                  pl.BlockSpec((KV_LORA, 2048), lambda b: (l, 0)),
                  new(MLA_WIDTH, P_ZA // MLA_WIDTH)],
        out_specs=pl.BlockSpec((DEC_SEQ, MLA_WIDTH), lambda b: (b, 0)),
        out_shape=jax.ShapeDtypeStruct((N_SAMPLE, MLA_WIDTH), BF16),
        compiler_params=_params(("parallel",)),
        name="attn_sample",
    )(q, ckv_cache, krc_t, ckv_s, krp, wukv, p)


def _mem_attn_kernel(q_ref, z_ref, k_ref, v_ref, o_ref):
    q = q_ref[...] * MEM_SCALE
    outs = []
    for h in range(MEM_HEADS):
        sl = slice(MEM_HEAD_DIM * h, MEM_HEAD_DIM * (h + 1))
        rows = pl.ds(h, N_MEM, stride=MEM_HEADS)
        s = _dot_nt(q[:, sl].astype(BF16), k_ref[rows, :].astype(BF16))
        m = jnp.max(s, axis=1, keepdims=True)
        e = jnp.exp2(s - m)
        l = jnp.sum(e, axis=1, keepdims=True)
        outs.append(_dot(e.astype(BF16), v_ref[rows, :].astype(BF16)) * (1.0 / l))
    o_ref[...] = (jnp.concatenate(outs, axis=1) * _silu(z_ref[...])).astype(BF16)


def _mem_attn(p, mem_k, mem_v, *, nb, seq, tq, row0, mem_blk0, name):
    seq_blocks = seq // tq
    r0 = row0 // tq
    rowmap = lambda c: (lambda b, i: (r0 + b * seq_blocks + i, c))
    mem = pl.BlockSpec((N_MEM * MEM_HEADS, MEM_HEAD_DIM), lambda b, i: (mem_blk0 + b, 0))
    return pl.pallas_call(
        _mem_attn_kernel,
        grid=(nb, seq_blocks),
        in_specs=[pl.BlockSpec((tq, MEM_WIDTH), rowmap(P_QM // MEM_WIDTH)),
                  pl.BlockSpec((tq, MEM_WIDTH), rowmap(P_ZM // MEM_WIDTH)),
                  mem, mem],
        out_specs=pl.BlockSpec((tq, MEM_WIDTH), lambda b, i: (b * seq_blocks + i, 0)),
        out_shape=jax.ShapeDtypeStruct((nb * seq, MEM_WIDTH), BF16),
        compiler_params=_params(("parallel", "parallel")),
        name=name,
    )(p, p, mem_k, mem_v)


def _s5_disc_kernel(are_ref, aim_ref, ldt_ref, bre_ref, bim_ref, abr_ref, abi_ref, bbr_ref, bbi_ref):
    dt = jnp.exp(ldt_ref[...])
    lr, li = are_ref[...], aim_ref[...]
    mag = jnp.exp(lr * dt)
    ab_re, ab_im = mag * jnp.cos(li * dt), mag * jnp.sin(li * dt)
    den = lr * lr + li * li
    nr, ni = ab_re - 1.0, ab_im
    f_re = (nr * lr + ni * li) / den
    f_im = (ni * lr - nr * li) / den
    abr_ref[...] = ab_re
    abi_ref[...] = ab_im
    br, bi = bre_ref[...], bim_ref[...]
    bbr_ref[...] = f_re * br - f_im * bi
    bbi_ref[...] = f_re * bi + f_im * br


def _s5_discretize(a_re, a_im, log_dt, b_re_t, b_im_t):
    G, P, C = a_re.shape[0], SSM_STATE, SSM_GROUP
    ab_re, ab_im, bb_re_t, bb_im_t = pl.pallas_call(
        _s5_disc_kernel,
        out_shape=[jax.ShapeDtypeStruct((G, 1, P), F32), jax.ShapeDtypeStruct((G, 1, P), F32),
                   jax.ShapeDtypeStruct((G, C, P), F32), jax.ShapeDtypeStruct((G, C, P), F32)],
        name="s5_discretize",
    )(a_re.reshape(G, 1, P), a_im.reshape(G, 1, P), log_dt.reshape(G, 1, 1), b_re_t, b_im_t)
    return ab_re.reshape(G, P), ab_im.reshape(G, P), bb_re_t, bb_im_t


SSM_PAIRS = 4
SSM_PAIR_W = SSM_WIDTH // SSM_PAIRS
SSM_HALF_STATE = 512
SSM_SUB = 32
SSM_TT = 128


def _ssm_kernel(*refs, nseq, tt):
    u_refs, z_refs = refs[:nseq], refs[nseq:2 * nseq]
    (d_ref, wglu_ref, bmat_ref, cmat_ref, are_ref, aim_ref, h0_ref, pin_ref, pout_ref,
     y_ref, ht_ref, st_ref, hb_ref, g_ref) = refs[2 * nseq:]
    q = 2 * nseq
    r = nseq * SSM_SUB
    hs = SSM_HALF_STATE
    c = pl.program_id(0)

    @pl.when(c == 0)
    def _():
        st_ref[...] = h0_ref[...]

    lane = lax.broadcasted_iota(jnp.int32, (r, SSM_WIDTH), 1)
    low = (lane & LANES) == 0
    rows2 = lax.broadcasted_iota(jnp.int32, (q * SSM_SUB, SSM_WIDTH), 0)
    lane2 = lax.broadcasted_iota(jnp.int32, (q * SSM_SUB, SSM_WIDTH), 1)
    keep = ((rows2 & 1) == 0) == ((lane2 & LANES) == 0)

    def sub_tile(s, carry):
        t0 = pl.multiple_of(s * SSM_SUB, SSM_SUB)
        u = jnp.concatenate([u_refs[j][pl.ds(t0, SSM_SUB), :] for j in range(nseq)], axis=0)
        ub = u.astype(BF16)
        zero = jnp.zeros_like(ub)
        stacked = jnp.concatenate([jnp.where(low, ub, zero), jnp.where(low, zero, ub)], axis=0)
        lall = _dot(pin_ref[...], stacked).astype(BF16)
        for p in range(SSM_PAIRS):
            hb_ref[p] = _dot(lall[:, SSM_PAIR_W * p:SSM_PAIR_W * (p + 1)], bmat_ref[p])
        for p in range(SSM_PAIRS):
            ar = are_ref[p]
            ai = aim_ref[p]

            def step(t, hc, p=p, ar=ar, ai=ai):
                hr, hi = hc
                r0 = pl.multiple_of(t * q, q)
                bur = hb_ref[p, pl.ds(r0, q), 0:hs]
                bui = hb_ref[p, pl.ds(r0, q), hs:2 * hs]
                nr = ar * hr - ai * hi + bur
                ni = ar * hi + ai * hr + bui
                hb_ref[p, pl.ds(r0, q), 0:hs] = nr
                hb_ref[p, pl.ds(r0, q), hs:2 * hs] = ni
                return nr, ni

            hr, hi = lax.fori_loop(0, SSM_SUB, step, (st_ref[p, :, 0:hs], st_ref[p, :, hs:2 * hs]), unroll=True)
            st_ref[p, :, 0:hs] = hr
            st_ref[p, :, hs:2 * hs] = hi
        y2 = jnp.concatenate([_dot(hb_ref[p].astype(BF16), cmat_ref[p]) for p in range(SSM_PAIRS)], axis=1)
        zf = jnp.where(keep, y2, 0.0)
        z_hi = zf.astype(BF16)
        r1 = zf - z_hi.astype(F32)
        z_mid = r1.astype(BF16)
        z_lo = (r1 - z_mid.astype(F32)).astype(BF16)
        pout = pout_ref[...]
        yn = _dot(pout, z_hi) + _dot(pout, z_mid) + _dot(pout, z_lo)
        g = _gelu_tanh(yn + d_ref[...] * u).astype(BF16)
        for j in range(nseq):
            g_ref[pl.ds(pl.multiple_of(j * tt + t0, SSM_SUB), SSM_SUB), :] = g[j * SSM_SUB:(j + 1) * SSM_SUB]
        return carry

    lax.fori_loop(0, tt // SSM_SUB, sub_tile, 0)

    g = g_ref[...]
    ga = _dot(g, wglu_ref[:, :SSM_WIDTH].astype(BF16))
    gb = _dot(g, wglu_ref[:, SSM_WIDTH:].astype(BF16))
    z = jnp.concatenate([z_refs[j][...] for j in range(nseq)], axis=0)
    out = (ga * _sigmoid(gb) * _silu(z)).astype(BF16)
    for j in range(nseq):
        y_ref[j] = out[j * tt:(j + 1) * tt]

    @pl.when(c == pl.num_programs(0) - 1)
    def _():
        ht_ref[...] = st_ref[...]


def _ssm_perms(nseq):
    r, q = nseq * SSM_SUB, 2 * nseq
    pin = np.zeros((2 * r, 2 * r), np.float32)
    pout = np.zeros((r, 2 * r), np.float32)
    for t in range(SSM_SUB):
        for j in range(nseq):
            for h in range(2):
                pin[t * q + 2 * j + h, h * r + j * SSM_SUB + t] = 1.0
                pout[j * SSM_SUB + t, t * q + 2 * j + h] = 1.0
    return jnp.asarray(pin, BF16), jnp.asarray(pout, BF16)


def _ssm(p, d, w_glu, bmat, cmat, a_re, a_im, h0, *, nseq, s, tt, row0, l, h0_layer, name):
    nt = s // tt
    q = 2 * nseq
    rb0 = row0 // tt
    pin, pout = _ssm_perms(nseq)
    seq_spec = lambda j, col: pl.BlockSpec((tt, SSM_WIDTH), lambda c: (rb0 + j * nt + c, col))
    whole = lambda a: pl.BlockSpec(a.shape, lambda c: (0,) * a.ndim)
    layer = lambda a, lay: pl.BlockSpec((SSM_PAIRS,) + a.shape[1:], lambda c: (lay, 0, 0))
    in_specs = ([seq_spec(j, P_U // SSM_WIDTH) for j in range(nseq)]
                + [seq_spec(j, P_ZS // SSM_WIDTH) for j in range(nseq)]
                + [pl.BlockSpec((1, SSM_WIDTH), lambda c: (0, 0)),
                   pl.BlockSpec((SSM_WIDTH, 2 * SSM_WIDTH), lambda c: (l, 0), pipeline_mode=pl.Buffered(1)),
                   layer(bmat, l), layer(cmat, l), layer(a_re, l), layer(a_im, l), layer(h0, h0_layer),
                   whole(pin), whole(pout)])
    return pl.pallas_call(
        functools.partial(_ssm_kernel, nseq=nseq, tt=tt),
        grid=(nt,),
        in_specs=in_specs,
        out_specs=[pl.BlockSpec((nseq, tt, SSM_WIDTH), lambda c: (0, c, 0)),
                   pl.BlockSpec((SSM_PAIRS, q, 2 * SSM_HALF_STATE), lambda c: (0, 0, 0))],
        out_shape=[jax.ShapeDtypeStruct((nseq, s, SSM_WIDTH), BF16),
                   jax.ShapeDtypeStruct((SSM_PAIRS, q, 2 * SSM_HALF_STATE), F32)],
        scratch_shapes=[pltpu.VMEM((SSM_PAIRS, q, 2 * SSM_HALF_STATE), F32),
                        pltpu.VMEM((SSM_PAIRS, q * SSM_SUB, 2 * SSM_HALF_STATE), F32),
                        pltpu.VMEM((nseq * tt, SSM_WIDTH), BF16)],
        compiler_params=_params(("arbitrary",)),
        name=name,
    )(*([p] * (2 * nseq)), d, w_glu, bmat, cmat, a_re, a_im, h0, pin, pout)


def _ssm_state_in(h_re, h_im):
    nl, nseq = h_re.shape[:2]

    def arr(h):
        return h.reshape(nl, nseq, SSM_PAIRS, 2, SSM_HALF_STATE).transpose(0, 2, 1, 3, 4).reshape(
            nl * SSM_PAIRS, nseq * 2, SSM_HALF_STATE)
    return jnp.concatenate([arr(h_re), arr(h_im)], axis=-1)


def _ssm_state_out(ht, nseq):
    nl = ht.shape[0]

    def arr(h):
        return h.reshape(nl, SSM_PAIRS, nseq, 2, 8, SSM_STATE).transpose(0, 2, 1, 3, 4, 5).reshape(
            nl, nseq, SSM_GROUPS, SSM_STATE)
    return arr(ht[..., :SSM_HALF_STATE]), arr(ht[..., SSM_HALF_STATE:])


def _ssm_mats_kernel(bre_ref, bim_ref, cre_ref, cim_ref, t_ref, b_ref, ct_ref):
    def expand(x):
        rep = _dot(x.astype(BF16), t_ref[...])
        g_row = lax.shift_right_logical(lax.broadcasted_iota(jnp.int32, rep.shape, 0), 4) & 7
        g_col = lax.shift_right_logical(lax.broadcasted_iota(jnp.int32, rep.shape, 1), 6)
        return jnp.where(g_row == g_col, rep, 0.0).astype(BF16)

    b_ref[0] = jnp.concatenate([expand(bre_ref[0]), expand(bim_ref[0])], axis=1)
    ct_ref[0] = jnp.concatenate([expand(cre_ref[0]), expand(-cim_ref[0])], axis=1)


def _ssm_mats(bb_re_t, bb_im_t, c_re, c_im):
    assert SSM_GROUP == 16 and SSM_STATE == 64 and SSM_GROUPS // (2 * SSM_PAIRS) == 8
    pairs = bb_re_t.shape[0] * SSM_GROUP // SSM_PAIR_W
    rows = lambda a: a.reshape(pairs, SSM_PAIR_W, SSM_STATE)
    tile_eye = jnp.tile(jnp.eye(SSM_STATE, dtype=BF16), (1, SSM_HALF_STATE // SSM_STATE))
    blk = pl.BlockSpec((1, SSM_PAIR_W, SSM_STATE), lambda i: (i, 0, 0))
    out = pl.BlockSpec((1, SSM_PAIR_W, 2 * SSM_HALF_STATE), lambda i: (i, 0, 0))
    b_all, ct_all = pl.pallas_call(
        _ssm_mats_kernel,
        grid=(pairs,),
        in_specs=[blk, blk, blk, blk, pl.BlockSpec(tile_eye.shape, lambda i: (0, 0))],
        out_specs=[out, out],
        out_shape=[jax.ShapeDtypeStruct((pairs, SSM_PAIR_W, 2 * SSM_HALF_STATE), BF16)] * 2,
        compiler_params=_params(("parallel",)),
        name="ssm_mats",
    )(rows(bb_re_t), rows(bb_im_t), rows(c_re), rows(c_im), tile_eye)
    return b_all, jnp.swapaxes(ct_all, 1, 2)


def _ssm_a_rows(a, nseq):
    a3 = a.reshape(-1, 1, 2, SSM_HALF_STATE)
    return jnp.broadcast_to(a3, (a3.shape[0], nseq, 2, SSM_HALF_STATE)).reshape(
        a3.shape[0], nseq * 2, SSM_HALF_STATE)


OUT_TM = 256


OUT_PROMPT_TILES = N_PROMPT // OUT_TM


def _merge_out_kernel(*refs, split_x):
    nx = 2 if split_x else 1
    x_refs = refs[:nx]
    (ysp_ref, yss_ref, yap_ref, yas_ref, ymp_ref, yms_ref, g0_ref, g1_ref, g2_ref,
     ws_ref, wa_ref, wm_ref, wo_ref, gp_ref, *o_refs) = refs[nx:]
    i = pl.program_id(0)
    pick = lambda prompt_ref, sample_ref: jnp.where(i < OUT_PROMPT_TILES, prompt_ref[...], sample_ref[...])
    x = pick(*x_refs) if split_x else x_refs[0][...]
    merged = (jax.nn.sigmoid(g0_ref[...]) * _dot(pick(ysp_ref, yss_ref), ws_ref[...])
              + jax.nn.sigmoid(g1_ref[...]) * _dot(pick(yap_ref, yas_ref), wa_ref[...])
              + jax.nn.sigmoid(g2_ref[...]) * _dot(pick(ymp_ref, yms_ref), wm_ref[...]))
    out = _dot(merged.astype(BF16), wo_ref[...])
    y = x + _rms(out, gp_ref[...])
    if len(o_refs) == 1:
        o_refs[0][...] = y
    else:
        @pl.when(i < OUT_PROMPT_TILES)
        def _():
            o_refs[0][...] = y

        @pl.when(i == OUT_PROMPT_TILES)
        def _():
            o_refs[1][...] = y


def _merge_out(x, y_ssm, y_mla, y_mem, p, w_ssm_o, w_mla_o, w_mem_o, w_out, g_post, l, split):
    tm = OUT_TM
    assert N_SAMPLE == tm
    split_x = isinstance(x, tuple)
    row = lambda c: (lambda i: (i, c))
    const = lambda i: (0, 0)
    prompt_row = lambda i: (jnp.minimum(i, OUT_PROMPT_TILES - 1), 0)
    resident = lambda shape: pl.BlockSpec(shape, lambda i: (l, 0), pipeline_mode=pl.Buffered(1))
    branch = lambda width: [pl.BlockSpec((tm, width), prompt_row), pl.BlockSpec((tm, width), const)]
    if split:
        out_specs = [pl.BlockSpec((tm, D_MODEL), prompt_row),
                     pl.BlockSpec((tm, D_MODEL), const)]
        out_shape = [jax.ShapeDtypeStruct((N_PROMPT, D_MODEL), F32),
                     jax.ShapeDtypeStruct((N_SAMPLE, D_MODEL), F32)]
    else:
        out_specs = pl.BlockSpec((tm, D_MODEL), row(0))
        out_shape = jax.ShapeDtypeStruct((N_TOK, D_MODEL), F32)
    x_specs = branch(D_MODEL) if split_x else [pl.BlockSpec((tm, D_MODEL), row(0))]
    x_args = list(x) if split_x else [x]
    return pl.pallas_call(
        functools.partial(_merge_out_kernel, split_x=split_x),
        grid=(N_TOK // tm,),
        in_specs=[*x_specs,
                  *branch(SSM_WIDTH), *branch(MLA_WIDTH), *branch(MEM_WIDTH),
                  *[pl.BlockSpec((tm, D_MODEL), row(P_GATE // D_MODEL + k)) for k in range(3)],
                  resident((SSM_WIDTH, D_MODEL)),
                  resident((MLA_WIDTH, D_MODEL)),
                  resident((MEM_WIDTH, D_MODEL)),
                  resident((D_MODEL, D_MODEL)),
                  pl.BlockSpec((1, D_MODEL), const)],
        out_specs=out_specs,
        out_shape=out_shape,
        compiler_params=_params(("arbitrary",)),
        name="merge_out",
    )(*x_args, *y_ssm, *y_mla, *y_mem, p, p, p, w_ssm_o, w_mla_o, w_mem_o, w_out, g_post)


def _rot_half_cols(w):
    half = w.shape[-1] // 2
    return jnp.concatenate([w[..., half:], w[..., :half]], axis=-1)


def _stack_rows(w):
    return w.astype(BF16).reshape(w.shape[0] * w.shape[1], w.shape[2])


def _prep_small_weights(w_in_t, w_uq, w_uk, w_uv):
    wk = jnp.swapaxes(w_in_t[:, W_IN_K_ROPE:W_IN_K_ROPE + QK_ROPE, :], 1, 2)
    w_kr = _stack_rows(jnp.concatenate([wk, _rot_half_cols(wk)], axis=-1))
    uq = w_uq.reshape(DEPTH, Q_LORA, MLA_HEADS, QK_NOPE + QK_ROPE)
    uq_nope = uq[..., :QK_NOPE].reshape(DEPTH, Q_LORA, MLA_WIDTH)
    uq_rope = uq[..., QK_NOPE:]
    uq_rr = jnp.concatenate([uq_rope, _rot_half_cols(uq_rope)], axis=-1).reshape(DEPTH, Q_LORA, MLA_WIDTH)
    w_uq2 = jnp.concatenate([uq_nope, uq_rr], axis=-1)
    w_ukv = jnp.concatenate([w_uk, w_uv], axis=-1)
    return w_kr, _stack_rows(w_uq2), _stack_rows(w_ukv)


def _rope_table():
    half = QK_ROPE // 2
    pos = jnp.concatenate([jnp.arange(SEQ, dtype=jnp.int32),
                           jnp.tile(PAST_LEN + jnp.arange(DEC_SEQ, dtype=jnp.int32), DEC_BATCH)])
    inv = ROPE_THETA ** (-jnp.arange(half, dtype=F32) / half)
    ang = pos.astype(F32)[:, None] * inv[None, :]
    cos, sin = jnp.cos(ang), jnp.sin(ang)
    return jnp.concatenate([cos, cos, -sin, sin], axis=-1)


def kernel(x_prompt, x_sample, cache_mla_ckv, cache_mla_krope, cache_mem_k, cache_mem_v, state_ssm_re, state_ssm_im, mem_prompt, norm_pre, w_in, ssm_a_re, ssm_a_im, ssm_log_dt, ssm_b_re, ssm_b_im, ssm_c_re, ssm_c_im, ssm_d, w_glu, mla_q_norm, w_uq, mla_kv_norm, w_uk, w_uv, w_mem_k, w_mem_v, w_ssm_o, w_mla_o, w_mem_o, w_out, norm_post):
    w_in_t = jnp.swapaxes(w_in, 1, 2)
    w_in_rows = w_in_t.reshape(DEPTH * W_IN_COLS, D_MODEL)
    w_kr, w_uq2, w_ukv = _prep_small_weights(w_in_t, w_uq, w_uk, w_uv)
    w_glu_rows = w_glu.reshape(DEPTH * SSM_WIDTH, 2 * SSM_WIDTH)
    w_ssm_o_b, w_mla_o_b, w_mem_o_b, w_out_b = (_stack_rows(w) for w in (w_ssm_o, w_mla_o, w_mem_o, w_out))
    w_mem_kv = _stack_rows(jnp.concatenate([w_mem_k, w_mem_v], axis=-1))
    cs = _rope_table()
    mem_rows = mem_prompt.reshape(BATCH * N_MEM, D_MODEL)
    ckv_cache = cache_mla_ckv.reshape(DEPTH * DEC_BATCH * PAST_LEN, KV_LORA)
    krc_t = jnp.swapaxes(cache_mla_krope, 2, 3).reshape(DEPTH * DEC_BATCH * QK_ROPE, PAST_LEN)
    mem_k_cache = cache_mem_k.reshape(DEPTH * DEC_BATCH * N_MEM * MEM_HEADS, MEM_HEAD_DIM)
    mem_v_cache = cache_mem_v.reshape(DEPTH * DEC_BATCH * N_MEM * MEM_HEADS, MEM_HEAD_DIM)

    zeros_state = jnp.zeros((BATCH, SSM_GROUPS, SSM_STATE), F32)
    x = (x_prompt.reshape(N_PROMPT, D_MODEL), x_sample.reshape(N_SAMPLE, D_MODEL))

    mk_all, mv_all = _mem_kv(mem_rows, w_mem_kv)

    lg = DEPTH * SSM_GROUPS
    flat = lambda a: a.reshape((lg,) + a.shape[2:])
    ab_re, ab_im, bb_re_t, bb_im_t = _s5_discretize(
        flat(ssm_a_re), flat(ssm_a_im), ssm_log_dt.reshape(lg),
        flat(jnp.swapaxes(ssm_b_re, 2, 3)), flat(jnp.swapaxes(ssm_b_im, 2, 3)))
    bmat, cmat = _ssm_mats(bb_re_t, bb_im_t, flat(ssm_c_re), flat(ssm_c_im))
    a_rows_p = (_ssm_a_rows(ab_re, BATCH), _ssm_a_rows(ab_im, BATCH))
    a_rows_s = (_ssm_a_rows(ab_re, DEC_BATCH), _ssm_a_rows(ab_im, DEC_BATCH))
    h0_p = _ssm_state_in(zeros_state[None], zeros_state[None])
    h0_s = _ssm_state_in(state_ssm_re, state_ssm_im)

    ht_p, ht_s = [], []
    stacked = None
    for l in range(DEPTH):
        p, kr = _in_proj(x, norm_pre[l][None], w_in_rows, w_kr, l)

        ys_p, ht = _ssm(p, ssm_d[l][None], w_glu_rows, bmat, cmat, *a_rows_p, h0_p,
                        nseq=BATCH, s=SEQ, tt=SSM_TT, row0=0, l=l, h0_layer=0, name="ssm_prompt")
        ht_p.append(ht)
        ys_s, ht = _ssm(p, ssm_d[l][None], w_glu_rows, bmat, cmat, *a_rows_s, h0_s,
                        nseq=DEC_BATCH, s=DEC_SEQ, tt=DEC_SEQ, row0=N_PROMPT, l=l, h0_layer=l, name="ssm_sample")
        ht_s.append(ht)
        y_ssm = (ys_p.reshape(N_PROMPT, SSM_WIDTH), ys_s.reshape(N_SAMPLE, SSM_WIDTH))

        q, k, v, krp, *stacked = _mla_prep(p, kr, cs, mla_q_norm[l][None], mla_kv_norm[l][None],
                                           w_uq2, w_ukv, l, stacked)
        y_mla = (_attn_prompt(q, k, krp, v, p),
                 _attn_sample(q, ckv_cache, krc_t, stacked[1], krp, w_ukv, p, l))

        y_mem = (_mem_attn(p, mk_all, mv_all, nb=BATCH, seq=SEQ, tq=1024, row0=0, mem_blk0=l * BATCH,
                           name="mem_attn_prompt"),
                 _mem_attn(p, mem_k_cache, mem_v_cache, nb=DEC_BATCH, seq=DEC_SEQ, tq=DEC_SEQ, row0=N_PROMPT,
                           mem_blk0=l * DEC_BATCH, name="mem_attn_sample"))

        x = _merge_out(x, y_ssm, y_mla, y_mem, p, w_ssm_o_b, w_mla_o_b, w_mem_o_b, w_out_b,
                       norm_post[l][None], l, split=l == DEPTH - 1)

    x_p, x_s = x
    ckv_p, ckv_s, kro_p, kro_s = stacked
    p_re, p_im = _ssm_state_out(jnp.stack(ht_p), BATCH)
    s_re, s_im = _ssm_state_out(jnp.stack(ht_s), DEC_BATCH)
    mem_shape = (DEPTH, BATCH, N_MEM, MEM_HEADS, MEM_HEAD_DIM)
    return (x_p.reshape(BATCH, SEQ, D_MODEL), x_s.reshape(DEC_BATCH, DEC_SEQ, D_MODEL),
            p_re, p_im,
            ckv_p.reshape(DEPTH, BATCH, SEQ, KV_LORA), kro_p.reshape(DEPTH, BATCH, SEQ, QK_ROPE),
            mk_all.reshape(mem_shape), mv_all.reshape(mem_shape),
            s_re, s_im,
            ckv_s.reshape(DEPTH, DEC_BATCH, DEC_SEQ, KV_LORA), kro_s.reshape(DEPTH, DEC_BATCH, DEC_SEQ, QK_ROPE))
```
